```python
import jax, jax.numpy as jnp
from jax import lax
import numpy as np

D_MODEL = 1024
BATCH = 16
SEQ = 256
DEPTH = 4
DEC_BATCH = 4
DEC_SEQ = 1024
PAST_LEN = 512

GRID_W = 64
N_MIXERS = 3
N_ATTN = (DEPTH + 2) // 3
N_CONV = (DEPTH + 1) // 3
N_GLA = DEPTH // 3
NA_HEADS = 16
NA_HEAD_DIM = D_MODEL // NA_HEADS
NA_WIN_R = 8
NA_WIN_C = 16
CONV_WIDTH = 31
GLA_HEADS = 4
GLA_KEY_DIM = D_MODEL // 2
GLA_VAL_DIM = D_MODEL
GLA_DK = GLA_KEY_DIM // GLA_HEADS
GLA_DV = GLA_VAL_DIM // GLA_HEADS
GLA_GATE_RANK = 16
GLA_GATE_NORM = 16.0
GLA_CHUNK = 32
D_FF = 2816
FFN_CONV_WIDTH = 3
NORM_EPS = 1e-6
NEG_INF = -1e30

kernel_name = "hybrid_diffusion_prefix_trunk_step"


def rms_norm(x, g):
    xf = x.astype(jnp.float32)
    y = xf * lax.rsqrt(jnp.mean(xf * xf, axis=-1, keepdims=True) + NORM_EPS)
    return (y * g.astype(jnp.float32)).astype(x.dtype)


def layer_norm(x, g, b):
    xf = x.astype(jnp.float32)
    mu = jnp.mean(xf, axis=-1, keepdims=True)
    var = jnp.mean(jnp.square(xf - mu), axis=-1, keepdims=True)
    y = (xf - mu) * lax.rsqrt(var + NORM_EPS)
    return (y * g.astype(jnp.float32) + b.astype(jnp.float32)).astype(x.dtype)


def dwconv1d(x, w, b):
    ch = x.shape[-1]
    y = lax.conv_general_dilated(x, w[:, None, :].astype(x.dtype), (1,), 'SAME',
                                 dimension_numbers=('NWC', 'WIO', 'NWC'),
                                 feature_group_count=ch)
    return y + b.astype(x.dtype)


def adaln(cond, w, b):
    mod = jax.nn.silu(cond) @ w + b
    return [m[:, None, :] for m in jnp.split(mod, 6, axis=-1)]


def modulate(x, shift, scale):
    return x * (1 + scale) + shift


def _qkv(h, w_qkv, q_g, k_g):
    b, l, _ = h.shape
    qkv = (h @ w_qkv).reshape(b, l, 3, NA_HEADS, NA_HEAD_DIM)
    q = rms_norm(qkv[:, :, 0], q_g) * (NA_HEAD_DIM ** -0.5)
    k = rms_norm(qkv[:, :, 1], k_g)
    return q, k, qkv[:, :, 2]


def attn_context(h, w_qkv, w_o, q_g, k_g):
    b, l, _ = h.shape
    q, k, v = _qkv(h, w_qkv, q_g, k_g)
    s = jnp.einsum('bqhd,bkhd->bhqk', q, k, preferred_element_type=jnp.float32)
    p = jax.nn.softmax(s, axis=-1).astype(v.dtype)
    o = jnp.einsum('bhqk,bkhd->bqhd', p, v).reshape(b, l, D_MODEL)
    return o @ w_o, k, v


def na_latent(h, ctx_k, ctx_v, w_qkv, w_o, q_g, k_g, rpb):
    b, l, _ = h.shape
    rows = l // GRID_W
    wr = min(NA_WIN_R, rows)
    q, k, v = _qkv(h, w_qkv, q_g, k_g)
    qg = q.reshape(b, rows, GRID_W, NA_HEADS, NA_HEAD_DIM)
    kg = k.reshape(b, rows, GRID_W, NA_HEADS, NA_HEAD_DIM)
    vg = v.reshape(b, rows, GRID_W, NA_HEADS, NA_HEAD_DIM)
    r_idx = jnp.arange(rows)
    r_start = jnp.clip(r_idx - wr // 2, 0, rows - wr)
    row_idx = r_start[:, None] + jnp.arange(wr)[None, :]
    k_blk = kg[:, row_idx]
    v_blk = vg[:, row_idx]
    s_loc = jnp.einsum('brqhd,brikhd->bhrqik', qg, k_blk,
                       preferred_element_type=jnp.float32)
    cols = jnp.arange(GRID_W)
    c_start = jnp.clip(cols - NA_WIN_C // 2, 0, GRID_W - NA_WIN_C)
    in_win = (cols[None, :] >= c_start[:, None]) & (cols[None, :] < c_start[:, None] + NA_WIN_C)
    dr = row_idx - r_idx[:, None] + (NA_WIN_R - 1)
    dc = jnp.clip(cols[None, :] - cols[:, None] + (NA_WIN_C - 1), 0, 2 * NA_WIN_C - 2)
    bias = rpb.astype(jnp.float32)[:, dr[:, None, :, None], dc[None, :, None, :]]
    bias = jnp.where(in_win[None, None, :, None, :], bias, NEG_INF)
    s_loc = (s_loc + bias[None]).reshape(b, NA_HEADS, rows, GRID_W, wr * GRID_W)
    s_ctx = jnp.einsum('brqhd,bchd->bhrqc', qg, ctx_k, preferred_element_type=jnp.float32)
    p = jax.nn.softmax(jnp.concatenate([s_loc, s_ctx], axis=-1), axis=-1).astype(v.dtype)
    p_loc = p[..., :wr * GRID_W].reshape(b, NA_HEADS, rows, GRID_W, wr, GRID_W)
    p_ctx = p[..., wr * GRID_W:]
    o = (jnp.einsum('bhrqik,brikhd->brqhd', p_loc, v_blk)
         + jnp.einsum('bhrqc,bchd->brqhd', p_ctx, ctx_v))
    return o.reshape(b, l, D_MODEL) @ w_o


def conformer_conv(h, w_pw1, b_pw1, w_dw, b_dw, ln_g, ln_b, w_pw2, b_pw2):
    a, g = jnp.split(h @ w_pw1 + b_pw1, 2, axis=-1)
    u = a * jax.nn.sigmoid(g)
    u = dwconv1d(u, w_dw, b_dw)
    u = jax.nn.silu(layer_norm(u, ln_g, ln_b))
    return u @ w_pw2 + b_pw2


def gla_chunk_scan(q, k, v, g, s0):
    b, l, h, dk = q.shape
    dv = v.shape[-1]
    n = l // GLA_CHUNK
    def chunks(t):
        return t.reshape(b, n, GLA_CHUNK, h, t.shape[-1]).transpose(1, 0, 3, 2, 4).astype(jnp.float32)
    qc, kc, vc, gc = chunks(q), chunks(k), chunks(v), chunks(g)
    bcum = jnp.cumsum(gc, axis=3)
    b_last = bcum[..., -1:, :]
    causal = jnp.tril(jnp.ones((GLA_CHUNK, GLA_CHUNK), dtype=bool))
    diff = bcum[..., :, None, :] - bcum[..., None, :, :]
    decay = jnp.exp(jnp.where(causal[:, :, None], diff, -jnp.inf))
    att = jnp.einsum('nbhtd,nbhsd,nbhtsd->nbhts', qc, kc, decay)
    o_intra = jnp.einsum('nbhts,nbhsv->nbhtv', att, vc)
    q_in = qc * jnp.exp(bcum)
    k_out = kc * jnp.exp(b_last - bcum)
    def step(s, xs):
        qi, ki, vi, dl = xs
        o = jnp.einsum('bhtd,bhdv->bhtv', qi, s)
        s = s * dl[:, :, 0, :, None] + jnp.einsum('bhtd,bhtv->bhdv', ki, vi)
        return s, o
    s_fin, o_inter = lax.scan(step, s0.astype(jnp.float32), (q_in, k_out, vc, jnp.exp(b_last)))
    o = (o_intra + o_inter).transpose(1, 0, 3, 2, 4).reshape(b, l, h, dv)
    return o.astype(v.dtype), s_fin.astype(v.dtype)


def gla_mixer(h, s0_f, s0_b, w_q, w_k, w_v, w_g, w_gk1, w_gk2, b_gk, o_g, w_o):
    b, l, _ = h.shape
    q = (h @ w_q).reshape(b, l, GLA_HEADS, GLA_DK) * (GLA_DK ** -0.5)
    k = (h @ w_k).reshape(b, l, GLA_HEADS, GLA_DK)
    v = (h @ w_v).reshape(b, l, GLA_HEADS, GLA_DV)
    def log_gate(d):
        z = (h @ w_gk1[d]) @ w_gk2[d] + b_gk[d]
        return (jax.nn.log_sigmoid(z.astype(jnp.float32)) / GLA_GATE_NORM).reshape(b, l, GLA_HEADS, GLA_DK)
    flip = lambda t: jnp.flip(t, axis=1)
    o_f, st_f = gla_chunk_scan(q, k, v, log_gate(0), s0_f)
    o_b, st_b = gla_chunk_scan(flip(q), flip(k), flip(v), flip(log_gate(1)), s0_b)
    o = rms_norm(o_f + flip(o_b), o_g) * jax.nn.silu((h @ w_g).reshape(b, l, GLA_HEADS, GLA_DV))
    return o.reshape(b, l, GLA_VAL_DIM) @ w_o, st_f, st_b


def conv_ffn(h, w_up, b_up, w_dw, b_dw, w_down, b_down):
    u = dwconv1d(h @ w_up + b_up, w_dw, b_dw)
    a, g = jnp.split(u, 2, axis=-1)
    return (jax.nn.silu(g) * a) @ w_down + b_down


def setup_inputs(seed: int = 0) -> dict:
    key = jax.random.key(seed)
    ks = iter(jax.random.split(key, 40))
    def nrm(shape, scale):
        return jax.random.normal(next(ks), shape, jnp.float32) * scale
    D = D_MODEL
    inp = {}
    inp['x_prompt'] = nrm((BATCH, SEQ, D), 1.0)
    inp['x_sample'] = nrm((DEC_BATCH, DEC_SEQ, D), 1.0)
    inp['c'] = nrm((DEC_BATCH, D), 1.0)
    inp['cache_attn_k'] = nrm((DEC_BATCH, N_ATTN, PAST_LEN, NA_HEADS, NA_HEAD_DIM), 1.0)
    inp['cache_attn_v'] = nrm((DEC_BATCH, N_ATTN, PAST_LEN, NA_HEADS, NA_HEAD_DIM), 1.0)
    inp['state_gla_fwd'] = nrm((DEC_BATCH, N_GLA, GLA_HEADS, GLA_DK, GLA_DV), 1.0)
    inp['state_gla_bwd'] = nrm((DEC_BATCH, N_GLA, GLA_HEADS, GLA_DK, GLA_DV), 1.0)
    inp['c_ctx'] = nrm((D,), 1.0)
    inp['mod_w'] = nrm((DEPTH, D, 6 * D), 0.5 * D ** -0.5)
    inp['mod_b'] = nrm((DEPTH, 6 * D), 0.01)
    inp['norm1_g'] = 1.0 + nrm((DEPTH, D), 0.02)
    inp['norm2_g'] = 1.0 + nrm((DEPTH, D), 0.02)
    inp['attn_w_qkv'] = nrm((N_ATTN, D, 3 * D), D ** -0.5)
    inp['attn_w_o'] = nrm((N_ATTN, D, D), D ** -0.5)
    inp['attn_q_norm'] = 1.0 + nrm((N_ATTN, NA_HEAD_DIM), 0.02)
    inp['attn_k_norm'] = 1.0 + nrm((N_ATTN, NA_HEAD_DIM), 0.02)
    inp['attn_rpb'] = nrm((N_ATTN, NA_HEADS, 2 * NA_WIN_R - 1, 2 * NA_WIN_C - 1), 0.1)
    inp['conv_w_pw1'] = nrm((N_CONV, D, 2 * D), D ** -0.5)
    inp['conv_b_pw1'] = nrm((N_CONV, 2 * D), 0.01)
    inp['conv_w_dw'] = nrm((N_CONV, CONV_WIDTH, D), CONV_WIDTH ** -0.5)
    inp['conv_b_dw'] = nrm((N_CONV, D), 0.01)
    inp['conv_ln_g'] = 1.0 + nrm((N_CONV, D), 0.02)
    inp['conv_ln_b'] = nrm((N_CONV, D), 0.01)
    inp['conv_w_pw2'] = nrm((N_CONV, D, D), D ** -0.5)
    inp['conv_b_pw2'] = nrm((N_CONV, D), 0.01)
    inp['gla_w_q'] = nrm((N_GLA, D, GLA_KEY_DIM), D ** -0.5)
    inp['gla_w_k'] = nrm((N_GLA, D, GLA_KEY_DIM), D ** -0.5)
    inp['gla_w_v'] = nrm((N_GLA, D, GLA_VAL_DIM), D ** -0.5)
    inp['gla_w_g'] = nrm((N_GLA, D, GLA_VAL_DIM), D ** -0.5)
    inp['gla_w_gk1'] = nrm((N_GLA, 2, D, GLA_GATE_RANK), D ** -0.5)
    inp['gla_w_gk2'] = nrm((N_GLA, 2, GLA_GATE_RANK, GLA_KEY_DIM), GLA_GATE_RANK ** -0.5)
    inp['gla_b_gk'] = nrm((N_GLA, 2, GLA_KEY_DIM), 0.1)
    inp['gla_o_norm'] = 1.0 + nrm((N_GLA, GLA_DV), 0.02)
    inp['gla_w_o'] = nrm((N_GLA, GLA_VAL_DIM, D), GLA_VAL_DIM ** -0.5)
    inp['ffn_w_up'] = nrm((DEPTH, D, 2 * D_FF), D ** -0.5)
    inp['ffn_b_up'] = nrm((DEPTH, 2 * D_FF), 0.01)
    inp['ffn_w_dw'] = nrm((DEPTH, FFN_CONV_WIDTH, 2 * D_FF), FFN_CONV_WIDTH ** -0.5)
    inp['ffn_b_dw'] = nrm((DEPTH, 2 * D_FF), 0.01)
    inp['ffn_w_down'] = nrm((DEPTH, D_FF, D), D_FF ** -0.5)
    inp['ffn_b_down'] = nrm((DEPTH, D), 0.01)
    return inp


def reference(x_prompt, x_sample, c, cache_attn_k, cache_attn_v, state_gla_fwd, state_gla_bwd, c_ctx,
              mod_w, mod_b, norm1_g, norm2_g,
              attn_w_qkv, attn_w_o, attn_q_norm, attn_k_norm, attn_rpb,
              conv_w_pw1, conv_b_pw1, conv_w_dw, conv_b_dw, conv_ln_g, conv_ln_b, conv_w_pw2, conv_b_pw2,
              gla_w_q, gla_w_k, gla_w_v, gla_w_g, gla_w_gk1, gla_w_gk2, gla_b_gk, gla_o_norm, gla_w_o,
              ffn_w_up, ffn_b_up, ffn_w_dw, ffn_b_dw, ffn_w_down, ffn_b_down):
    xp, xs = x_prompt, x_sample
    new_k, new_v, new_sf, new_sb = [], [], [], []
    for i in range(DEPTH):
        kind, j = i % N_MIXERS, i // N_MIXERS
        sh1p, sc1p, g1p, sh2p, sc2p, g2p = adaln(c_ctx[None, :], mod_w[i], mod_b[i])
        sh1s, sc1s, g1s, sh2s, sc2s, g2s = adaln(c, mod_w[i], mod_b[i])
        hp = modulate(rms_norm(xp, norm1_g[i]), sh1p, sc1p)
        hs = modulate(rms_norm(xs, norm1_g[i]), sh1s, sc1s)
        if kind == 0:
            op, kp, vp = attn_context(hp, attn_w_qkv[j], attn_w_o[j], attn_q_norm[j], attn_k_norm[j])
            os_ = na_latent(hs, cache_attn_k[:, j], cache_attn_v[:, j], attn_w_qkv[j], attn_w_o[j],
                            attn_q_norm[j], attn_k_norm[j], attn_rpb[j])
            new_k.append(kp)
            new_v.append(vp)
        elif kind == 1:
            cw = (conv_w_pw1[j], conv_b_pw1[j], conv_w_dw[j], conv_b_dw[j], conv_ln_g[j], conv_ln_b[j],
                  conv_w_pw2[j], conv_b_pw2[j])
            op = conformer_conv(hp, *cw)
            os_ = conformer_conv(hs, *cw)
        else:
            gw = (gla_w_q[j], gla_w_k[j], gla_w_v[j], gla_w_g[j], gla_w_gk1[j], gla_w_gk2[j], gla_b_gk[j],
                  gla_o_norm[j], gla_w_o[j])
            zero = jnp.zeros((xp.shape[0], GLA_HEADS, GLA_DK, GLA_DV), xp.dtype)
            op, sfp, sbp = gla_mixer(hp, zero, zero, *gw)
            os_, _, _ = gla_mixer(hs, state_gla_fwd[:, j], state_gla_bwd[:, j], *gw)
            new_sf.append(sfp)
            new_sb.append(sbp)
        xp = xp + g1p * op
        xs = xs + g1s * os_
        fw = (ffn_w_up[i], ffn_b_up[i], ffn_w_dw[i], ffn_b_dw[i], ffn_w_down[i], ffn_b_down[i])
        xp = xp + g2p * conv_ffn(modulate(rms_norm(xp, norm2_g[i]), sh2p, sc2p), *fw)
        xs = xs + g2s * conv_ffn(modulate(rms_norm(xs, norm2_g[i]), sh2s, sc2s), *fw)
    new_attn_k = jnp.stack(new_k, axis=1)
    new_attn_v = jnp.stack(new_v, axis=1)
    new_gla_fwd = jnp.stack(new_sf, axis=1)
    new_gla_bwd = jnp.stack(new_sb, axis=1)
    return (xp, xs, new_attn_k, new_attn_v, new_gla_fwd, new_gla_bwd)
```

```python
import functools

import numpy as np
import jax
import jax.numpy as jnp
from jax import lax
from jax.experimental import pallas as pl
from jax.experimental.pallas import tpu as pltpu

F32 = jnp.float32
BF16 = jnp.bfloat16

D_MODEL = 1024
BATCH = 16
SEQ = 256
DEPTH = 4
DEC_BATCH = 4
DEC_SEQ = 1024
PAST_LEN = 512
GRID_W = 64
N_MIXERS = 3
NA_HEADS = 16
NA_HEAD_DIM = 64
NA_WIN_R = 8
NA_WIN_C = 16
CONV_WIDTH = 31
GLA_HEADS = 4
GLA_KEY_DIM = 512
GLA_VAL_DIM = 1024
GLA_DK = 128
GLA_DV = 256
GLA_GATE_RANK = 16
GLA_GATE_NORM = 16.0
D_FF = 2816
NORM_EPS = 1e-6
NEG_INF = -1e30

TM = 1024
N_CTX_TOK = BATCH * SEQ
N_LAT_TOK = DEC_BATCH * DEC_SEQ
N_TOK = N_CTX_TOK + N_LAT_TOK
N_CTX_GROUPS = N_CTX_TOK // TM
N_GROUPS = N_TOK // TM
N_COND = 8
TF = 256
GLA_CHUNK = 256
GLA_PROJ_N = 3200
VMEM_LIMIT = 60 * 1024 * 1024


def _cparams(sem):
    return pltpu.CompilerParams(dimension_semantics=sem, vmem_limit_bytes=VMEM_LIMIT)


def _sigmoid(x):
    return 1.0 / (1.0 + jnp.exp(-x))


def _norm_mod(x, g, shift, scale):
    ms = jnp.mean(x * x, axis=-1, keepdims=True)
    y = x * lax.rsqrt(ms + NORM_EPS) * g
    return y * (1.0 + scale) + shift


def _mod_slice(mod_ref, idx):
    return mod_ref[0, :, idx * D_MODEL:(idx + 1) * D_MODEL]


def _mod_kernel(cond_ref, w_ref, b_ref, o_ref):
    c = cond_ref[...]
    s = (c * _sigmoid(c)).astype(BF16)
    o_ref[0] = jnp.dot(s, w_ref[0].astype(BF16), preferred_element_type=F32) + b_ref[0]


def _modulation(cond, mod_w, mod_b):
    tn = 1536
    n = 6 * D_MODEL
    return pl.pallas_call(
        _mod_kernel,
        grid=(DEPTH, n // tn),
        in_specs=[
            pl.BlockSpec((N_COND, D_MODEL), lambda l, j: (0, 0)),
            pl.BlockSpec((1, D_MODEL, tn), lambda l, j: (l, 0, j)),
            pl.BlockSpec((1, 1, tn), lambda l, j: (l, 0, j)),
        ],
        out_specs=pl.BlockSpec((1, N_COND, tn), lambda l, j: (l, 0, j)),
        out_shape=jax.ShapeDtypeStruct((DEPTH, N_COND, n), F32),
        compiler_params=_cparams(("arbitrary", "arbitrary")),
        name="modulation",
    )(cond, mod_w, mod_b.reshape(DEPTH, 1, n))


def _proj_prologue(x_ref, g_ref, mod_ref, h_ref):
    @pl.when(pl.program_id(1) == 0)
    def _():
        h = _norm_mod(x_ref[...], g_ref[...], _mod_slice(mod_ref, 0), _mod_slice(mod_ref, 1))
        h_ref[...] = h.astype(BF16)


def _head_rms(acc, gain, hsum_ref, hexp_ref):
    ms = jnp.dot((acc * acc).astype(BF16), hsum_ref[...], preferred_element_type=F32)
    inv = lax.rsqrt(ms + NORM_EPS)
    hi = inv.astype(BF16)
    lo = (inv - hi.astype(F32)).astype(BF16)
    inv_full = jnp.dot(jnp.concatenate([hi, lo], axis=1), hexp_ref[...], preferred_element_type=F32)
    return acc * inv_full * gain


def _qkv_kernel(x_ref, g_ref, mod_ref, w_ref, gain_ref, hsum_ref, hexp_ref,
                q_ref, k_ref, v_ref, h_ref):
    _proj_prologue(x_ref, g_ref, mod_ref, h_ref)
    j = pl.program_id(1)
    acc = jnp.dot(h_ref[...], w_ref[...], preferred_element_type=F32)

    @pl.when(j == 0)
    def _():
        q_ref[...] = _head_rms(acc, gain_ref[...], hsum_ref, hexp_ref).astype(q_ref.dtype)

    @pl.when(j == 1)
    def _():
        k_ref[...] = _head_rms(acc, gain_ref[...], hsum_ref, hexp_ref).astype(k_ref.dtype)

    @pl.when(j == 2)
    def _():
        v_ref[...] = acc.astype(v_ref.dtype)


def _head_matrices():
    lane = np.arange(D_MODEL)
    hsum = np.zeros((D_MODEL, 128), np.float32)
    hsum[lane, lane // NA_HEAD_DIM] = 1.0 / NA_HEAD_DIM
    hexp = np.zeros((128, D_MODEL), np.float32)
    hexp[lane // NA_HEAD_DIM, lane] = 1.0
    return jnp.asarray(hsum, BF16), jnp.asarray(np.concatenate([hexp, hexp], 0), BF16)


def _qkv_proj(x, norm_g, mod, w_qkv, gain, group0, n_groups, kv_dtype):
    hsum, hexp = _head_matrices()
    ntok = n_groups * TM
    row = lambda g, j: (g, 0)
    return pl.pallas_call(
        _qkv_kernel,
        grid=(n_groups, 3),
        in_specs=[
            pl.BlockSpec((TM, D_MODEL), lambda g, j: (g + group0, 0)),
            pl.BlockSpec((1, D_MODEL), lambda g, j: (0, 0)),
            pl.BlockSpec((1, 1, 6 * D_MODEL), lambda g, j: (g + group0, 0, 0)),
            pl.BlockSpec((D_MODEL, D_MODEL), lambda g, j: (0, j)),
            pl.BlockSpec((1, D_MODEL), lambda g, j: (0, j)),
            pl.BlockSpec((D_MODEL, 128), lambda g, j: (0, 0)),
            pl.BlockSpec((256, D_MODEL), lambda g, j: (0, 0)),
        ],
        out_specs=[pl.BlockSpec((TM, D_MODEL), row)] * 3,
        out_shape=[jax.ShapeDtypeStruct((ntok, D_MODEL), BF16),
                   jax.ShapeDtypeStruct((ntok, D_MODEL), kv_dtype),
                   jax.ShapeDtypeStruct((ntok, D_MODEL), kv_dtype)],
        scratch_shapes=[pltpu.VMEM((TM, D_MODEL), BF16)],
        compiler_params=_cparams(("arbitrary", "arbitrary")),
        name="qkv_proj",
    )(x, norm_g, mod, w_qkv, gain, hsum, hexp)


def _glu_kernel(x_ref, g_ref, mod_ref, wa_ref, wg_ref, ba_ref, bg_ref, u_ref, h_ref):
    _proj_prologue(x_ref, g_ref, mod_ref, h_ref)
    h = h_ref[...]
    a = jnp.dot(h, wa_ref[...], preferred_element_type=F32) + ba_ref[...]
    g = jnp.dot(h, wg_ref[...], preferred_element_type=F32) + bg_ref[...]
    u_ref[...] = a * _sigmoid(g)


def _glu_proj(x, norm_g, mod, w_pw1, b_pw1):
    tn = 512
    nj = D_MODEL // tn
    b = b_pw1.reshape(1, 2 * D_MODEL)
    return pl.pallas_call(
        _glu_kernel,
        grid=(N_GROUPS, nj),
        in_specs=[
            pl.BlockSpec((TM, D_MODEL), lambda g, j: (g, 0)),
            pl.BlockSpec((1, D_MODEL), lambda g, j: (0, 0)),
            pl.BlockSpec((1, 1, 6 * D_MODEL), lambda g, j: (g, 0, 0)),
            pl.BlockSpec((D_MODEL, tn), lambda g, j: (0, j)),
            pl.BlockSpec((D_MODEL, tn), lambda g, j: (0, j + nj)),
            pl.BlockSpec((1, tn), lambda g, j: (0, j)),
            pl.BlockSpec((1, tn), lambda g, j: (0, j + nj)),
        ],
        out_specs=pl.BlockSpec((TM, tn), lambda g, j: (g, j)),
        out_shape=jax.ShapeDtypeStruct((N_TOK, D_MODEL), F32),
        scratch_shapes=[pltpu.VMEM((TM, D_MODEL), BF16)],
        compiler_params=_cparams(("arbitrary", "arbitrary")),
        name="glu_proj",
    )(x, norm_g, mod, w_pw1, w_pw1, b, b)


def _scaled_proj_kernel(x_ref, g_ref, mod_ref, w_ref, cs_ref, y_ref, h_ref):
    _proj_prologue(x_ref, g_ref, mod_ref, h_ref)
    y_ref[...] = jnp.dot(h_ref[...], w_ref[...], preferred_element_type=F32) * cs_ref[...]


def _scaled_proj(x, norm_g, mod, w, colscale):
    n = w.shape[1]
    tn = 640
    return pl.pallas_call(
        _scaled_proj_kernel,
        grid=(N_GROUPS, n // tn),
        in_specs=[
            pl.BlockSpec((TM, D_MODEL), lambda g, j: (g, 0)),
            pl.BlockSpec((1, D_MODEL), lambda g, j: (0, 0)),
            pl.BlockSpec((1, 1, 6 * D_MODEL), lambda g, j: (g, 0, 0)),
            pl.BlockSpec((D_MODEL, tn), lambda g, j: (0, j)),
            pl.BlockSpec((1, tn), lambda g, j: (0, j)),
        ],
        out_specs=pl.BlockSpec((TM, tn), lambda g, j: (g, j)),
        out_shape=jax.ShapeDtypeStruct((N_TOK, n), F32),
        scratch_shapes=[pltpu.VMEM((TM, D_MODEL), BF16)],
        compiler_params=_cparams(("arbitrary", "arbitrary")),
        name="gla_proj",
    )(x, norm_g, mod, w, colscale)


def _oproj_kernel(x_ref, a_ref, w_ref, mod_ref, o_ref):
    r = jnp.dot(a_ref[...], w_ref[...], preferred_element_type=F32)
    o_ref[...] = x_ref[...] + _mod_slice(mod_ref, 2) * r


def _out_proj(x, a, w, mod):
    return pl.pallas_call(
        _oproj_kernel,
        grid=(N_GROUPS,),
        in_specs=[
            pl.BlockSpec((TM, D_MODEL), lambda g: (g, 0)),
            pl.BlockSpec((TM, D_MODEL), lambda g: (g, 0)),
            pl.BlockSpec((D_MODEL, D_MODEL), lambda g: (0, 0)),
            pl.BlockSpec((1, 1, 6 * D_MODEL), lambda g: (g, 0, 0)),
        ],
        out_specs=pl.BlockSpec((TM, D_MODEL), lambda g: (g, 0)),
        out_shape=jax.ShapeDtypeStruct((N_TOK, D_MODEL), F32),
        compiler_params=_cparams(("arbitrary",)),
        name="out_proj",
    )(x, a, w, mod)


def _gla_oproj_kernel(x_ref, o_ref, gz_ref, og_ref, w_ref, mod_ref, out_ref):
    og = og_ref[...]
    parts = []
    for h in range(GLA_HEADS):
        oh = o_ref[:, h * GLA_DV:(h + 1) * GLA_DV]
        ms = jnp.mean(oh * oh, axis=-1, keepdims=True)
        parts.append(oh * lax.rsqrt(ms + NORM_EPS) * og)
    y = jnp.concatenate(parts, axis=1)
    gz = gz_ref[...]
    a = (y * (gz * _sigmoid(gz))).astype(BF16)
    r = jnp.dot(a, w_ref[...], preferred_element_type=F32)
    out_ref[...] = x_ref[...] + _mod_slice(mod_ref, 2) * r


def _gla_out_proj(x, o, y, o_norm, w, mod):
    return pl.pallas_call(
        _gla_oproj_kernel,
        grid=(N_GROUPS,),
        in_specs=[
            pl.BlockSpec((TM, D_MODEL), lambda g: (g, 0)),
            pl.BlockSpec((TM, GLA_VAL_DIM), lambda g: (g, 0)),
            pl.BlockSpec((TM, GLA_VAL_DIM), lambda g: (g, 2)),
            pl.BlockSpec((1, GLA_DV), lambda g: (0, 0)),
            pl.BlockSpec((GLA_VAL_DIM, D_MODEL), lambda g: (0, 0)),
            pl.BlockSpec((1, 1, 6 * D_MODEL), lambda g: (g, 0, 0)),
        ],
        out_specs=pl.BlockSpec((TM, D_MODEL), lambda g: (g, 0)),
        out_shape=jax.ShapeDtypeStruct((N_TOK, D_MODEL), F32),
        compiler_params=_cparams(("arbitrary",)),
        name="gla_out_proj",
    )(x, o, y, o_norm, w, mod)


def _ffn_kernel(x_ref, g_ref, mod_ref, wup_ref, bup_ref, wdw_ref, bdw_ref, wdn_ref, bdn_ref,
                out_ref, h_ref, acc_ref):
    x = x_ref[...]
    h_ref[...] = _norm_mod(x, g_ref[...], _mod_slice(mod_ref, 3), _mod_slice(mod_ref, 4)).astype(BF16)
    seq_len = jnp.where(pl.program_id(0) < N_CTX_GROUPS, SEQ, DEC_SEQ)
    pos = lax.broadcasted_iota(jnp.int32, (TM, 1), 0) & (seq_len - 1)
    first = pos == 0
    last = pos == seq_len - 1
    acc_ref[...] = jnp.zeros_like(acc_ref)

    def conv(u, col):
        w = wdw_ref[:, pl.ds(col, TF)]
        prev = jnp.where(first, 0.0, pltpu.roll(u, 1, 0))
        nxt = jnp.where(last, 0.0, pltpu.roll(u, TM - 1, 0))
        return w[0:1] * prev + w[1:2] * u + w[2:3] * nxt + bdw_ref[:, pl.ds(col, TF)]

    def body(c, carry):
        ca = pl.multiple_of(c * TF, TF)
        cg = pl.multiple_of(D_FF + c * TF, TF)
        h = h_ref[...]
        ua = jnp.dot(h, wup_ref[:, pl.ds(ca, TF)], preferred_element_type=F32) + bup_ref[:, pl.ds(ca, TF)]
        ug = jnp.dot(h, wup_ref[:, pl.ds(cg, TF)], preferred_element_type=F32) + bup_ref[:, pl.ds(cg, TF)]
        a = conv(ua, ca)
        g = conv(ug, cg)
        act = (g * _sigmoid(g) * a).astype(BF16)
        acc_ref[...] += jnp.dot(act, wdn_ref[pl.ds(ca, TF), :], preferred_element_type=F32)
        return carry

    lax.fori_loop(0, D_FF // TF, body, 0)
    out_ref[...] = x + _mod_slice(mod_ref, 5) * (acc_ref[...] + bdn_ref[...])


def _ffn(x, norm_g, mod, w_up, b_up, w_dw, b_dw, w_down, b_down):
    const = lambda g: (0, 0)
    resident = dict(pipeline_mode=pl.Buffered(1))
    return pl.pallas_call(
        _ffn_kernel,
        grid=(N_GROUPS,),
        in_specs=[
            pl.BlockSpec((TM, D_MODEL), lambda g: (g, 0)),
            pl.BlockSpec((1, D_MODEL), const),
            pl.BlockSpec((1, 1, 6 * D_MODEL), lambda g: (g, 0, 0)),
            pl.BlockSpec((D_MODEL, 2 * D_FF), const, **resident),
            pl.BlockSpec((1, 2 * D_FF), const),
            pl.BlockSpec((3, 2 * D_FF), const),
            pl.BlockSpec((1, 2 * D_FF), const),
            pl.BlockSpec((D_FF, D_MODEL), const, **resident),
            pl.BlockSpec((1, D_MODEL), const),
        ],
        out_specs=pl.BlockSpec((TM, D_MODEL), lambda g: (g, 0)),
        out_shape=jax.ShapeDtypeStruct((N_TOK, D_MODEL), F32),
        scratch_shapes=[pltpu.VMEM((TM, D_MODEL), BF16), pltpu.VMEM((TM, D_MODEL), F32)],
        compiler_params=_cparams(("arbitrary",)),
        name="conv_ffn",
    )(x, norm_g, mod, w_up, b_up.reshape(1, -1), w_dw, b_dw.reshape(1, -1), w_down, b_down.reshape(1, -1))


def _pair_queries(q):
    lane = lax.broadcasted_iota(jnp.int32, q.shape, 1)
    zero = jnp.zeros_like(q)
    return jnp.concatenate([jnp.where(lane < NA_HEAD_DIM, q, zero),
                            jnp.where(lane < NA_HEAD_DIM, zero, q)], axis=0)


def _pair_merge(o2):
    n = o2.shape[0] // 2
    lane = lax.broadcasted_iota(jnp.int32, (n, o2.shape[1]), 1)
    return jnp.where(lane < NA_HEAD_DIM, o2[:n], o2[n:])


def _nt_dot(a, b):
    return lax.dot_general(a, b, (((1,), (1,)), ((), ())), preferred_element_type=F32)


def _ctx_attn_kernel(q_ref, k_ref, v_ref, o_ref):
    for b in range(TM // SEQ):
        rows = slice(b * SEQ, (b + 1) * SEQ)
        q2 = _pair_queries(q_ref[rows, :])
        s = _nt_dot(q2, k_ref[rows, :].astype(BF16))
        p = jnp.exp(s - jnp.max(s, axis=-1, keepdims=True))
        l = jnp.sum(p, axis=-1, keepdims=True)
        o2 = jnp.dot(p.astype(BF16), v_ref[rows, :].astype(BF16), preferred_element_type=F32) / l
        o_ref[rows, :] = _pair_merge(o2).astype(o_ref.dtype)


def _ctx_attention(q, k, v):
    spec = pl.BlockSpec((TM, 128), lambda g, hp: (g, hp))
    return pl.pallas_call(
        _ctx_attn_kernel,
        grid=(N_CTX_GROUPS, NA_HEADS // 2),
        in_specs=[spec, spec, spec],
        out_specs=spec,
        out_shape=jax.ShapeDtypeStruct((N_TOK, D_MODEL), BF16),
        compiler_params=_cparams(("arbitrary", "arbitrary")),
        name="ctx_attention",
    )(q, k, v)


_NA_BLOCKS = ((0, 8, 0), (0, 12, 1), (4, 12, 1), (8, 8, 2))


def _na_attn_kernel(q_ref, k_ref, v_ref, ck_ref, cv_ref, ba_ref, bm_ref, bc_ref, o_in_ref, o_ref):
    del o_in_ref
    bias_refs = (ba_ref, bm_ref, bc_ref)
    ck = ck_ref[0, 0].astype(BF16)
    cv = cv_ref[0, 0].astype(BF16)
    for blk, (row0, nrows, bidx) in enumerate(_NA_BLOCKS):
        rows = slice(blk * 4 * GRID_W, (blk + 1) * 4 * GRID_W)
        keys = slice(row0 * GRID_W, (row0 + nrows) * GRID_W)
        q2 = _pair_queries(q_ref[rows, :])
        bias = bias_refs[bidx][...]
        s_loc = _nt_dot(q2, k_ref[keys, :]) + bias.reshape(2 * 4 * GRID_W, nrows * GRID_W)
        s_ctx = _nt_dot(q2, ck)
        m = jnp.maximum(jnp.max(s_loc, axis=-1, keepdims=True), jnp.max(s_ctx, axis=-1, keepdims=True))
        p_loc = jnp.exp(s_loc - m)
        p_ctx = jnp.exp(s_ctx - m)
        l = jnp.sum(p_loc, axis=-1, keepdims=True) + jnp.sum(p_ctx, axis=-1, keepdims=True)
        o2 = (jnp.dot(p_loc.astype(BF16), v_ref[keys, :], preferred_element_type=F32)
              + jnp.dot(p_ctx.astype(BF16), cv, preferred_element_type=F32)) / l
        o_ref[rows, :] = _pair_merge(o2).astype(o_ref.dtype)


def _na_bias_tables(rpb):
    cols = np.arange(GRID_W)
    c_start = np.clip(cols - NA_WIN_C // 2, 0, GRID_W - NA_WIN_C)
    in_win = (cols[None, :] >= c_start[:, None]) & (cols[None, :] < c_start[:, None] + NA_WIN_C)
    dc = np.clip(cols[None, :] - cols[:, None] + (NA_WIN_C - 1), 0, 2 * NA_WIN_C - 2)
    rows_total = DEC_SEQ // GRID_W
    tables = []
    for blk in (0, 1, 3):
        row0, nrows, _ = _NA_BLOCKS[blk]
        r = blk * 4 + np.arange(4)
        r_start = np.clip(r - NA_WIN_R // 2, 0, rows_total - NA_WIN_R)
        krow = row0 + np.arange(nrows)
        valid_r = (krow[None, :] >= r_start[:, None]) & (krow[None, :] < r_start[:, None] + NA_WIN_R)
        dr = np.clip(krow[None, :] - r[:, None] + (NA_WIN_R - 1), 0, 2 * NA_WIN_R - 2)
        shape = (4, GRID_W, nrows, GRID_W)
        dr_f = np.broadcast_to(dr[:, None, :, None], shape).reshape(4 * GRID_W, nrows * GRID_W)
        dc_f = np.broadcast_to(dc[None, :, None, :], shape).reshape(4 * GRID_W, nrows * GRID_W)
        ok = np.broadcast_to(valid_r[:, None, :, None] & in_win[None, :, None, :], shape)
        ok = ok.reshape(4 * GRID_W, nrows * GRID_W)
        tables.append(jnp.where(ok[None], rpb.astype(F32)[:, dr_f, dc_f], NEG_INF))
    return tables


def _na_attention(q, k, v, cache_k, cache_v, layer_j, bias_tables, o_ctx):
    ba, bm, bc = bias_tables
    tok = pl.BlockSpec((TM, 128), lambda hp, b: (b, hp))
    cache = pl.BlockSpec((1, 1, PAST_LEN, 128), lambda hp, b: (b, layer_j, 0, hp))
    bias = lambda n: pl.BlockSpec((2, 4 * GRID_W, n), lambda hp, b: (hp, 0, 0))
    return pl.pallas_call(
        _na_attn_kernel,
        grid=(NA_HEADS // 2, DEC_BATCH),
        in_specs=[tok, tok, tok, cache, cache, bias(8 * GRID_W), bias(12 * GRID_W), bias(8 * GRID_W),
                  pl.BlockSpec(memory_space=pl.ANY)],
        out_specs=pl.BlockSpec((TM, 128), lambda hp, b: (b + N_CTX_GROUPS, hp)),
        out_shape=jax.ShapeDtypeStruct((N_TOK, D_MODEL), BF16),
        input_output_aliases={8: 0},
        compiler_params=_cparams(("arbitrary", "arbitrary")),
        name="na_attention",
    )(q, k, v, cache_k, cache_v, ba, bm, bc, o_ctx)


CONV_PAD = 16
CONV_ROWS = 32
CONV_LANES = 256


def _conv_tail_kernel(x_ref, u_ref, wdw_ref, bdw_ref, lng_ref, lnb_ref, w_ref, b_ref, mod_ref,
                      o_ref, pad_ref, a_ref, *, seq_len):
    zeros = jnp.zeros((CONV_PAD, D_MODEL), F32)
    pad_ref[0:CONV_PAD, :] = zeros
    pad_ref[CONV_PAD + seq_len:, :] = zeros
    pad_ref[CONV_PAD:CONV_PAD + seq_len, :] = u_ref[...]
    half = CONV_WIDTH // 2

    win = CONV_ROWS + 8

    def rows(i, carry):
        r0 = pl.multiple_of(i * CONV_ROWS, CONV_ROWS)
        strips = []
        for l0 in range(0, D_MODEL, CONV_LANES):
            lanes = slice(l0, l0 + CONV_LANES)
            acc = jnp.zeros((CONV_ROWS, CONV_LANES), F32) + bdw_ref[:, lanes]
            for b in range(8):
                z = None
                for a in range(-2, 2):
                    t = 8 * a + b + half
                    if not 0 <= t < CONV_WIDTH:
                        continue
                    term = wdw_ref[t:t + 1, lanes] * pad_ref[pl.ds(r0 + (CONV_PAD + 8 * a), win), lanes]
                    z = term if z is None else z + term
                if b:
                    z = pltpu.roll(z, win - b, 0)
                acc = acc + z[:CONV_ROWS]
            strips.append(acc)
        acc = jnp.concatenate(strips, axis=1)
        mu = jnp.mean(acc, axis=-1, keepdims=True)
        cen = acc - mu
        var = jnp.mean(cen * cen, axis=-1, keepdims=True)
        y = cen * lax.rsqrt(var + NORM_EPS) * lng_ref[...] + lnb_ref[...]
        a_ref[pl.ds(r0, CONV_ROWS), :] = (y * _sigmoid(y)).astype(BF16)
        return carry

    lax.fori_loop(0, seq_len // CONV_ROWS, rows, 0)
    r = jnp.dot(a_ref[...], w_ref[...], preferred_element_type=F32) + b_ref[...]
    o_ref[...] = x_ref[...] + _mod_slice(mod_ref, 2) * r


def _conv_tail(x, u, w_dw, b_dw, ln_g, ln_b, w_pw2, b_pw2, mod, seq_len, seq0, n_seq):
    per_group = TM // seq_len
    tok = pl.BlockSpec((seq_len, D_MODEL), lambda s: (s + seq0, 0))
    const = lambda s: (0, 0)
    vec = pl.BlockSpec((1, D_MODEL), const)
    return pl.pallas_call(
        functools.partial(_conv_tail_kernel, seq_len=seq_len),
        grid=(n_seq,),
        in_specs=[tok, tok, pl.BlockSpec((CONV_WIDTH, D_MODEL), const), vec, vec, vec,
                  pl.BlockSpec((D_MODEL, D_MODEL), const), vec,
                  pl.BlockSpec((1, 1, 6 * D_MODEL), lambda s: ((s + seq0) // per_group, 0, 0))],
        out_specs=tok,
        out_shape=jax.ShapeDtypeStruct((N_TOK, D_MODEL), F32),
        scratch_shapes=[pltpu.VMEM((seq_len + 2 * CONV_PAD, D_MODEL), F32),
                        pltpu.VMEM((seq_len, D_MODEL), BF16)],
        input_output_aliases={0: 0},
        compiler_params=_cparams(("arbitrary",)),
        name="conv_tail",
    )(x, u, w_dw, b_dw.reshape(1, -1), ln_g.reshape(1, -1), ln_b.reshape(1, -1), w_pw2,
      b_pw2.reshape(1, -1), mod)


def _log_sigmoid(z):
    return -(jnp.maximum(-z, 0.0) + jnp.log(1.0 + jnp.exp(-jnp.abs(z))))


def _gla_scan_kernel(*refs, seq_len, has_init, emit_state):
    it = iter(refs)
    q_ref, k_ref, v_ref, r_ref, w2_ref, bgk_ref = (next(it) for _ in range(6))
    s0_refs = (next(it), next(it)) if has_init else None
    o_in_ref = next(it)
    del o_in_ref
    o_ref = next(it)
    st_out = (next(it), next(it)) if emit_state else None
    g_ref, cum_ref, att_ref, st_ref = (next(it) for _ in range(4))

    c = GLA_CHUNK
    n_chunks = seq_len // c
    rlow = r_ref[...].astype(BF16)
    row = lax.broadcasted_iota(jnp.int32, (c, 1), 0)
    ri = lax.broadcasted_iota(jnp.int32, (c, c), 0)
    ci = lax.broadcasted_iota(jnp.int32, (c, c), 1)
    lane8 = lax.broadcasted_iota(jnp.int32, (8, c), 1)
    r8 = lax.broadcasted_iota(jnp.int32, (8, 1), 0)

    for d in range(2):
        reverse = d == 1
        z = jnp.dot(rlow, w2_ref[d].astype(BF16), preferred_element_type=F32) + bgk_ref[d]
        g_ref[...] = _log_sigmoid(z) * (1.0 / GLA_GATE_NORM)
        if has_init:
            st_ref[...] = s0_refs[d][0, 0].T
        else:
            st_ref[...] = jnp.zeros_like(st_ref)

        order = range(n_chunks - 1, -1, -1) if reverse else range(n_chunks)
        for ch in order:
            base = ch * c
            rows = slice(base, base + c)
            q = q_ref[rows, :]
            k = k_ref[rows, :]
            v = v_ref[rows, :].astype(BF16)
            cum = g_ref[rows, :]
            sh = 1
            while sh < c:
                if reverse:
                    cum = cum + jnp.where(row < c - sh, pltpu.roll(cum, c - sh, 0), 0.0)
                else:
                    cum = cum + jnp.where(row >= sh, pltpu.roll(cum, sh, 0), 0.0)
                sh *= 2
            cum_ref[...] = cum

            def diag(i, carry, base=base, reverse=reverse):
                r0 = pl.multiple_of(i * 8, 8)
                c8 = cum_ref[pl.ds(r0, 8), :]
                q8 = q_ref[pl.ds(base + r0, 8), :]
                tile = jnp.zeros((8, c), F32)
                for s in range(8):
                    cs = cum_ref[pl.ds(r0 + s, 1), :]
                    ks = k_ref[pl.ds(base + r0 + s, 1), :]
                    ok = (r8 <= s) if reverse else (r8 >= s)
                    e = jnp.exp(jnp.where(ok, c8 - cs, 0.0))
                    col = jnp.sum(jnp.where(ok, q8 * ks * e, 0.0), axis=-1, keepdims=True)
                    tile = jnp.where(lane8 == r0 + s, col, tile)
                att_ref[pl.ds(r0, 8), :] = tile
                return carry

            lax.fori_loop(0, c // 8, diag, 0)

            att = att_ref[...]
            m = 16
            while m <= c:
                half = m // 2
                pieces = []
                for blk in range(c // m):
                    arow = blk * m + (half if reverse else half - 1)
                    pieces.append(jnp.broadcast_to(cum_ref[arow:arow + 1, :], (m, GLA_DK)))
                anchor = pieces[0] if len(pieces) == 1 else jnp.concatenate(pieces, axis=0)
                in_first = (row & (m - 1)) < half
                later = in_first if reverse else jnp.logical_not(in_first)
                fac = jnp.exp(jnp.where(later, cum - anchor, anchor - cum))
                qs = jnp.where(later, q * fac, 0.0).astype(BF16)
                ks = jnp.where(later, 0.0, k * fac).astype(BF16)
                p = _nt_dot(qs, ks)
                if m < c:
                    p = jnp.where((ri & -m) == (ci & -m), p, 0.0)
                att = att + p
                m *= 2

            last = 0 if reverse else c - 1
            total = cum_ref[last:last + 1, :]
            q_in = (q * jnp.exp(cum)).astype(BF16)
            k_out = (k * jnp.exp(total - cum)).astype(BF16)
            st = st_ref[...]
            o = (jnp.dot(att.astype(BF16), v, preferred_element_type=F32)
                 + _nt_dot(q_in, st.astype(BF16)))
            if reverse:
                o_ref[rows, :] += o
            else:
                o_ref[rows, :] = o
            kv = lax.dot_general(v, k_out, (((0,), (0,)), ((), ())), preferred_element_type=F32)
            st_ref[...] = st * jnp.exp(total) + kv

        if emit_state:
            st_out[d][0, 0] = st_ref[...].T


def _gla_scan(y, w2pad, b_gk, init_states, o_prev, seq_len, blk0, n_seq, emit_state):
    has_init = init_states is not None
    qk = lambda off: pl.BlockSpec((seq_len, GLA_DK), lambda b, h: (b + blk0, h + off))
    state = pl.BlockSpec((1, 1, GLA_DK, GLA_DV), lambda b, h: (b, h, 0, 0))
    in_specs = [
        qk(0), qk(GLA_KEY_DIM // GLA_DK),
        pl.BlockSpec((seq_len, GLA_DV), lambda b, h: (b + blk0, h + 2 * GLA_KEY_DIM // GLA_DV)),
        pl.BlockSpec((seq_len, 128), lambda b, h: (b + blk0, (2 * GLA_KEY_DIM + 2 * GLA_VAL_DIM) // 128)),
        pl.BlockSpec((2, 128, GLA_DK), lambda b, h: (0, 0, h)),
        pl.BlockSpec((2, 1, GLA_DK), lambda b, h: (0, 0, h)),
    ]
    args = [y, y, y, y, w2pad, b_gk]
    if has_init:
        in_specs += [state, state]
        args += list(init_states)
    in_specs.append(pl.BlockSpec(memory_space=pl.ANY))
    aliases = {}
    if o_prev is None:
        o_prev = jnp.zeros((8, 128), F32)
    else:
        aliases = {len(args): 0}
    args.append(o_prev)
    out_specs = [pl.BlockSpec((seq_len, GLA_DV), lambda b, h: (b + blk0, h))]
    out_shape = [jax.ShapeDtypeStruct((N_TOK, GLA_VAL_DIM), F32)]
    if emit_state:
        out_specs += [state, state]
        out_shape += [jax.ShapeDtypeStruct((n_seq, GLA_HEADS, GLA_DK, GLA_DV), F32)] * 2
    return pl.pallas_call(
        functools.partial(_gla_scan_kernel, seq_len=seq_len, has_init=has_init, emit_state=emit_state),
        grid=(n_seq, GLA_HEADS),
        in_specs=in_specs,
        out_specs=out_specs,
        out_shape=out_shape,
        scratch_shapes=[pltpu.VMEM((seq_len, GLA_DK), F32), pltpu.VMEM((GLA_CHUNK, GLA_DK), F32),
                        pltpu.VMEM((GLA_CHUNK, GLA_CHUNK), F32), pltpu.VMEM((GLA_DV, GLA_DK), F32)],
        input_output_aliases=aliases,
        compiler_params=_cparams(("arbitrary", "arbitrary")),
        name="gla_scan",
    )(*args)


_GROUP_COND_ROW = np.array([0] * N_CTX_GROUPS + list(range(1, DEC_BATCH + 1)))


def kernel(x_prompt, x_sample, c, cache_attn_k, cache_attn_v, state_gla_fwd, state_gla_bwd, c_ctx,
           mod_w, mod_b, norm1_g, norm2_g,
           attn_w_qkv, attn_w_o, attn_q_norm, attn_k_norm, attn_rpb,
           conv_w_pw1, conv_b_pw1, conv_w_dw, conv_b_dw, conv_ln_g, conv_ln_b, conv_w_pw2, conv_b_pw2,
           gla_w_q, gla_w_k, gla_w_v, gla_w_g, gla_w_gk1, gla_w_gk2, gla_b_gk, gla_o_norm, gla_w_o,
           ffn_w_up, ffn_b_up, ffn_w_dw, ffn_b_dw, ffn_w_down, ffn_b_down):
    x = jnp.concatenate([x_prompt.reshape(N_CTX_TOK, D_MODEL), x_sample.reshape(N_LAT_TOK, D_MODEL)], axis=0)
    cond = jnp.concatenate([c_ctx[None, :], c, jnp.zeros((N_COND - 1 - DEC_BATCH, D_MODEL), F32)], axis=0)
    mod_all = _modulation(cond, mod_w, mod_b)
    mod_all = mod_all[:, _GROUP_COND_ROW][:, :, None, :]

    cache_k = cache_attn_k.reshape(DEC_BATCH, -1, PAST_LEN, D_MODEL)
    cache_v = cache_attn_v.reshape(DEC_BATCH, -1, PAST_LEN, D_MODEL)
    new_k, new_v, new_sf, new_sb = [], [], [], []
    for i in range(DEPTH):
        kind, j = i % N_MIXERS, i // N_MIXERS
        mod = mod_all[i]
        n1 = norm1_g[i].reshape(1, D_MODEL)
        if kind == 0:
            w_qkv = attn_w_qkv[j].astype(BF16)
            gain = jnp.concatenate([jnp.tile(attn_q_norm[j], NA_HEADS) * (NA_HEAD_DIM ** -0.5),
                                    jnp.tile(attn_k_norm[j], NA_HEADS),
                                    jnp.ones((D_MODEL,), F32)]).reshape(1, 3 * D_MODEL)
            qp, kp, vp = _qkv_proj(x, n1, mod, w_qkv, gain, 0, N_CTX_GROUPS, F32)
            qs, ks, vs = _qkv_proj(x, n1, mod, w_qkv, gain, N_CTX_GROUPS, N_GROUPS - N_CTX_GROUPS, BF16)
            o = _ctx_attention(qp, kp, vp)
            o = _na_attention(qs, ks, vs, cache_k, cache_v, j, _na_bias_tables(attn_rpb[j]), o)
            x = _out_proj(x, o, attn_w_o[j].astype(BF16), mod)
            new_k.append(kp.reshape(BATCH, SEQ, NA_HEADS, NA_HEAD_DIM))
            new_v.append(vp.reshape(BATCH, SEQ, NA_HEADS, NA_HEAD_DIM))
        elif kind == 1:
            u = _glu_proj(x, n1, mod, conv_w_pw1[j].astype(BF16), conv_b_pw1[j])
            tail = (conv_w_dw[j], conv_b_dw[j], conv_ln_g[j], conv_ln_b[j], conv_w_pw2[j].astype(BF16),
                    conv_b_pw2[j], mod)
            x = _conv_tail(x, u, *tail, SEQ, 0, BATCH)
            x = _conv_tail(x, u, *tail, DEC_SEQ, N_CTX_GROUPS, DEC_BATCH)
        else:
            pad = GLA_PROJ_N - 2 * GLA_KEY_DIM - 2 * GLA_VAL_DIM - 2 * GLA_GATE_RANK
            w_cat = jnp.concatenate([gla_w_q[j], gla_w_k[j], gla_w_v[j], gla_w_g[j], gla_w_gk1[j, 0],
                                     gla_w_gk1[j, 1], jnp.zeros((D_MODEL, pad), F32)], axis=1).astype(BF16)
            colscale = jnp.concatenate([jnp.full((GLA_KEY_DIM,), GLA_DK ** -0.5, F32),
                                        jnp.ones((GLA_PROJ_N - GLA_KEY_DIM,), F32)]).reshape(1, GLA_PROJ_N)
            y = _scaled_proj(x, n1, mod, w_cat, colscale)
            w2pad = jnp.zeros((2, 128, GLA_KEY_DIM), F32)
            w2pad = w2pad.at[0, :GLA_GATE_RANK].set(gla_w_gk2[j, 0])
            w2pad = w2pad.at[1, GLA_GATE_RANK:2 * GLA_GATE_RANK].set(gla_w_gk2[j, 1])
            b_gk = gla_b_gk[j].reshape(2, 1, GLA_KEY_DIM)
            o, sf, sb = _gla_scan(y, w2pad, b_gk, None, None, SEQ, 0, BATCH, True)
            (o,) = _gla_scan(y, w2pad, b_gk, (state_gla_fwd[:, j], state_gla_bwd[:, j]), o,
                             DEC_SEQ, N_CTX_GROUPS, DEC_BATCH, False)
            x = _gla_out_proj(x, o, y, gla_o_norm[j].reshape(1, GLA_DV), gla_w_o[j].astype(BF16), mod)
            new_sf.append(sf)
            new_sb.append(sb)
        x = _ffn(x, norm2_g[i].reshape(1, D_MODEL), mod, ffn_w_up[i].astype(BF16), ffn_b_up[i],
                 ffn_w_dw[i], ffn_b_dw[i], ffn_w_down[i].astype(BF16), ffn_b_down[i])

    y_prompt = x[:N_CTX_TOK].reshape(BATCH, SEQ, D_MODEL)
    y_sample = x[N_CTX_TOK:].reshape(DEC_BATCH, DEC_SEQ, D_MODEL)
    return (y_prompt, y_sample, jnp.stack(new_k, axis=1), jnp.stack(new_v, axis=1),
            jnp.stack(new_sf, axis=1), jnp.stack(new_sb, axis=1))
```

```python
import functools

import numpy as np
import jax
import jax.numpy as jnp
from jax import lax
from jax.experimental import pallas as pl
from jax.experimental.pallas import tpu as pltpu

F32 = jnp.float32
BF16 = jnp.bfloat16

D_MODEL = 1024
BATCH = 16
SEQ = 256
DEPTH = 4
DEC_BATCH = 4
DEC_SEQ = 1024
PAST_LEN = 512
GRID_W = 64
N_MIXERS = 3
NA_HEADS = 16
NA_HEAD_DIM = 64
NA_WIN_R = 8
NA_WIN_C = 16
CONV_WIDTH = 31
GLA_HEADS = 4
GLA_KEY_DIM = 512
GLA_VAL_DIM = 1024
GLA_DK = 128
GLA_DV = 256
GLA_GATE_RANK = 16
GLA_GATE_NORM = 16.0
D_FF = 2816
NORM_EPS = 1e-6
NEG_INF = -1e30

TM = 1024
N_CTX_TOK = BATCH * SEQ
N_LAT_TOK = DEC_BATCH * DEC_SEQ
N_TOK = N_CTX_TOK + N_LAT_TOK
N_CTX_GROUPS = N_CTX_TOK // TM
N_GROUPS = N_TOK // TM
N_COND = 8
TF = 256
GLA_CHUNK = 256
GLA_PROJ_N = 3200
VMEM_LIMIT = 60 * 1024 * 1024


def _cparams(sem):
    return pltpu.CompilerParams(dimension_semantics=sem, vmem_limit_bytes=VMEM_LIMIT)


def _sigmoid(x):
    return 1.0 / (1.0 + jnp.exp(-x))


def _norm_mod(x, g, shift, scale):
    ms = jnp.mean(x * x, axis=-1, keepdims=True)
    y = x * lax.rsqrt(ms + NORM_EPS) * g
    return y * (1.0 + scale) + shift


def _mod_slice(mod_ref, idx):
    return mod_ref[0, :, idx * D_MODEL:(idx + 1) * D_MODEL]


def _mod_kernel(cond_ref, w_ref, b_ref, o_ref):
    c = cond_ref[...]
    s = (c * _sigmoid(c)).astype(BF16)
    o_ref[0] = jnp.dot(s, w_ref[0].astype(BF16), preferred_element_type=F32) + b_ref[0]


def _modulation(cond, mod_w, mod_b):
    tn = 1536
    n = 6 * D_MODEL
    return pl.pallas_call(
        _mod_kernel,
        grid=(DEPTH, n // tn),
        in_specs=[
            pl.BlockSpec((N_COND, D_MODEL), lambda l, j: (0, 0)),
            pl.BlockSpec((1, D_MODEL, tn), lambda l, j: (l, 0, j)),
            pl.BlockSpec((1, 1, tn), lambda l, j: (l, 0, j)),
        ],
        out_specs=pl.BlockSpec((1, N_COND, tn), lambda l, j: (l, 0, j)),
        out_shape=jax.ShapeDtypeStruct((DEPTH, N_COND, n), F32),
        compiler_params=_cparams(("arbitrary", "arbitrary")),
        name="modulation",
    )(cond, mod_w, mod_b.reshape(DEPTH, 1, n))


def _proj_prologue(x_ref, g_ref, mod_ref, h_ref):
    @pl.when(pl.program_id(1) == 0)
    def _():
        h = _norm_mod(x_ref[...], g_ref[...], _mod_slice(mod_ref, 0), _mod_slice(mod_ref, 1))
        h_ref[...] = h.astype(BF16)


def _head_rms(acc, gain, hsum_ref, hexp_ref):
    ms = jnp.dot((acc * acc).astype(BF16), hsum_ref[...], preferred_element_type=F32)
    inv = lax.rsqrt(ms + NORM_EPS)
    hi = inv.astype(BF16)
    lo = (inv - hi.astype(F32)).astype(BF16)
    inv_full = jnp.dot(jnp.concatenate([hi, lo], axis=1), hexp_ref[...], preferred_element_type=F32)
    return acc * inv_full * gain


def _qkv_kernel(x_ref, g_ref, mod_ref, w_ref, gain_ref, hsum_ref, hexp_ref,
                q_ref, k_ref, v_ref, h_ref):
    _proj_prologue(x_ref, g_ref, mod_ref, h_ref)
    j = pl.program_id(1)
    acc = jnp.dot(h_ref[...], w_ref[...], preferred_element_type=F32)

    @pl.when(j == 0)
    def _():
        q_ref[...] = _head_rms(acc, gain_ref[...], hsum_ref, hexp_ref).astype(q_ref.dtype)

    @pl.when(j == 1)
    def _():
        k_ref[...] = _head_rms(acc, gain_ref[...], hsum_ref, hexp_ref).astype(k_ref.dtype)

    @pl.when(j == 2)
    def _():
        v_ref[...] = acc.astype(v_ref.dtype)


def _head_matrices():
    lane = np.arange(D_MODEL)
    hsum = np.zeros((D_MODEL, 128), np.float32)
    hsum[lane, lane // NA_HEAD_DIM] = 1.0 / NA_HEAD_DIM
    hexp = np.zeros((128, D_MODEL), np.float32)
    hexp[lane // NA_HEAD_DIM, lane] = 1.0
    return jnp.asarray(hsum, BF16), jnp.asarray(np.concatenate([hexp, hexp], 0), BF16)


def _qkv_proj(x, norm_g, mod, w_qkv, gain, group0, n_groups, kv_dtype):
    hsum, hexp = _head_matrices()
    ntok = n_groups * TM
    row = lambda g, j: (g, 0)
    return pl.pallas_call(
        _qkv_kernel,
        grid=(n_groups, 3),
        in_specs=[
            pl.BlockSpec((TM, D_MODEL), lambda g, j: (g + group0, 0)),
            pl.BlockSpec((1, D_MODEL), lambda g, j: (0, 0)),
            pl.BlockSpec((1, 1, 6 * D_MODEL), lambda g, j: (g + group0, 0, 0)),
            pl.BlockSpec((D_MODEL, D_MODEL), lambda g, j: (0, j)),
            pl.BlockSpec((1, D_MODEL), lambda g, j: (0, j)),
            pl.BlockSpec((D_MODEL, 128), lambda g, j: (0, 0)),
            pl.BlockSpec((256, D_MODEL), lambda g, j: (0, 0)),
        ],
        out_specs=[pl.BlockSpec((TM, D_MODEL), row)] * 3,
        out_shape=[jax.ShapeDtypeStruct((ntok, D_MODEL), BF16),
                   jax.ShapeDtypeStruct((ntok, D_MODEL), kv_dtype),
                   jax.ShapeDtypeStruct((ntok, D_MODEL), kv_dtype)],
        scratch_shapes=[pltpu.VMEM((TM, D_MODEL), BF16)],
        compiler_params=_cparams(("arbitrary", "arbitrary")),
        name="qkv_proj",
    )(x, norm_g, mod, w_qkv, gain, hsum, hexp)


def _glu_kernel(x_ref, g_ref, mod_ref, wa_ref, wg_ref, ba_ref, bg_ref, u_ref, h_ref):
    _proj_prologue(x_ref, g_ref, mod_ref, h_ref)
    h = h_ref[...]
    a = jnp.dot(h, wa_ref[...], preferred_element_type=F32) + ba_ref[...]
    g = jnp.dot(h, wg_ref[...], preferred_element_type=F32) + bg_ref[...]
    u_ref[...] = a * _sigmoid(g)


def _glu_proj(x, norm_g, mod, w_pw1, b_pw1):
    tn = 512
    nj = D_MODEL // tn
    b = b_pw1.reshape(1, 2 * D_MODEL)
    return pl.pallas_call(
        _glu_kernel,
        grid=(N_GROUPS, nj),
        in_specs=[
            pl.BlockSpec((TM, D_MODEL), lambda g, j: (g, 0)),
            pl.BlockSpec((1, D_MODEL), lambda g, j: (0, 0)),
            pl.BlockSpec((1, 1, 6 * D_MODEL), lambda g, j: (g, 0, 0)),
            pl.BlockSpec((D_MODEL, tn), lambda g, j: (0, j)),
            pl.BlockSpec((D_MODEL, tn), lambda g, j: (0, j + nj)),
            pl.BlockSpec((1, tn), lambda g, j: (0, j)),
            pl.BlockSpec((1, tn), lambda g, j: (0, j + nj)),
        ],
        out_specs=pl.BlockSpec((TM, tn), lambda g, j: (g, j)),
        out_shape=jax.ShapeDtypeStruct((N_TOK, D_MODEL), F32),
        scratch_shapes=[pltpu.VMEM((TM, D_MODEL), BF16)],
        compiler_params=_cparams(("arbitrary", "arbitrary")),
        name="glu_proj",
    )(x, norm_g, mod, w_pw1, w_pw1, b, b)


def _scaled_proj_kernel(x_ref, g_ref, mod_ref, w_ref, cs_ref, y_ref, h_ref):
    _proj_prologue(x_ref, g_ref, mod_ref, h_ref)
    y_ref[...] = jnp.dot(h_ref[...], w_ref[...], preferred_element_type=F32) * cs_ref[...]


def _scaled_proj(x, norm_g, mod, w, colscale):
    n = w.shape[1]
    tn = 640
    return pl.pallas_call(
        _scaled_proj_kernel,
        grid=(N_GROUPS, n // tn),
        in_specs=[
            pl.BlockSpec((TM, D_MODEL), lambda g, j: (g, 0)),
            pl.BlockSpec((1, D_MODEL), lambda g, j: (0, 0)),
            pl.BlockSpec((1, 1, 6 * D_MODEL), lambda g, j: (g, 0, 0)),
            pl.BlockSpec((D_MODEL, tn), lambda g, j: (0, j)),
            pl.BlockSpec((1, tn), lambda g, j: (0, j)),
        ],
        out_specs=pl.BlockSpec((TM, tn), lambda g, j: (g, j)),
        out_shape=jax.ShapeDtypeStruct((N_TOK, n), F32),
        scratch_shapes=[pltpu.VMEM((TM, D_MODEL), BF16)],
        compiler_params=_cparams(("arbitrary", "arbitrary")),
        name="gla_proj",
    )(x, norm_g, mod, w, colscale)


def _oproj_kernel(x_ref, a_ref, w_ref, mod_ref, o_ref):
    r = jnp.dot(a_ref[...], w_ref[...], preferred_element_type=F32)
    o_ref[...] = x_ref[...] + _mod_slice(mod_ref, 2) * r


def _out_proj(x, a, w, mod):
    return pl.pallas_call(
        _oproj_kernel,
        grid=(N_GROUPS,),
        in_specs=[
            pl.BlockSpec((TM, D_MODEL), lambda g: (g, 0)),
            pl.BlockSpec((TM, D_MODEL), lambda g: (g, 0)),
            pl.BlockSpec((D_MODEL, D_MODEL), lambda g: (0, 0)),
            pl.BlockSpec((1, 1, 6 * D_MODEL), lambda g: (g, 0, 0)),
        ],
        out_specs=pl.BlockSpec((TM, D_MODEL), lambda g: (g, 0)),
        out_shape=jax.ShapeDtypeStruct((N_TOK, D_MODEL), F32),
        compiler_params=_cparams(("arbitrary",)),
        name="out_proj",
    )(x, a, w, mod)


def _gla_oproj_kernel(x_ref, o_ref, gz_ref, og_ref, w_ref, mod_ref, out_ref):
    og = og_ref[...]
    parts = []
    for h in range(GLA_HEADS):
        oh = o_ref[:, h * GLA_DV:(h + 1) * GLA_DV]
        ms = jnp.mean(oh * oh, axis=-1, keepdims=True)
        parts.append(oh * lax.rsqrt(ms + NORM_EPS) * og)
    y = jnp.concatenate(parts, axis=1)
    gz = gz_ref[...]
    a = (y * (gz * _sigmoid(gz))).astype(BF16)
    r = jnp.dot(a, w_ref[...], preferred_element_type=F32)
    out_ref[...] = x_ref[...] + _mod_slice(mod_ref, 2) * r


def _gla_out_proj(x, o, y, o_norm, w, mod):
    return pl.pallas_call(
        _gla_oproj_kernel,
        grid=(N_GROUPS,),
        in_specs=[
            pl.BlockSpec((TM, D_MODEL), lambda g: (g, 0)),
            pl.BlockSpec((TM, GLA_VAL_DIM), lambda g: (g, 0)),
            pl.BlockSpec((TM, GLA_VAL_DIM), lambda g: (g, 2)),
            pl.BlockSpec((1, GLA_DV), lambda g: (0, 0)),
            pl.BlockSpec((GLA_VAL_DIM, D_MODEL), lambda g: (0, 0)),
            pl.BlockSpec((1, 1, 6 * D_MODEL), lambda g: (g, 0, 0)),
        ],
        out_specs=pl.BlockSpec((TM, D_MODEL), lambda g: (g, 0)),
        out_shape=jax.ShapeDtypeStruct((N_TOK, D_MODEL), F32),
        compiler_params=_cparams(("arbitrary",)),
        name="gla_out_proj",
    )(x, o, y, o_norm, w, mod)


def _ffn_kernel(x_ref, g_ref, mod_ref, wup_ref, bup_ref, wdw_ref, bdw_ref, wdn_ref, bdn_ref,
                out_ref, h_ref, acc_ref):
    x = x_ref[...]
    h_ref[...] = _norm_mod(x, g_ref[...], _mod_slice(mod_ref, 3), _mod_slice(mod_ref, 4)).astype(BF16)
    seq_len = jnp.where(pl.program_id(0) < N_CTX_GROUPS, SEQ, DEC_SEQ)
    pos = lax.broadcasted_iota(jnp.int32, (TM, 1), 0) & (seq_len - 1)
    first = pos == 0
    last = pos == seq_len - 1
    acc_ref[...] = jnp.zeros_like(acc_ref)

    def conv(u, col):
        w = wdw_ref[:, pl.ds(col, TF)]
        prev = jnp.where(first, 0.0, pltpu.roll(u, 1, 0))
        nxt = jnp.where(last, 0.0, pltpu.roll(u, TM - 1, 0))
        return w[0:1] * prev + w[1:2] * u + w[2:3] * nxt + bdw_ref[:, pl.ds(col, TF)]

    def body(c, carry):
        ca = pl.multiple_of(c * TF, TF)
        cg = pl.multiple_of(D_FF + c * TF, TF)
        h = h_ref[...]
        ua = jnp.dot(h, wup_ref[:, pl.ds(ca, TF)], preferred_element_type=F32) + bup_ref[:, pl.ds(ca, TF)]
        ug = jnp.dot(h, wup_ref[:, pl.ds(cg, TF)], preferred_element_type=F32) + bup_ref[:, pl.ds(cg, TF)]
        a = conv(ua, ca)
        g = conv(ug, cg)
        act = (g * _sigmoid(g) * a).astype(BF16)
        acc_ref[...] += jnp.dot(act, wdn_ref[pl.ds(ca, TF), :], preferred_element_type=F32)
        return carry

    lax.fori_loop(0, D_FF // TF, body, 0)
    out_ref[...] = x + _mod_slice(mod_ref, 5) * (acc_ref[...] + bdn_ref[...])


def _ffn(x, norm_g, mod, w_up, b_up, w_dw, b_dw, w_down, b_down):
    const = lambda g: (0, 0)
    resident = dict(pipeline_mode=pl.Buffered(1))
    return pl.pallas_call(
        _ffn_kernel,
        grid=(N_GROUPS,),
        in_specs=[
            pl.BlockSpec((TM, D_MODEL), lambda g: (g, 0)),
            pl.BlockSpec((1, D_MODEL), const),
            pl.BlockSpec((1, 1, 6 * D_MODEL), lambda g: (g, 0, 0)),
            pl.BlockSpec((D_MODEL, 2 * D_FF), const, **resident),
            pl.BlockSpec((1, 2 * D_FF), const),
            pl.BlockSpec((3, 2 * D_FF), const),
            pl.BlockSpec((1, 2 * D_FF), const),
            pl.BlockSpec((D_FF, D_MODEL), const, **resident),
            pl.BlockSpec((1, D_MODEL), const),
        ],
        out_specs=pl.BlockSpec((TM, D_MODEL), lambda g: (g, 0)),
        out_shape=jax.ShapeDtypeStruct((N_TOK, D_MODEL), F32),
        scratch_shapes=[pltpu.VMEM((TM, D_MODEL), BF16), pltpu.VMEM((TM, D_MODEL), F32)],
        compiler_params=_cparams(("arbitrary",)),
        name="conv_ffn",
    )(x, norm_g, mod, w_up, b_up.reshape(1, -1), w_dw, b_dw.reshape(1, -1), w_down, b_down.reshape(1, -1))


def _pair_queries(q):
    lane = lax.broadcasted_iota(jnp.int32, q.shape, 1)
    zero = jnp.zeros_like(q)
    return jnp.concatenate([jnp.where(lane < NA_HEAD_DIM, q, zero),
                            jnp.where(lane < NA_HEAD_DIM, zero, q)], axis=0)


def _pair_merge(o2):
    n = o2.shape[0] // 2
    lane = lax.broadcasted_iota(jnp.int32, (n, o2.shape[1]), 1)
    return jnp.where(lane < NA_HEAD_DIM, o2[:n], o2[n:])


def _nt_dot(a, b):
    return lax.dot_general(a, b, (((1,), (1,)), ((), ())), preferred_element_type=F32)


def _ctx_attn_kernel(q_ref, k_ref, v_ref, o_ref):
    for b in range(TM // SEQ):
        rows = slice(b * SEQ, (b + 1) * SEQ)
        q2 = _pair_queries(q_ref[rows, :])
        s = _nt_dot(q2, k_ref[rows, :].astype(BF16))
        p = jnp.exp(s - jnp.max(s, axis=-1, keepdims=True))
        l = jnp.sum(p, axis=-1, keepdims=True)
        o2 = jnp.dot(p.astype(BF16), v_ref[rows, :].astype(BF16), preferred_element_type=F32) / l
        o_ref[rows, :] = _pair_merge(o2).astype(o_ref.dtype)


def _ctx_attention(q, k, v):
    spec = pl.BlockSpec((TM, 128), lambda g, hp: (g, hp))
    return pl.pallas_call(
        _ctx_attn_kernel,
        grid=(N_CTX_GROUPS, NA_HEADS // 2),
        in_specs=[spec, spec, spec],
        out_specs=spec,
        out_shape=jax.ShapeDtypeStruct((N_TOK, D_MODEL), BF16),
        compiler_params=_cparams(("arbitrary", "arbitrary")),
        name="ctx_attention",
    )(q, k, v)


_NA_BLOCKS = ((0, 8, 0), (0, 12, 1), (4, 12, 1), (8, 8, 2))


def _na_attn_kernel(q_ref, k_ref, v_ref, ck_ref, cv_ref, ba_ref, bm_ref, bc_ref, o_in_ref, o_ref):
    del o_in_ref
    bias_refs = (ba_ref, bm_ref, bc_ref)
    ck = ck_ref[0, 0].astype(BF16)
    cv = cv_ref[0, 0].astype(BF16)
    for blk, (row0, nrows, bidx) in enumerate(_NA_BLOCKS):
        rows = slice(blk * 4 * GRID_W, (blk + 1) * 4 * GRID_W)
        keys = slice(row0 * GRID_W, (row0 + nrows) * GRID_W)
        q2 = _pair_queries(q_ref[rows, :])
        bias = bias_refs[bidx][...]
        s_loc = _nt_dot(q2, k_ref[keys, :]) + bias.reshape(2 * 4 * GRID_W, nrows * GRID_W)
        s_ctx = _nt_dot(q2, ck)
        m = jnp.maximum(jnp.max(s_loc, axis=-1, keepdims=True), jnp.max(s_ctx, axis=-1, keepdims=True))
        p_loc = jnp.exp(s_loc - m)
        p_ctx = jnp.exp(s_ctx - m)
        l = jnp.sum(p_loc, axis=-1, keepdims=True) + jnp.sum(p_ctx, axis=-1, keepdims=True)
        o2 = (jnp.dot(p_loc.astype(BF16), v_ref[keys, :], preferred_element_type=F32)
              + jnp.dot(p_ctx.astype(BF16), cv, preferred_element_type=F32)) / l
        o_ref[rows, :] = _pair_merge(o2).astype(o_ref.dtype)


N_RPB_R = 2 * NA_WIN_R - 1
N_RPB_C = 2 * NA_WIN_C - 1


def _na_bias_kernel(rpb_ref, ba_ref, bm_ref, bc_ref):
    base = pl.program_id(0) * (N_RPB_R * N_RPB_C)
    qc = lax.broadcasted_iota(jnp.int32, (GRID_W, 2 * GRID_W), 0)
    lane = lax.broadcasted_iota(jnp.int32, (GRID_W, 2 * GRID_W), 1)
    kc = lane & (GRID_W - 1)
    right = lane >= GRID_W
    dcol = kc - qc + (NA_WIN_C - 1)
    c_start = jnp.clip(qc - NA_WIN_C // 2, 0, GRID_W - NA_WIN_C)
    in_win = (kc >= c_start) & (kc < c_start + NA_WIN_C)
    neg = jnp.full((GRID_W, 2 * GRID_W), NEG_INF, F32)
    tiles = {}

    def rpb_at(dr, j):
        return rpb_ref[base + dr * N_RPB_C + j] if 0 <= dr < N_RPB_R else jnp.float32(0.0)

    def pair_tile(dr):
        if dr not in tiles:
            acc = jnp.zeros((GRID_W, 2 * GRID_W), F32)
            for j in range(N_RPB_C):
                acc = jnp.where(dcol == j, jnp.where(right, rpb_at(dr + 1, j), rpb_at(dr, j)), acc)
            tiles[dr] = acc
        return tiles[dr]

    rows_total = DEC_SEQ // GRID_W
    for ref, blk in ((ba_ref, 0), (bm_ref, 1), (bc_ref, 3)):
        row0, nrows, _ = _NA_BLOCKS[blk]
        for rr in range(4):
            r = blk * 4 + rr
            r_start = min(max(r - NA_WIN_R // 2, 0), rows_total - NA_WIN_R)
            for ip in range(nrows // 2):
                krow = row0 + 2 * ip
                ok_l = r_start <= krow < r_start + NA_WIN_R
                ok_r = r_start <= krow + 1 < r_start + NA_WIN_R
                if ok_l or ok_r:
                    mask = in_win
                    if not ok_l:
                        mask = mask & right
                    if not ok_r:
                        mask = mask & jnp.logical_not(right)
                    tile = jnp.where(mask, pair_tile(krow - r + NA_WIN_R - 1), neg)
                else:
                    tile = neg
                ref[0, rr * GRID_W:(rr + 1) * GRID_W, ip * 2 * GRID_W:(ip + 1) * 2 * GRID_W] = tile


def _na_bias_tables(rpb):
    out = lambda n: pl.BlockSpec((1, 4 * GRID_W, n), lambda h: (h, 0, 0))
    shape = lambda n: jax.ShapeDtypeStruct((NA_HEADS, 4 * GRID_W, n), F32)
    return pl.pallas_call(
        _na_bias_kernel,
        grid=(NA_HEADS,),
        in_specs=[pl.BlockSpec(memory_space=pltpu.SMEM)],
        out_specs=[out(8 * GRID_W), out(12 * GRID_W), out(8 * GRID_W)],
        out_shape=[shape(8 * GRID_W), shape(12 * GRID_W), shape(8 * GRID_W)],
        compiler_params=_cparams(("arbitrary",)),
        name="na_bias",
    )(rpb.astype(F32).reshape(-1))


def _na_attention(q, k, v, cache_k, cache_v, layer_j, bias_tables, o_ctx):
    ba, bm, bc = bias_tables
    tok = pl.BlockSpec((TM, 128), lambda hp, b: (b, hp))
    cache = pl.BlockSpec((1, 1, PAST_LEN, 128), lambda hp, b: (b, layer_j, 0, hp))
    bias = lambda n: pl.BlockSpec((2, 4 * GRID_W, n), lambda hp, b: (hp, 0, 0))
    return pl.pallas_call(
        _na_attn_kernel,
        grid=(NA_HEADS // 2, DEC_BATCH),
        in_specs=[tok, tok, tok, cache, cache, bias(8 * GRID_W), bias(12 * GRID_W), bias(8 * GRID_W),
                  pl.BlockSpec(memory_space=pl.ANY)],
        out_specs=pl.BlockSpec((TM, 128), lambda hp, b: (b + N_CTX_GROUPS, hp)),
        out_shape=jax.ShapeDtypeStruct((N_TOK, D_MODEL), BF16),
        input_output_aliases={8: 0},
        compiler_params=_cparams(("arbitrary", "arbitrary")),
        name="na_attention",
    )(q, k, v, cache_k, cache_v, ba, bm, bc, o_ctx)


CONV_PAD = 16
CONV_ROWS = 32
CONV_LANES = 256


def _conv_tail_kernel(x_ref, u_ref, wdw_ref, bdw_ref, lng_ref, lnb_ref, w_ref, b_ref, mod_ref,
                      o_ref, pad_ref, a_ref, *, seq_len):
    zeros = jnp.zeros((CONV_PAD, D_MODEL), F32)
    pad_ref[0:CONV_PAD, :] = zeros
    pad_ref[CONV_PAD + seq_len:, :] = zeros
    pad_ref[CONV_PAD:CONV_PAD + seq_len, :] = u_ref[...]
    half = CONV_WIDTH // 2

    win = CONV_ROWS + 8

    def rows(i, carry):
        r0 = pl.multiple_of(i * CONV_ROWS, CONV_ROWS)
        strips = []
        for l0 in range(0, D_MODEL, CONV_LANES):
            lanes = slice(l0, l0 + CONV_LANES)
            acc = jnp.zeros((CONV_ROWS, CONV_LANES), F32) + bdw_ref[:, lanes]
            for b in range(8):
                z = None
                for a in range(-2, 2):
                    t = 8 * a + b + half
                    if not 0 <= t < CONV_WIDTH:
                        continue
                    term = wdw_ref[t:t + 1, lanes] * pad_ref[pl.ds(r0 + (CONV_PAD + 8 * a), win), lanes]
                    z = term if z is None else z + term
                if b:
                    z = pltpu.roll(z, win - b, 0)
                acc = acc + z[:CONV_ROWS]
            strips.append(acc)
        acc = jnp.concatenate(strips, axis=1)
        mu = jnp.mean(acc, axis=-1, keepdims=True)
        cen = acc - mu
        var = jnp.mean(cen * cen, axis=-1, keepdims=True)
        y = cen * lax.rsqrt(var + NORM_EPS) * lng_ref[...] + lnb_ref[...]
        a_ref[pl.ds(r0, CONV_ROWS), :] = (y * _sigmoid(y)).astype(BF16)
        return carry

    lax.fori_loop(0, seq_len // CONV_ROWS, rows, 0)
    r = jnp.dot(a_ref[...], w_ref[...], preferred_element_type=F32) + b_ref[...]
    o_ref[...] = x_ref[...] + _mod_slice(mod_ref, 2) * r


def _conv_tail(x, u, w_dw, b_dw, ln_g, ln_b, w_pw2, b_pw2, mod, seq_len, seq0, n_seq):
    per_group = TM // seq_len
    tok = pl.BlockSpec((seq_len, D_MODEL), lambda s: (s + seq0, 0))
    const = lambda s: (0, 0)
    vec = pl.BlockSpec((1, D_MODEL), const)
    return pl.pallas_call(
        functools.partial(_conv_tail_kernel, seq_len=seq_len),
        grid=(n_seq,),
        in_specs=[tok, tok, pl.BlockSpec((CONV_WIDTH, D_MODEL), const), vec, vec, vec,
                  pl.BlockSpec((D_MODEL, D_MODEL), const), vec,
                  pl.BlockSpec((1, 1, 6 * D_MODEL), lambda s: ((s + seq0) // per_group, 0, 0))],
        out_specs=tok,
        out_shape=jax.ShapeDtypeStruct((N_TOK, D_MODEL), F32),
        scratch_shapes=[pltpu.VMEM((seq_len + 2 * CONV_PAD, D_MODEL), F32),
                        pltpu.VMEM((seq_len, D_MODEL), BF16)],
        input_output_aliases={0: 0},
        compiler_params=_cparams(("arbitrary",)),
        name="conv_tail",
    )(x, u, w_dw, b_dw.reshape(1, -1), ln_g.reshape(1, -1), ln_b.reshape(1, -1), w_pw2,
      b_pw2.reshape(1, -1), mod)


def _log_sigmoid(z):
    return -(jnp.maximum(-z, 0.0) + jnp.log(1.0 + jnp.exp(-jnp.abs(z))))


def _gla_scan_kernel(*refs, seq_len, has_init, emit_state):
    it = iter(refs)
    q_ref, k_ref, v_ref, r_ref, w2_ref, bgk_ref = (next(it) for _ in range(6))
    s0_refs = (next(it), next(it)) if has_init else None
    o_in_ref = next(it)
    del o_in_ref
    o_ref = next(it)
    st_out = (next(it), next(it)) if emit_state else None
    g_ref, cum_ref, att_ref, st_ref = (next(it) for _ in range(4))

    c = GLA_CHUNK
    n_chunks = seq_len // c
    rlow = r_ref[...].astype(BF16)
    row = lax.broadcasted_iota(jnp.int32, (c, 1), 0)
    ri = lax.broadcasted_iota(jnp.int32, (c, c), 0)
    ci = lax.broadcasted_iota(jnp.int32, (c, c), 1)
    lane8 = lax.broadcasted_iota(jnp.int32, (8, c), 1)
    r8 = lax.broadcasted_iota(jnp.int32, (8, 1), 0)

    for d in range(2):
        reverse = d == 1
        z = jnp.dot(rlow, w2_ref[d].astype(BF16), preferred_element_type=F32) + bgk_ref[d]
        g_ref[...] = _log_sigmoid(z) * (1.0 / GLA_GATE_NORM)
        if has_init:
            st_ref[...] = s0_refs[d][0, 0].T
        else:
            st_ref[...] = jnp.zeros_like(st_ref)

        order = range(n_chunks - 1, -1, -1) if reverse else range(n_chunks)
        for ch in order:
            base = ch * c
            rows = slice(base, base + c)
            q = q_ref[rows, :]
            k = k_ref[rows, :]
            v = v_ref[rows, :].astype(BF16)
            cum = g_ref[rows, :]
            sh = 1
            while sh < c:
                if reverse:
                    cum = cum + jnp.where(row < c - sh, pltpu.roll(cum, c - sh, 0), 0.0)
                else:
                    cum = cum + jnp.where(row >= sh, pltpu.roll(cum, sh, 0), 0.0)
                sh *= 2
            cum_ref[...] = cum

            def diag(i, carry, base=base, reverse=reverse):
                r0 = pl.multiple_of(i * 8, 8)
                c8 = cum_ref[pl.ds(r0, 8), :]
                q8 = q_ref[pl.ds(base + r0, 8), :]
                tile = jnp.zeros((8, c), F32)
                for s in range(8):
                    cs = cum_ref[pl.ds(r0 + s, 1), :]
                    ks = k_ref[pl.ds(base + r0 + s, 1), :]
                    ok = (r8 <= s) if reverse else (r8 >= s)
                    e = jnp.exp(jnp.where(ok, c8 - cs, 0.0))
                    col = jnp.sum(jnp.where(ok, q8 * ks * e, 0.0), axis=-1, keepdims=True)
                    tile = jnp.where(lane8 == r0 + s, col, tile)
                att_ref[pl.ds(r0, 8), :] = tile
                return carry

            lax.fori_loop(0, c // 8, diag, 0)

            att = att_ref[...]
            m = 16
            while m <= c:
                half = m // 2
                pieces = []
                for blk in range(c // m):
                    arow = blk * m + (half if reverse else half - 1)
                    pieces.append(jnp.broadcast_to(cum_ref[arow:arow + 1, :], (m, GLA_DK)))
                anchor = pieces[0] if len(pieces) == 1 else jnp.concatenate(pieces, axis=0)
                in_first = (row & (m - 1)) < half
                later = in_first if reverse else jnp.logical_not(in_first)
                fac = jnp.exp(jnp.where(later, cum - anchor, anchor - cum))
                qs = jnp.where(later, q * fac, 0.0).astype(BF16)
                ks = jnp.where(later, 0.0, k * fac).astype(BF16)
                p = _nt_dot(qs, ks)
                if m < c:
                    p = jnp.where((ri & -m) == (ci & -m), p, 0.0)
                att = att + p
                m *= 2

            last = 0 if reverse else c - 1
            total = cum_ref[last:last + 1, :]
            q_in = (q * jnp.exp(cum)).astype(BF16)
            k_out = (k * jnp.exp(total - cum)).astype(BF16)
            st = st_ref[...]
            o = (jnp.dot(att.astype(BF16), v, preferred_element_type=F32)
                 + _nt_dot(q_in, st.astype(BF16)))
            if reverse:
                o_ref[rows, :] += o
            else:
                o_ref[rows, :] = o
            kv = lax.dot_general(v, k_out, (((0,), (0,)), ((), ())), preferred_element_type=F32)
            st_ref[...] = st * jnp.exp(total) + kv

        if emit_state:
            st_out[d][0, 0] = st_ref[...].T


def _gla_scan(y, w2pad, b_gk, init_states, o_prev, seq_len, blk0, n_seq, emit_state):
    has_init = init_states is not None
    qk = lambda off: pl.BlockSpec((seq_len, GLA_DK), lambda b, h: (b + blk0, h + off))
    state = pl.BlockSpec((1, 1, GLA_DK, GLA_DV), lambda b, h: (b, h, 0, 0))
    in_specs = [
        qk(0), qk(GLA_KEY_DIM // GLA_DK),
        pl.BlockSpec((seq_len, GLA_DV), lambda b, h: (b + blk0, h + 2 * GLA_KEY_DIM // GLA_DV)),
        pl.BlockSpec((seq_len, 128), lambda b, h: (b + blk0, (2 * GLA_KEY_DIM + 2 * GLA_VAL_DIM) // 128)),
        pl.BlockSpec((2, 128, GLA_DK), lambda b, h: (0, 0, h)),
        pl.BlockSpec((2, 1, GLA_DK), lambda b, h: (0, 0, h)),
    ]
    args = [y, y, y, y, w2pad, b_gk]
    if has_init:
        in_specs += [state, state]
        args += list(init_states)
    in_specs.append(pl.BlockSpec(memory_space=pl.ANY))
    aliases = {}
    if o_prev is None:
        o_prev = jnp.zeros((8, 128), F32)
    else:
        aliases = {len(args): 0}
    args.append(o_prev)
    out_specs = [pl.BlockSpec((seq_len, GLA_DV), lambda b, h: (b + blk0, h))]
    out_shape = [jax.ShapeDtypeStruct((N_TOK, GLA_VAL_DIM), F32)]
    if emit_state:
        out_specs += [state, state]
        out_shape += [jax.ShapeDtypeStruct((n_seq, GLA_HEADS, GLA_DK, GLA_DV), F32)] * 2
    return pl.pallas_call(
        functools.partial(_gla_scan_kernel, seq_len=seq_len, has_init=has_init, emit_state=emit_state),
        grid=(n_seq, GLA_HEADS),
        in_specs=in_specs,
        out_specs=out_specs,
        out_shape=out_shape,
        scratch_shapes=[pltpu.VMEM((seq_len, GLA_DK), F32), pltpu.VMEM((GLA_CHUNK, GLA_DK), F32),
                        pltpu.VMEM((GLA_CHUNK, GLA_CHUNK), F32), pltpu.VMEM((GLA_DV, GLA_DK), F32)],
        input_output_aliases=aliases,
        compiler_params=_cparams(("arbitrary", "arbitrary")),
        name="gla_scan",
    )(*args)


_GROUP_COND_ROW = np.array([0] * N_CTX_GROUPS + list(range(1, DEC_BATCH + 1)))


def kernel(x_prompt, x_sample, c, cache_attn_k, cache_attn_v, state_gla_fwd, state_gla_bwd, c_ctx,
           mod_w, mod_b, norm1_g, norm2_g,
           attn_w_qkv, attn_w_o, attn_q_norm, attn_k_norm, attn_rpb,
           conv_w_pw1, conv_b_pw1, conv_w_dw, conv_b_dw, conv_ln_g, conv_ln_b, conv_w_pw2, conv_b_pw2,
           gla_w_q, gla_w_k, gla_w_v, gla_w_g, gla_w_gk1, gla_w_gk2, gla_b_gk, gla_o_norm, gla_w_o,
           ffn_w_up, ffn_b_up, ffn_w_dw, ffn_b_dw, ffn_w_down, ffn_b_down):
    x = jnp.concatenate([x_prompt.reshape(N_CTX_TOK, D_MODEL), x_sample.reshape(N_LAT_TOK, D_MODEL)], axis=0)
    cond = jnp.concatenate([c_ctx[None, :], c, jnp.zeros((N_COND - 1 - DEC_BATCH, D_MODEL), F32)], axis=0)
    mod_all = _modulation(cond, mod_w, mod_b)
    mod_all = mod_all[:, _GROUP_COND_ROW][:, :, None, :]

    cache_k = cache_attn_k.reshape(DEC_BATCH, -1, PAST_LEN, D_MODEL)
    cache_v = cache_attn_v.reshape(DEC_BATCH, -1, PAST_LEN, D_MODEL)
    new_k, new_v, new_sf, new_sb = [], [], [], []
    for i in range(DEPTH):
        kind, j = i % N_MIXERS, i // N_MIXERS
        mod = mod_all[i]
        n1 = norm1_g[i].reshape(1, D_MODEL)
        if kind == 0:
            w_qkv = attn_w_qkv[j].astype(BF16)
            gain = jnp.concatenate([jnp.tile(attn_q_norm[j], NA_HEADS) * (NA_HEAD_DIM ** -0.5),
                                    jnp.tile(attn_k_norm[j], NA_HEADS),
                                    jnp.ones((D_MODEL,), F32)]).reshape(1, 3 * D_MODEL)
            qp, kp, vp = _qkv_proj(x, n1, mod, w_qkv, gain, 0, N_CTX_GROUPS, F32)
            qs, ks, vs = _qkv_proj(x, n1, mod, w_qkv, gain, N_CTX_GROUPS, N_GROUPS - N_CTX_GROUPS, BF16)
            o = _ctx_attention(qp, kp, vp)
            o = _na_attention(qs, ks, vs, cache_k, cache_v, j, _na_bias_tables(attn_rpb[j]), o)
            x = _out_proj(x, o, attn_w_o[j].astype(BF16), mod)
            new_k.append(kp.reshape(BATCH, SEQ, NA_HEADS, NA_HEAD_DIM))
            new_v.append(vp.reshape(BATCH, SEQ, NA_HEADS, NA_HEAD_DIM))
        elif kind == 1:
            u = _glu_proj(x, n1, mod, conv_w_pw1[j].astype(BF16), conv_b_pw1[j])
            tail = (conv_w_dw[j], conv_b_dw[j], conv_ln_g[j], conv_ln_b[j], conv_w_pw2[j].astype(BF16),
                    conv_b_pw2[j], mod)
            x = _conv_tail(x, u, *tail, SEQ, 0, BATCH)
            x = _conv_tail(x, u, *tail, DEC_SEQ, N_CTX_GROUPS, DEC_BATCH)
        else:
            pad = GLA_PROJ_N - 2 * GLA_KEY_DIM - 2 * GLA_VAL_DIM - 2 * GLA_GATE_RANK
            w_cat = jnp.concatenate([gla_w_q[j], gla_w_k[j], gla_w_v[j], gla_w_g[j], gla_w_gk1[j, 0],
                                     gla_w_gk1[j, 1], jnp.zeros((D_MODEL, pad), F32)], axis=1).astype(BF16)
            colscale = jnp.concatenate([jnp.full((GLA_KEY_DIM,), GLA_DK ** -0.5, F32),
                                        jnp.ones((GLA_PROJ_N - GLA_KEY_DIM,), F32)]).reshape(1, GLA_PROJ_N)
            y = _scaled_proj(x, n1, mod, w_cat, colscale)
            w2pad = jnp.zeros((2, 128, GLA_KEY_DIM), F32)
            w2pad = w2pad.at[0, :GLA_GATE_RANK].set(gla_w_gk2[j, 0])
            w2pad = w2pad.at[1, GLA_GATE_RANK:2 * GLA_GATE_RANK].set(gla_w_gk2[j, 1])
            b_gk = gla_b_gk[j].reshape(2, 1, GLA_KEY_DIM)
            o, sf, sb = _gla_scan(y, w2pad, b_gk, None, None, SEQ, 0, BATCH, True)
            (o,) = _gla_scan(y, w2pad, b_gk, (state_gla_fwd[:, j], state_gla_bwd[:, j]), o,
                             DEC_SEQ, N_CTX_GROUPS, DEC_BATCH, False)
            x = _gla_out_proj(x, o, y, gla_o_norm[j].reshape(1, GLA_DV), gla_w_o[j].astype(BF16), mod)
            new_sf.append(sf)
            new_sb.append(sb)
        x = _ffn(x, norm2_g[i].reshape(1, D_MODEL), mod, ffn_w_up[i].astype(BF16), ffn_b_up[i],
                 ffn_w_dw[i], ffn_b_dw[i], ffn_w_down[i].astype(BF16), ffn_b_down[i])

    y_prompt = x[:N_CTX_TOK].reshape(BATCH, SEQ, D_MODEL)
    y_sample = x[N_CTX_TOK:].reshape(DEC_BATCH, DEC_SEQ, D_MODEL)
    return (y_prompt, y_sample, jnp.stack(new_k, axis=1), jnp.stack(new_v, axis=1),
            jnp.stack(new_sf, axis=1), jnp.stack(new_sb, axis=1))
```

```python
import functools

import numpy as np
import jax
import jax.numpy as jnp
from jax import lax
from jax.experimental import pallas as pl
from jax.experimental.pallas import tpu as pltpu

F32 = jnp.float32
BF16 = jnp.bfloat16

D_MODEL = 1024
BATCH = 16
SEQ = 256
DEPTH = 4
DEC_BATCH = 4
DEC_SEQ = 1024
PAST_LEN = 512
GRID_W = 64
N_MIXERS = 3
NA_HEADS = 16
NA_HEAD_DIM = 64
NA_WIN_R = 8
NA_WIN_C = 16
CONV_WIDTH = 31
GLA_HEADS = 4
GLA_KEY_DIM = 512
GLA_VAL_DIM = 1024
GLA_DK = 128
GLA_DV = 256
GLA_GATE_RANK = 16
GLA_GATE_NORM = 16.0
D_FF = 2816
NORM_EPS = 1e-6
NEG_INF = -1e30

TM = 1024
N_CTX_TOK = BATCH * SEQ
N_LAT_TOK = DEC_BATCH * DEC_SEQ
N_TOK = N_CTX_TOK + N_LAT_TOK
N_CTX_GROUPS = N_CTX_TOK // TM
N_GROUPS = N_TOK // TM
N_COND = 8
TF = 256
GLA_CHUNK = 256
GLA_PROJ_N = 3328
VMEM_LIMIT = 60 * 1024 * 1024


def _cparams(sem, flags=None):
    return pltpu.CompilerParams(dimension_semantics=sem, vmem_limit_bytes=VMEM_LIMIT, flags=flags)


def _sigmoid(x):
    return 1.0 / (1.0 + jnp.exp(-x))


def _norm_mod(x, g, shift, scale):
    ms = jnp.mean(x * x, axis=-1, keepdims=True)
    y = x * lax.rsqrt(ms + NORM_EPS) * g
    return y * (1.0 + scale) + shift


def _mod_slice(mod_ref, idx):
    return mod_ref[0, :, idx * D_MODEL:(idx + 1) * D_MODEL]


def _mod_kernel(cond_ref, w_ref, b_ref, o_ref):
    c = cond_ref[...]
    s = (c * _sigmoid(c)).astype(BF16)
    o_ref[0] = jnp.dot(s, w_ref[0].astype(BF16), preferred_element_type=F32) + b_ref[0]


def _modulation(cond, mod_w, mod_b):
    tn = 1536
    n = 6 * D_MODEL
    return pl.pallas_call(
        _mod_kernel,
        grid=(DEPTH, n // tn),
        in_specs=[
            pl.BlockSpec((N_COND, D_MODEL), lambda l, j: (0, 0)),
            pl.BlockSpec((1, D_MODEL, tn), lambda l, j: (l, 0, j)),
            pl.BlockSpec((1, 1, tn), lambda l, j: (l, 0, j)),
        ],
        out_specs=pl.BlockSpec((1, N_COND, tn), lambda l, j: (l, 0, j)),
        out_shape=jax.ShapeDtypeStruct((DEPTH, N_COND, n), F32),
        compiler_params=_cparams(("arbitrary", "arbitrary")),
        name="modulation",
    )(cond, mod_w, mod_b.reshape(DEPTH, 1, n))


def _proj_prologue(x_ref, g_ref, mod_ref, h_ref):
    @pl.when(pl.program_id(1) == 0)
    def _():
        h = _norm_mod(x_ref[...], g_ref[...], _mod_slice(mod_ref, 0), _mod_slice(mod_ref, 1))
        h_ref[...] = h.astype(BF16)


def _head_rms(acc, gain, hsum_ref, hexp_ref):
    ms = jnp.dot((acc * acc).astype(BF16), hsum_ref[...], preferred_element_type=F32)
    inv = lax.rsqrt(ms + NORM_EPS)
    hi = inv.astype(BF16)
    lo = (inv - hi.astype(F32)).astype(BF16)
    inv_full = jnp.dot(jnp.concatenate([hi, lo], axis=1), hexp_ref[...], preferred_element_type=F32)
    return acc * inv_full * gain


def _qkv_kernel(x_ref, g_ref, mod_ref, w_ref, gain_ref, hsum_ref, hexp_ref,
                q_ref, k_ref, v_ref, h_ref):
    _proj_prologue(x_ref, g_ref, mod_ref, h_ref)
    j = pl.program_id(1)
    acc = jnp.dot(h_ref[...], w_ref[...], preferred_element_type=F32)

    @pl.when(j == 0)
    def _():
        q_ref[...] = _head_rms(acc, gain_ref[...], hsum_ref, hexp_ref).astype(q_ref.dtype)

    @pl.when(j == 1)
    def _():
        k_ref[...] = _head_rms(acc, gain_ref[...], hsum_ref, hexp_ref).astype(k_ref.dtype)

    @pl.when(j == 2)
    def _():
        v_ref[...] = acc.astype(v_ref.dtype)


def _head_matrices():
    lane = np.arange(D_MODEL)
    hsum = np.zeros((D_MODEL, 128), np.float32)
    hsum[lane, lane // NA_HEAD_DIM] = 1.0 / NA_HEAD_DIM
    hexp = np.zeros((128, D_MODEL), np.float32)
    hexp[lane // NA_HEAD_DIM, lane] = 1.0
    return jnp.asarray(hsum, BF16), jnp.asarray(np.concatenate([hexp, hexp], 0), BF16)


def _qkv_proj(x, norm_g, mod, w_qkv, gain, group0, n_groups, kv_dtype):
    hsum, hexp = _head_matrices()
    ntok = n_groups * TM
    row = lambda g, j: (g, 0)
    return pl.pallas_call(
        _qkv_kernel,
        grid=(n_groups, 3),
        in_specs=[
            pl.BlockSpec((TM, D_MODEL), lambda g, j: (g + group0, 0)),
            pl.BlockSpec((1, D_MODEL), lambda g, j: (0, 0)),
            pl.BlockSpec((1, 1, 6 * D_MODEL), lambda g, j: (g + group0, 0, 0)),
            pl.BlockSpec((D_MODEL, D_MODEL), lambda g, j: (0, j)),
            pl.BlockSpec((1, D_MODEL), lambda g, j: (0, j)),
            pl.BlockSpec((D_MODEL, 128), lambda g, j: (0, 0)),
            pl.BlockSpec((256, D_MODEL), lambda g, j: (0, 0)),
        ],
        out_specs=[pl.BlockSpec((TM, D_MODEL), row)] * 3,
        out_shape=[jax.ShapeDtypeStruct((ntok, D_MODEL), BF16),
                   jax.ShapeDtypeStruct((ntok, D_MODEL), kv_dtype),
                   jax.ShapeDtypeStruct((ntok, D_MODEL), kv_dtype)],
        scratch_shapes=[pltpu.VMEM((TM, D_MODEL), BF16)],
        compiler_params=_cparams(("arbitrary", "arbitrary")),
        name="qkv_proj",
    )(x, norm_g, mod, w_qkv, gain, hsum, hexp)


def _glu_kernel(x_ref, g_ref, mod_ref, wa_ref, wg_ref, ba_ref, bg_ref, u_ref, h_ref):
    _proj_prologue(x_ref, g_ref, mod_ref, h_ref)
    h = h_ref[...]
    a = jnp.dot(h, wa_ref[...], preferred_element_type=F32) + ba_ref[...]
    g = jnp.dot(h, wg_ref[...], preferred_element_type=F32) + bg_ref[...]
    u_ref[...] = a * _sigmoid(g)


def _glu_proj(x, norm_g, mod, w_pw1, b_pw1):
    tn = 512
    nj = D_MODEL // tn
    b = b_pw1.reshape(1, 2 * D_MODEL)
    return pl.pallas_call(
        _glu_kernel,
        grid=(N_GROUPS, nj),
        in_specs=[
            pl.BlockSpec((TM, D_MODEL), lambda g, j: (g, 0)),
            pl.BlockSpec((1, D_MODEL), lambda g, j: (0, 0)),
            pl.BlockSpec((1, 1, 6 * D_MODEL), lambda g, j: (g, 0, 0)),
            pl.BlockSpec((D_MODEL, tn), lambda g, j: (0, j)),
            pl.BlockSpec((D_MODEL, tn), lambda g, j: (0, j + nj)),
            pl.BlockSpec((1, tn), lambda g, j: (0, j)),
            pl.BlockSpec((1, tn), lambda g, j: (0, j + nj)),
        ],
        out_specs=pl.BlockSpec((TM, tn), lambda g, j: (g, j)),
        out_shape=jax.ShapeDtypeStruct((N_TOK, D_MODEL), F32),
        scratch_shapes=[pltpu.VMEM((TM, D_MODEL), BF16)],
        compiler_params=_cparams(("arbitrary", "arbitrary")),
        name="glu_proj",
    )(x, norm_g, mod, w_pw1, w_pw1, b, b)


def _scaled_proj_kernel(x_ref, g_ref, mod_ref, w_ref, cs_ref, y_ref, h_ref):
    _proj_prologue(x_ref, g_ref, mod_ref, h_ref)
    y_ref[...] = jnp.dot(h_ref[...], w_ref[...], preferred_element_type=F32) * cs_ref[...]


def _scaled_proj(x, norm_g, mod, w, colscale):
    n = w.shape[1]
    tn = n // 2
    return pl.pallas_call(
        _scaled_proj_kernel,
        grid=(N_GROUPS, n // tn),
        in_specs=[
            pl.BlockSpec((TM, D_MODEL), lambda g, j: (g, 0)),
            pl.BlockSpec((1, D_MODEL), lambda g, j: (0, 0)),
            pl.BlockSpec((1, 1, 6 * D_MODEL), lambda g, j: (g, 0, 0)),
            pl.BlockSpec((D_MODEL, tn), lambda g, j: (0, j)),
            pl.BlockSpec((1, tn), lambda g, j: (0, j)),
        ],
        out_specs=pl.BlockSpec((TM, tn), lambda g, j: (g, j)),
        out_shape=jax.ShapeDtypeStruct((N_TOK, n), F32),
        scratch_shapes=[pltpu.VMEM((TM, D_MODEL), BF16)],
        compiler_params=_cparams(("arbitrary", "arbitrary")),
        name="gla_proj",
    )(x, norm_g, mod, w, colscale)


_CTX_PART = pl.BlockSpec((TM, D_MODEL), lambda g: (jnp.minimum(g, N_CTX_GROUPS - 1), 0))
_LAT_PART = pl.BlockSpec((TM, D_MODEL), lambda g: (jnp.maximum(g - N_CTX_GROUPS, 0), 0))


def _group_part(ctx_ref, lat_ref):
    return jnp.where(pl.program_id(0) < N_CTX_GROUPS, ctx_ref[...], lat_ref[...])


def _oproj_kernel(x_ref, a_ctx_ref, a_lat_ref, w_ref, mod_ref, o_ref):
    r = jnp.dot(_group_part(a_ctx_ref, a_lat_ref), w_ref[...], preferred_element_type=F32)
    o_ref[...] = x_ref[...] + _mod_slice(mod_ref, 2) * r


def _out_proj(x, a_ctx, a_lat, w, mod):
    return pl.pallas_call(
        _oproj_kernel,
        grid=(N_GROUPS,),
        in_specs=[
            pl.BlockSpec((TM, D_MODEL), lambda g: (g, 0)),
            _CTX_PART, _LAT_PART,
            pl.BlockSpec((D_MODEL, D_MODEL), lambda g: (0, 0)),
            pl.BlockSpec((1, 1, 6 * D_MODEL), lambda g: (g, 0, 0)),
        ],
        out_specs=pl.BlockSpec((TM, D_MODEL), lambda g: (g, 0)),
        out_shape=jax.ShapeDtypeStruct((N_TOK, D_MODEL), F32),
        compiler_params=_cparams(("arbitrary",)),
        name="out_proj",
    )(x, a_ctx, a_lat, w, mod)


def _gla_oproj_kernel(x_ref, o_ctx_ref, o_lat_ref, gz_ref, og_ref, w_ref, mod_ref, out_ref):
    og = og_ref[...]
    o = _group_part(o_ctx_ref, o_lat_ref)
    parts = []
    for h in range(GLA_HEADS):
        oh = o[:, h * GLA_DV:(h + 1) * GLA_DV]
        ms = jnp.mean(oh * oh, axis=-1, keepdims=True)
        parts.append(oh * lax.rsqrt(ms + NORM_EPS) * og)
    y = jnp.concatenate(parts, axis=1)
    gz = gz_ref[...]
    a = (y * (gz * _sigmoid(gz))).astype(BF16)
    r = jnp.dot(a, w_ref[...], preferred_element_type=F32)
    out_ref[...] = x_ref[...] + _mod_slice(mod_ref, 2) * r


def _gla_out_proj(x, o_ctx, o_lat, y, o_norm, w, mod):
    return pl.pallas_call(
        _gla_oproj_kernel,
        grid=(N_GROUPS,),
        in_specs=[
            pl.BlockSpec((TM, D_MODEL), lambda g: (g, 0)),
            _CTX_PART, _LAT_PART,
            pl.BlockSpec((TM, GLA_VAL_DIM), lambda g: (g, 2)),
            pl.BlockSpec((1, GLA_DV), lambda g: (0, 0)),
            pl.BlockSpec((GLA_VAL_DIM, D_MODEL), lambda g: (0, 0)),
            pl.BlockSpec((1, 1, 6 * D_MODEL), lambda g: (g, 0, 0)),
        ],
        out_specs=pl.BlockSpec((TM, D_MODEL), lambda g: (g, 0)),
        out_shape=jax.ShapeDtypeStruct((N_TOK, D_MODEL), F32),
        compiler_params=_cparams(("arbitrary",)),
        name="gla_out_proj",
    )(x, o_ctx, o_lat, y, o_norm, w, mod)


def _ffn_kernel(x_ref, g_ref, mod_ref, wup_ref, bup_ref, wdw_ref, bdw_ref, wdn_ref, bdn_ref,
                out_ref, h_ref, acc_ref, ua0_ref, ug0_ref, ua1_ref, ug1_ref, act0_ref, act1_ref):
    x = x_ref[...]
    h_ref[...] = _norm_mod(x, g_ref[...], _mod_slice(mod_ref, 3), _mod_slice(mod_ref, 4)).astype(BF16)
    seq_len = jnp.where(pl.program_id(0) < N_CTX_GROUPS, SEQ, DEC_SEQ)
    pos = lax.broadcasted_iota(jnp.int32, (TM, 1), 0) & (seq_len - 1)
    first = pos == 0
    last = pos == seq_len - 1
    up_bufs = ((ua0_ref, ug0_ref), (ua1_ref, ug1_ref))
    act_bufs = (act0_ref, act1_ref)

    def up(c, slot):
        ca = pl.multiple_of(c * TF, TF)
        cg = pl.multiple_of(D_FF + c * TF, TF)
        h = h_ref[...]
        up_bufs[slot][0][...] = jnp.dot(h, wup_ref[:, pl.ds(ca, TF)], preferred_element_type=F32)
        up_bufs[slot][1][...] = jnp.dot(h, wup_ref[:, pl.ds(cg, TF)], preferred_element_type=F32)

    def conv(u_ref, col):
        u = u_ref[...] + bup_ref[:, pl.ds(col, TF)]
        w = wdw_ref[:, pl.ds(col, TF)]
        prev = jnp.where(first, 0.0, pltpu.roll(u, 1, 0))
        nxt = jnp.where(last, 0.0, pltpu.roll(u, TM - 1, 0))
        return w[0:1] * prev + w[1:2] * u + w[2:3] * nxt + bdw_ref[:, pl.ds(col, TF)]

    def gate(c, slot):
        a = conv(up_bufs[slot][0], pl.multiple_of(c * TF, TF))
        g = conv(up_bufs[slot][1], pl.multiple_of(D_FF + c * TF, TF))
        act_bufs[slot][...] = (g * _sigmoid(g) * a).astype(BF16)

    def down(c, slot):
        ca = pl.multiple_of(c * TF, TF)
        acc_ref[...] += jnp.dot(act_bufs[slot][...], wdn_ref[pl.ds(ca, TF), :], preferred_element_type=F32)

    n_chunks = D_FF // TF
    acc_ref[...] = jnp.zeros_like(acc_ref)
    up(0, 0)

    def body(i, carry):
        c = 2 * i
        up(c + 1, 1)
        gate(c, 0)
        down(c, 0)
        up(c + 2, 0)
        gate(c + 1, 1)
        down(c + 1, 1)
        return carry

    lax.fori_loop(0, (n_chunks - 1) // 2, body, 0)
    gate(n_chunks - 1, 0)
    down(n_chunks - 1, 0)
    out_ref[...] = x + _mod_slice(mod_ref, 5) * (acc_ref[...] + bdn_ref[...])


def _ffn(x, norm_g, mod, w_up, b_up, w_dw, b_dw, w_down, b_down):
    const = lambda g: (0, 0)
    resident = dict(pipeline_mode=pl.Buffered(1))
    return pl.pallas_call(
        _ffn_kernel,
        grid=(N_GROUPS,),
        in_specs=[
            pl.BlockSpec((TM, D_MODEL), lambda g: (g, 0)),
            pl.BlockSpec((1, D_MODEL), const),
            pl.BlockSpec((1, 1, 6 * D_MODEL), lambda g: (g, 0, 0)),
            pl.BlockSpec((D_MODEL, 2 * D_FF), const, **resident),
            pl.BlockSpec((1, 2 * D_FF), const),
            pl.BlockSpec((3, 2 * D_FF), const),
            pl.BlockSpec((1, 2 * D_FF), const),
            pl.BlockSpec((D_FF, D_MODEL), const, **resident),
            pl.BlockSpec((1, D_MODEL), const),
        ],
        out_specs=pl.BlockSpec((TM, D_MODEL), lambda g: (g, 0)),
        out_shape=jax.ShapeDtypeStruct((N_TOK, D_MODEL), F32),
        scratch_shapes=[pltpu.VMEM((TM, D_MODEL), BF16), pltpu.VMEM((TM, D_MODEL), F32)]
        + [pltpu.VMEM((TM, TF), F32)] * 4 + [pltpu.VMEM((TM, TF), BF16)] * 2,
        compiler_params=_cparams(("arbitrary",)),
        name="conv_ffn",
    )(x, norm_g, mod, w_up, b_up.reshape(1, -1), w_dw, b_dw.reshape(1, -1), w_down, b_down.reshape(1, -1))


def _pair_queries(q):
    lane = lax.broadcasted_iota(jnp.int32, q.shape, 1)
    zero = jnp.zeros_like(q)
    return jnp.concatenate([jnp.where(lane < NA_HEAD_DIM, q, zero),
                            jnp.where(lane < NA_HEAD_DIM, zero, q)], axis=0)


def _pair_merge(o2):
    n = o2.shape[0] // 2
    lane = lax.broadcasted_iota(jnp.int32, (n, o2.shape[1]), 1)
    return jnp.where(lane < NA_HEAD_DIM, o2[:n], o2[n:])


def _nt_dot(a, b):
    return lax.dot_general(a, b, (((1,), (1,)), ((), ())), preferred_element_type=F32)


def _ctx_attn_kernel(q_ref, k_ref, v_ref, o_ref):
    for b in range(TM // SEQ):
        rows = slice(b * SEQ, (b + 1) * SEQ)
        q2 = _pair_queries(q_ref[rows, :])
        s = _nt_dot(q2, k_ref[rows, :].astype(BF16))
        p = jnp.exp(s - jnp.max(s, axis=-1, keepdims=True))
        l = jnp.sum(p, axis=-1, keepdims=True)
        o2 = jnp.dot(p.astype(BF16), v_ref[rows, :].astype(BF16), preferred_element_type=F32) / l
        o_ref[rows, :] = _pair_merge(o2).astype(o_ref.dtype)


def _ctx_attention(q, k, v):
    spec = pl.BlockSpec((TM, 128), lambda g, hp: (g, hp))
    return pl.pallas_call(
        _ctx_attn_kernel,
        grid=(N_CTX_GROUPS, NA_HEADS // 2),
        in_specs=[spec, spec, spec],
        out_specs=spec,
        out_shape=jax.ShapeDtypeStruct((N_CTX_TOK, D_MODEL), BF16),
        compiler_params=_cparams(("arbitrary", "arbitrary")),
        name="ctx_attention",
    )(q, k, v)


_NA_BLOCKS = ((0, 8, 0), (0, 12, 1), (4, 12, 1), (8, 8, 2))


def _na_attn_kernel(q_ref, k_ref, v_ref, ck_ref, cv_ref, ba_ref, bm_ref, bc_ref, o_ref):
    bias_refs = (ba_ref, bm_ref, bc_ref)
    ck = ck_ref[0, 0].astype(BF16)
    cv = cv_ref[0, 0].astype(BF16)
    for blk, (row0, nrows, bidx) in enumerate(_NA_BLOCKS):
        rows = slice(blk * 4 * GRID_W, (blk + 1) * 4 * GRID_W)
        keys = slice(row0 * GRID_W, (row0 + nrows) * GRID_W)
        q2 = _pair_queries(q_ref[rows, :])
        bias = bias_refs[bidx][...]
        s_loc = _nt_dot(q2, k_ref[keys, :]) + bias.reshape(2 * 4 * GRID_W, nrows * GRID_W)
        s_ctx = _nt_dot(q2, ck)
        m = jnp.maximum(jnp.max(s_loc, axis=-1, keepdims=True), jnp.max(s_ctx, axis=-1, keepdims=True))
        p_loc = jnp.exp(s_loc - m)
        p_ctx = jnp.exp(s_ctx - m)
        l = jnp.sum(p_loc, axis=-1, keepdims=True) + jnp.sum(p_ctx, axis=-1, keepdims=True)
        o2 = (jnp.dot(p_loc.astype(BF16), v_ref[keys, :], preferred_element_type=F32)
              + jnp.dot(p_ctx.astype(BF16), cv, preferred_element_type=F32)) / l
        o_ref[rows, :] = _pair_merge(o2).astype(o_ref.dtype)


N_RPB_R = 2 * NA_WIN_R - 1
N_RPB_C = 2 * NA_WIN_C - 1


def _na_bias_kernel(rpb_ref, ba_ref, bm_ref, bc_ref):
    base = pl.program_id(0) * (N_RPB_R * N_RPB_C)
    qc = lax.broadcasted_iota(jnp.int32, (GRID_W, 2 * GRID_W), 0)
    lane = lax.broadcasted_iota(jnp.int32, (GRID_W, 2 * GRID_W), 1)
    kc = lane & (GRID_W - 1)
    right = lane >= GRID_W
    dcol = kc - qc + (NA_WIN_C - 1)
    c_start = jnp.clip(qc - NA_WIN_C // 2, 0, GRID_W - NA_WIN_C)
    in_win = (kc >= c_start) & (kc < c_start + NA_WIN_C)
    neg = jnp.full((GRID_W, 2 * GRID_W), NEG_INF, F32)
    tiles = {}

    def rpb_at(dr, j):
        return rpb_ref[base + dr * N_RPB_C + j] if 0 <= dr < N_RPB_R else jnp.float32(0.0)

    def pair_tile(dr):
        if dr not in tiles:
            acc = jnp.zeros((GRID_W, 2 * GRID_W), F32)
            for j in range(N_RPB_C):
                acc = jnp.where(dcol == j, jnp.where(right, rpb_at(dr + 1, j), rpb_at(dr, j)), acc)
            tiles[dr] = acc
        return tiles[dr]

    rows_total = DEC_SEQ // GRID_W
    for ref, blk in ((ba_ref, 0), (bm_ref, 1), (bc_ref, 3)):
        row0, nrows, _ = _NA_BLOCKS[blk]
        for rr in range(4):
            r = blk * 4 + rr
            r_start = min(max(r - NA_WIN_R // 2, 0), rows_total - NA_WIN_R)
            for ip in range(nrows // 2):
                krow = row0 + 2 * ip
                ok_l = r_start <= krow < r_start + NA_WIN_R
                ok_r = r_start <= krow + 1 < r_start + NA_WIN_R
                if ok_l or ok_r:
                    mask = in_win
                    if not ok_l:
                        mask = mask & right
                    if not ok_r:
                        mask = mask & jnp.logical_not(right)
                    tile = jnp.where(mask, pair_tile(krow - r + NA_WIN_R - 1), neg)
                else:
                    tile = neg
                ref[0, rr * GRID_W:(rr + 1) * GRID_W, ip * 2 * GRID_W:(ip + 1) * 2 * GRID_W] = tile


def _na_bias_tables(rpb):
    out = lambda n: pl.BlockSpec((1, 4 * GRID_W, n), lambda h: (h, 0, 0))
    shape = lambda n: jax.ShapeDtypeStruct((NA_HEADS, 4 * GRID_W, n), F32)
    return pl.pallas_call(
        _na_bias_kernel,
        grid=(NA_HEADS,),
        in_specs=[pl.BlockSpec(memory_space=pltpu.SMEM)],
        out_specs=[out(8 * GRID_W), out(12 * GRID_W), out(8 * GRID_W)],
        out_shape=[shape(8 * GRID_W), shape(12 * GRID_W), shape(8 * GRID_W)],
        compiler_params=_cparams(("arbitrary",)),
        name="na_bias",
    )(rpb.astype(F32).reshape(-1))


def _na_attention(q, k, v, cache_k, cache_v, layer_j, bias_tables):
    ba, bm, bc = bias_tables
    tok = pl.BlockSpec((TM, 128), lambda hp, b: (b, hp))
    cache = pl.BlockSpec((1, 1, PAST_LEN, 128), lambda hp, b: (b, layer_j, 0, hp))
    bias = lambda n: pl.BlockSpec((2, 4 * GRID_W, n), lambda hp, b: (hp, 0, 0))
    return pl.pallas_call(
        _na_attn_kernel,
        grid=(NA_HEADS // 2, DEC_BATCH),
        in_specs=[tok, tok, tok, cache, cache, bias(8 * GRID_W), bias(12 * GRID_W), bias(8 * GRID_W)],
        out_specs=tok,
        out_shape=jax.ShapeDtypeStruct((N_LAT_TOK, D_MODEL), BF16),
        compiler_params=_cparams(("arbitrary", "arbitrary")),
        name="na_attention",
    )(q, k, v, cache_k, cache_v, ba, bm, bc)


CONV_PAD = 16
CONV_ROWS = 128
CONV_LANES = 128


def _conv_tail_kernel(x_ref, u_ref, wdw_ref, bdw_ref, lng_ref, lnb_ref, w_ref, b_ref, mod_ref,
                      o_ref, pad_ref, a_ref):
    half = CONV_WIDTH // 2
    win = CONV_ROWS + 8
    zeros = jnp.zeros((CONV_PAD, D_MODEL), F32)

    def conv_sequence(base, seq_len):
        pad_ref[0:CONV_PAD, :] = zeros
        pad_ref[CONV_PAD + seq_len:2 * CONV_PAD + seq_len, :] = zeros
        pad_ref[CONV_PAD:CONV_PAD + seq_len, :] = u_ref[base:base + seq_len, :]

        def rows(i, carry):
            r0 = pl.multiple_of(i * CONV_ROWS, CONV_ROWS)
            strips = []
            for l0 in range(0, D_MODEL, CONV_LANES):
                lanes = slice(l0, l0 + CONV_LANES)
                acc = jnp.zeros((CONV_ROWS, CONV_LANES), F32) + bdw_ref[:, lanes]
                for b in range(8):
                    z = None
                    for a in range(-2, 2):
                        t = 8 * a + b + half
                        if not 0 <= t < CONV_WIDTH:
                            continue
                        term = wdw_ref[t:t + 1, lanes] * pad_ref[pl.ds(r0 + (CONV_PAD + 8 * a), win), lanes]
                        z = term if z is None else z + term
                    if b:
                        z = pltpu.roll(z, win - b, 0)
                    acc = acc + z[:CONV_ROWS]
                strips.append(acc)
            acc = jnp.concatenate(strips, axis=1)
            mu = jnp.mean(acc, axis=-1, keepdims=True)
            cen = acc - mu
            var = jnp.mean(cen * cen, axis=-1, keepdims=True)
            y = cen * lax.rsqrt(var + NORM_EPS) * lng_ref[...] + lnb_ref[...]
            a_ref[pl.ds(base + r0, CONV_ROWS), :] = (y * _sigmoid(y)).astype(BF16)
            return carry

        lax.fori_loop(0, seq_len // CONV_ROWS, rows, 0)

    @pl.when(pl.program_id(0) < N_CTX_GROUPS)
    def _():
        for s in range(TM // SEQ):
            conv_sequence(s * SEQ, SEQ)

    @pl.when(pl.program_id(0) >= N_CTX_GROUPS)
    def _():
        conv_sequence(0, DEC_SEQ)

    r = jnp.dot(a_ref[...], w_ref[...], preferred_element_type=F32) + b_ref[...]
    o_ref[...] = x_ref[...] + _mod_slice(mod_ref, 2) * r


def _conv_tail(x, u, w_dw, b_dw, ln_g, ln_b, w_pw2, b_pw2, mod):
    tok = pl.BlockSpec((TM, D_MODEL), lambda g: (g, 0))
    const = lambda g: (0, 0)
    vec = pl.BlockSpec((1, D_MODEL), const)
    return pl.pallas_call(
        _conv_tail_kernel,
        grid=(N_GROUPS,),
        in_specs=[tok, tok, pl.BlockSpec((CONV_WIDTH, D_MODEL), const), vec, vec, vec,
                  pl.BlockSpec((D_MODEL, D_MODEL), const), vec,
                  pl.BlockSpec((1, 1, 6 * D_MODEL), lambda g: (g, 0, 0))],
        out_specs=tok,
        out_shape=jax.ShapeDtypeStruct((N_TOK, D_MODEL), F32),
        scratch_shapes=[pltpu.VMEM((TM + 2 * CONV_PAD, D_MODEL), F32),
                        pltpu.VMEM((TM, D_MODEL), BF16)],
        compiler_params=_cparams(("arbitrary",)),
        name="conv_tail",
    )(x, u, w_dw, b_dw.reshape(1, -1), ln_g.reshape(1, -1), ln_b.reshape(1, -1), w_pw2,
      b_pw2.reshape(1, -1), mod)


def _log_sigmoid(z):
    return -(jnp.maximum(-z, 0.0) + jnp.log(1.0 + jnp.exp(-jnp.abs(z))))


def _gla_scan_kernel(*refs, seq_len, has_init, emit_state):
    it = iter(refs)
    q_ref, k_ref, v_ref, r_ref, w2_ref, bgk_ref = (next(it) for _ in range(6))
    s0_refs = (next(it), next(it)) if has_init else None
    o_ref = next(it)
    st_out = (next(it), next(it)) if emit_state else None
    g_ref, cum_ref, st_ref = (next(it) for _ in range(3))

    c = GLA_CHUNK
    n_chunks = seq_len // c
    rlow = r_ref[...].astype(BF16)
    row = lax.broadcasted_iota(jnp.int32, (c, 1), 0)
    sub = row & 7
    ri = lax.broadcasted_iota(jnp.int32, (c, c), 0)
    ci = lax.broadcasted_iota(jnp.int32, (c, c), 1)

    for d in range(2):
        reverse = d == 1
        z = jnp.dot(rlow, w2_ref[d].astype(BF16), preferred_element_type=F32) + bgk_ref[d]
        g_ref[...] = _log_sigmoid(z) * (1.0 / GLA_GATE_NORM)
        if has_init:
            st_ref[...] = s0_refs[d][0, 0].T
        else:
            st_ref[...] = jnp.zeros_like(st_ref)

        order = range(n_chunks - 1, -1, -1) if reverse else range(n_chunks)
        for ch in order:
            base = ch * c
            rows = slice(base, base + c)
            q = q_ref[rows, :]
            k = k_ref[rows, :]
            v = v_ref[rows, :].astype(BF16)
            cum = g_ref[rows, :]
            sh = 1
            while sh < c:
                if reverse:
                    cum = cum + jnp.where(row < c - sh, pltpu.roll(cum, c - sh, 0), 0.0)
                else:
                    cum = cum + jnp.where(row >= sh, pltpu.roll(cum, sh, 0), 0.0)
                sh *= 2
            cum_ref[...] = cum

            att = jnp.zeros((c, c), F32)
            for lag in range(8):
                if lag == 0:
                    col = jnp.sum(q * k, axis=-1, keepdims=True)
                else:
                    shift = c - lag if reverse else lag
                    ok = (sub <= 7 - lag) if reverse else (sub >= lag)
                    e = jnp.exp(jnp.where(ok, cum - pltpu.roll(cum, shift, 0), 0.0))
                    col = jnp.sum(jnp.where(ok, q * pltpu.roll(k, shift, 0) * e, 0.0), axis=-1, keepdims=True)
                att = jnp.where(ci == (ri + lag if reverse else ri - lag), col, att)

            m = 16
            while m <= c:
                half = m // 2
                pieces = []
                for blk in range(c // m):
                    arow = blk * m + (half if reverse else half - 1)
                    pieces.append(jnp.broadcast_to(cum_ref[arow:arow + 1, :], (m, GLA_DK)))
                anchor = pieces[0] if len(pieces) == 1 else jnp.concatenate(pieces, axis=0)
                in_first = (row & (m - 1)) < half
                later = in_first if reverse else jnp.logical_not(in_first)
                fac = jnp.exp(jnp.where(later, cum - anchor, anchor - cum))
                qs = jnp.where(later, q * fac, 0.0).astype(BF16)
                ks = jnp.where(later, 0.0, k * fac).astype(BF16)
                p = _nt_dot(qs, ks)
                if m < c:
                    p = jnp.where((ri & -m) == (ci & -m), p, 0.0)
                att = att + p
                m *= 2

            last = 0 if reverse else c - 1
            total = cum_ref[last:last + 1, :]
            q_in = (q * jnp.exp(cum)).astype(BF16)
            k_out = (k * jnp.exp(total - cum)).astype(BF16)
            st = st_ref[...]
            o = (jnp.dot(att.astype(BF16), v, preferred_element_type=F32)
                 + _nt_dot(q_in, st.astype(BF16)))
            if reverse:
                o_ref[rows, :] += o
            else:
                o_ref[rows, :] = o
            kv = lax.dot_general(v, k_out, (((0,), (0,)), ((), ())), preferred_element_type=F32)
            st_ref[...] = st * jnp.exp(total) + kv

        if emit_state:
            st_out[d][0, 0] = st_ref[...].T


def _gla_scan(y, w2pad, b_gk, init_states, seq_len, blk0, n_seq, emit_state):
    has_init = init_states is not None
    qk = lambda off: pl.BlockSpec((seq_len, GLA_DK), lambda b, h: (b + blk0, h + off))
    state = pl.BlockSpec((1, 1, GLA_DK, GLA_DV), lambda b, h: (b, h, 0, 0))
    in_specs = [
        qk(0), qk(GLA_KEY_DIM // GLA_DK),
        pl.BlockSpec((seq_len, GLA_DV), lambda b, h: (b + blk0, h + 2 * GLA_KEY_DIM // GLA_DV)),
        pl.BlockSpec((seq_len, 128), lambda b, h: (b + blk0, (2 * GLA_KEY_DIM + 2 * GLA_VAL_DIM) // 128)),
        pl.BlockSpec((2, 128, GLA_DK), lambda b, h: (0, 0, h)),
        pl.BlockSpec((2, 1, GLA_DK), lambda b, h: (0, 0, h)),
    ]
    args = [y, y, y, y, w2pad, b_gk]
    if has_init:
        in_specs += [state, state]
        args += list(init_states)
    out_specs = [pl.BlockSpec((seq_len, GLA_DV), lambda b, h: (b, h))]
    out_shape = [jax.ShapeDtypeStruct((n_seq * seq_len, GLA_VAL_DIM), F32)]
    if emit_state:
        out_specs += [state, state]
        out_shape += [jax.ShapeDtypeStruct((n_seq, GLA_HEADS, GLA_DK, GLA_DV), F32)] * 2
    return pl.pallas_call(
        functools.partial(_gla_scan_kernel, seq_len=seq_len, has_init=has_init, emit_state=emit_state),
        grid=(n_seq, GLA_HEADS),
        in_specs=in_specs,
        out_specs=out_specs,
        out_shape=out_shape,
        scratch_shapes=[pltpu.VMEM((seq_len, GLA_DK), F32), pltpu.VMEM((GLA_CHUNK, GLA_DK), F32),
                        pltpu.VMEM((GLA_DV, GLA_DK), F32)],
        compiler_params=_cparams(("arbitrary", "arbitrary")),
        name="gla_scan",
    )(*args)


_GROUP_COND_ROW = np.array([0] * N_CTX_GROUPS + list(range(1, DEC_BATCH + 1)))


def kernel(x_prompt, x_sample, c, cache_attn_k, cache_attn_v, state_gla_fwd, state_gla_bwd, c_ctx,
           mod_w, mod_b, norm1_g, norm2_g,
           attn_w_qkv, attn_w_o, attn_q_norm, attn_k_norm, attn_rpb,
           conv_w_pw1, conv_b_pw1, conv_w_dw, conv_b_dw, conv_ln_g, conv_ln_b, conv_w_pw2, conv_b_pw2,
           gla_w_q, gla_w_k, gla_w_v, gla_w_g, gla_w_gk1, gla_w_gk2, gla_b_gk, gla_o_norm, gla_w_o,
           ffn_w_up, ffn_b_up, ffn_w_dw, ffn_b_dw, ffn_w_down, ffn_b_down):
    x = jnp.concatenate([x_prompt.reshape(N_CTX_TOK, D_MODEL), x_sample.reshape(N_LAT_TOK, D_MODEL)], axis=0)
    cond = jnp.concatenate([c_ctx[None, :], c, jnp.zeros((N_COND - 1 - DEC_BATCH, D_MODEL), F32)], axis=0)
    mod_all = _modulation(cond, mod_w, mod_b)
    mod_all = mod_all[:, _GROUP_COND_ROW][:, :, None, :]

    cache_k = cache_attn_k.reshape(DEC_BATCH, -1, PAST_LEN, D_MODEL)
    cache_v = cache_attn_v.reshape(DEC_BATCH, -1, PAST_LEN, D_MODEL)
    new_k, new_v, new_sf, new_sb = [], [], [], []
    for i in range(DEPTH):
        kind, j = i % N_MIXERS, i // N_MIXERS
        mod = mod_all[i]
        n1 = norm1_g[i].reshape(1, D_MODEL)
        if kind == 0:
            w_qkv = attn_w_qkv[j].astype(BF16)
            gain = jnp.concatenate([jnp.tile(attn_q_norm[j], NA_HEADS) * (NA_HEAD_DIM ** -0.5),
                                    jnp.tile(attn_k_norm[j], NA_HEADS),
                                    jnp.ones((D_MODEL,), F32)]).reshape(1, 3 * D_MODEL)
            qp, kp, vp = _qkv_proj(x, n1, mod, w_qkv, gain, 0, N_CTX_GROUPS, F32)
            qs, ks, vs = _qkv_proj(x, n1, mod, w_qkv, gain, N_CTX_GROUPS, N_GROUPS - N_CTX_GROUPS, BF16)
            o_ctx = _ctx_attention(qp, kp, vp)
            o_lat = _na_attention(qs, ks, vs, cache_k, cache_v, j, _na_bias_tables(attn_rpb[j]))
            x = _out_proj(x, o_ctx, o_lat, attn_w_o[j].astype(BF16), mod)
            new_k.append(kp.reshape(BATCH, SEQ, NA_HEADS, NA_HEAD_DIM))
            new_v.append(vp.reshape(BATCH, SEQ, NA_HEADS, NA_HEAD_DIM))
        elif kind == 1:
            u = _glu_proj(x, n1, mod, conv_w_pw1[j].astype(BF16), conv_b_pw1[j])
            x = _conv_tail(x, u, conv_w_dw[j], conv_b_dw[j], conv_ln_g[j], conv_ln_b[j],
                           conv_w_pw2[j].astype(BF16), conv_b_pw2[j], mod)
        else:
            pad = GLA_PROJ_N - 2 * GLA_KEY_DIM - 2 * GLA_VAL_DIM - 2 * GLA_GATE_RANK
            w_cat = jnp.concatenate([gla_w_q[j], gla_w_k[j], gla_w_v[j], gla_w_g[j], gla_w_gk1[j, 0],
                                     gla_w_gk1[j, 1], jnp.zeros((D_MODEL, pad), F32)], axis=1).astype(BF16)
            colscale = jnp.concatenate([jnp.full((GLA_KEY_DIM,), GLA_DK ** -0.5, F32),
                                        jnp.ones((GLA_PROJ_N - GLA_KEY_DIM,), F32)]).reshape(1, GLA_PROJ_N)
            y = _scaled_proj(x, n1, mod, w_cat, colscale)
            w2pad = jnp.zeros((2, 128, GLA_KEY_DIM), F32)
            w2pad = w2pad.at[0, :GLA_GATE_RANK].set(gla_w_gk2[j, 0])
            w2pad = w2pad.at[1, GLA_GATE_RANK:2 * GLA_GATE_RANK].set(gla_w_gk2[j, 1])
            b_gk = gla_b_gk[j].reshape(2, 1, GLA_KEY_DIM)
            o_ctx, sf, sb = _gla_scan(y, w2pad, b_gk, None, SEQ, 0, BATCH, True)
            (o_lat,) = _gla_scan(y, w2pad, b_gk, (state_gla_fwd[:, j], state_gla_bwd[:, j]),
                                 DEC_SEQ, N_CTX_GROUPS, DEC_BATCH, False)
            x = _gla_out_proj(x, o_ctx, o_lat, y, gla_o_norm[j].reshape(1, GLA_DV),
                              gla_w_o[j].astype(BF16), mod)
            new_sf.append(sf)
            new_sb.append(sb)
        x = _ffn(x, norm2_g[i].reshape(1, D_MODEL), mod, ffn_w_up[i].astype(BF16), ffn_b_up[i],
                 ffn_w_dw[i], ffn_b_dw[i], ffn_w_down[i].astype(BF16), ffn_b_down[i])

    y_prompt = x[:N_CTX_TOK].reshape(BATCH, SEQ, D_MODEL)
    y_sample = x[N_CTX_TOK:].reshape(DEC_BATCH, DEC_SEQ, D_MODEL)
    return (y_prompt, y_sample, jnp.stack(new_k, axis=1), jnp.stack(new_v, axis=1),
            jnp.stack(new_sf, axis=1), jnp.stack(new_sb, axis=1))
```

```python
import functools

import numpy as np
import jax
import jax.numpy as jnp
from jax import lax
from jax.experimental import pallas as pl
from jax.experimental.pallas import tpu as pltpu

F32 = jnp.float32
BF16 = jnp.bfloat16

D_MODEL = 1024
BATCH = 16
SEQ = 256
DEPTH = 4
DEC_BATCH = 4
DEC_SEQ = 1024
PAST_LEN = 512
GRID_W = 64
N_MIXERS = 3
NA_HEADS = 16
NA_HEAD_DIM = 64
NA_WIN_R = 8
NA_WIN_C = 16
CONV_WIDTH = 31
GLA_HEADS = 4
GLA_KEY_DIM = 512
GLA_VAL_DIM = 1024
GLA_DK = 128
GLA_DV = 256
GLA_GATE_RANK = 16
GLA_GATE_NORM = 16.0
D_FF = 2816
NORM_EPS = 1e-6
NEG_INF = -1e30

TM = 1024
N_CTX_TOK = BATCH * SEQ
N_LAT_TOK = DEC_BATCH * DEC_SEQ
N_TOK = N_CTX_TOK + N_LAT_TOK
N_CTX_GROUPS = N_CTX_TOK // TM
N_GROUPS = N_TOK // TM
N_COND = 8
TF = 256
PROJ_SLAB = 256
FFN_SLAB = SEQ
FFN_HALO = 8
GLA_CHUNK = 256
GLA_PROJ_N = 3328
VMEM_LIMIT = 60 * 1024 * 1024


def _cparams(sem, flags=None):
    return pltpu.CompilerParams(dimension_semantics=sem, vmem_limit_bytes=VMEM_LIMIT, flags=flags)


def _sigmoid(x):
    return 1.0 / (1.0 + jnp.exp(-x))


def _norm_mod(x, g, shift, scale):
    ms = jnp.mean(x * x, axis=-1, keepdims=True)
    y = x * lax.rsqrt(ms + NORM_EPS) * g
    return y * (1.0 + scale) + shift


def _mod_slice(mod_ref, idx):
    return mod_ref[0, :, idx * D_MODEL:(idx + 1) * D_MODEL]


def _mod_kernel(cond_ref, w_ref, b_ref, o_ref):
    c = cond_ref[...]
    s = (c * _sigmoid(c)).astype(BF16)
    o_ref[0] = jnp.dot(s, w_ref[0].astype(BF16), preferred_element_type=F32) + b_ref[0]


def _modulation(cond, mod_w, mod_b):
    tn = 1536
    n = 6 * D_MODEL
    return pl.pallas_call(
        _mod_kernel,
        grid=(DEPTH, n // tn),
        in_specs=[
            pl.BlockSpec((N_COND, D_MODEL), lambda l, j: (0, 0)),
            pl.BlockSpec((1, D_MODEL, tn), lambda l, j: (l, 0, j)),
            pl.BlockSpec((1, 1, tn), lambda l, j: (l, 0, j)),
        ],
        out_specs=pl.BlockSpec((1, N_COND, tn), lambda l, j: (l, 0, j)),
        out_shape=jax.ShapeDtypeStruct((DEPTH, N_COND, n), F32),
        compiler_params=_cparams(("arbitrary", "arbitrary")),
        name="modulation",
    )(cond, mod_w, mod_b.reshape(DEPTH, 1, n))


def _proj_slabs(x_ref, g_ref, mod_ref, h_ref, epilogue):
    def run(first):
        for s in range(TM // PROJ_SLAB):
            rows = slice(s * PROJ_SLAB, (s + 1) * PROJ_SLAB)
            if first:
                h = _norm_mod(x_ref[rows, :], g_ref[...], _mod_slice(mod_ref, 0), _mod_slice(mod_ref, 1))
                h_ref[rows, :] = h.astype(BF16)
            epilogue(rows, h_ref[rows, :])

    pl.when(pl.program_id(1) == 0)(lambda: run(True))
    pl.when(pl.program_id(1) != 0)(lambda: run(False))


def _head_rms(acc, gain, hsum_ref, hexp_ref):
    ms = jnp.dot((acc * acc).astype(BF16), hsum_ref[...], preferred_element_type=F32)
    inv = lax.rsqrt(ms + NORM_EPS)
    hi = inv.astype(BF16)
    lo = (inv - hi.astype(F32)).astype(BF16)
    inv_full = jnp.dot(jnp.concatenate([hi, lo], axis=1), hexp_ref[...], preferred_element_type=F32)
    return acc * inv_full * gain


def _qkv_kernel(x_ref, g_ref, mod_ref, w_ref, gain_ref, hsum_ref, hexp_ref,
                q_ref, k_ref, v_ref, h_ref):
    j = pl.program_id(1)

    def column_step(out_ref, normed, first):
        for s in range(TM // PROJ_SLAB):
            rows = slice(s * PROJ_SLAB, (s + 1) * PROJ_SLAB)
            if first:
                h = _norm_mod(x_ref[rows, :], g_ref[...], _mod_slice(mod_ref, 0), _mod_slice(mod_ref, 1))
                h_ref[rows, :] = h.astype(BF16)
            acc = jnp.dot(h_ref[rows, :], w_ref[...], preferred_element_type=F32)
            if normed:
                acc = _head_rms(acc, gain_ref[...], hsum_ref, hexp_ref)
            out_ref[rows, :] = acc.astype(out_ref.dtype)

    pl.when(j == 0)(lambda: column_step(q_ref, True, True))
    pl.when(j == 1)(lambda: column_step(k_ref, True, False))
    pl.when(j == 2)(lambda: column_step(v_ref, False, False))


def _head_matrices():
    lane = np.arange(D_MODEL)
    hsum = np.zeros((D_MODEL, 128), np.float32)
    hsum[lane, lane // NA_HEAD_DIM] = 1.0 / NA_HEAD_DIM
    hexp = np.zeros((128, D_MODEL), np.float32)
    hexp[lane // NA_HEAD_DIM, lane] = 1.0
    return jnp.asarray(hsum, BF16), jnp.asarray(np.concatenate([hexp, hexp], 0), BF16)


def _qkv_proj(x, norm_g, mod, w_qkv, gain, group0, n_groups, kv_dtype):
    hsum, hexp = _head_matrices()
    ntok = n_groups * TM
    row = lambda g, j: (g, 0)
    return pl.pallas_call(
        _qkv_kernel,
        grid=(n_groups, 3),
        in_specs=[
            pl.BlockSpec((TM, D_MODEL), lambda g, j: (g + group0, 0)),
            pl.BlockSpec((1, D_MODEL), lambda g, j: (0, 0)),
            pl.BlockSpec((1, 1, 6 * D_MODEL), lambda g, j: (g + group0, 0, 0)),
            pl.BlockSpec((D_MODEL, D_MODEL), lambda g, j: (0, j)),
            pl.BlockSpec((1, D_MODEL), lambda g, j: (0, j)),
            pl.BlockSpec((D_MODEL, 128), lambda g, j: (0, 0)),
            pl.BlockSpec((256, D_MODEL), lambda g, j: (0, 0)),
        ],
        out_specs=[pl.BlockSpec((TM, D_MODEL), row)] * 3,
        out_shape=[jax.ShapeDtypeStruct((ntok, D_MODEL), BF16),
                   jax.ShapeDtypeStruct((ntok, D_MODEL), kv_dtype),
                   jax.ShapeDtypeStruct((ntok, D_MODEL), kv_dtype)],
        scratch_shapes=[pltpu.VMEM((TM, D_MODEL), BF16)],
        compiler_params=_cparams(("arbitrary", "arbitrary")),
        name="qkv_proj",
    )(x, norm_g, mod, w_qkv, gain, hsum, hexp)


def _glu_kernel(x_ref, g_ref, mod_ref, wa_ref, wg_ref, ba_ref, bg_ref, u_ref, h_ref):
    def epilogue(rows, h):
        a = jnp.dot(h, wa_ref[...], preferred_element_type=F32) + ba_ref[...]
        g = jnp.dot(h, wg_ref[...], preferred_element_type=F32) + bg_ref[...]
        u_ref[rows, :] = a * _sigmoid(g)

    _proj_slabs(x_ref, g_ref, mod_ref, h_ref, epilogue)


def _glu_proj(x, norm_g, mod, w_pw1, b_pw1):
    tn = 512
    nj = D_MODEL // tn
    b = b_pw1.reshape(1, 2 * D_MODEL)
    return pl.pallas_call(
        _glu_kernel,
        grid=(N_GROUPS, nj),
        in_specs=[
            pl.BlockSpec((TM, D_MODEL), lambda g, j: (g, 0)),
            pl.BlockSpec((1, D_MODEL), lambda g, j: (0, 0)),
            pl.BlockSpec((1, 1, 6 * D_MODEL), lambda g, j: (g, 0, 0)),
            pl.BlockSpec((D_MODEL, tn), lambda g, j: (0, j)),
            pl.BlockSpec((D_MODEL, tn), lambda g, j: (0, j + nj)),
            pl.BlockSpec((1, tn), lambda g, j: (0, j)),
            pl.BlockSpec((1, tn), lambda g, j: (0, j + nj)),
        ],
        out_specs=pl.BlockSpec((TM, tn), lambda g, j: (g, j)),
        out_shape=jax.ShapeDtypeStruct((N_TOK, D_MODEL), F32),
        scratch_shapes=[pltpu.VMEM((TM, D_MODEL), BF16)],
        compiler_params=_cparams(("arbitrary", "arbitrary")),
        name="glu_proj",
    )(x, norm_g, mod, w_pw1, w_pw1, b, b)


def _scaled_proj_kernel(x_ref, g_ref, mod_ref, w_ref, cs_ref, y_ref, h_ref):
    def epilogue(rows, h):
        y_ref[rows, :] = jnp.dot(h, w_ref[...], preferred_element_type=F32) * cs_ref[...]

    _proj_slabs(x_ref, g_ref, mod_ref, h_ref, epilogue)


def _scaled_proj(x, norm_g, mod, w, colscale):
    n = w.shape[1]
    tn = n // 2
    return pl.pallas_call(
        _scaled_proj_kernel,
        grid=(N_GROUPS, n // tn),
        in_specs=[
            pl.BlockSpec((TM, D_MODEL), lambda g, j: (g, 0)),
            pl.BlockSpec((1, D_MODEL), lambda g, j: (0, 0)),
            pl.BlockSpec((1, 1, 6 * D_MODEL), lambda g, j: (g, 0, 0)),
            pl.BlockSpec((D_MODEL, tn), lambda g, j: (0, j)),
            pl.BlockSpec((1, tn), lambda g, j: (0, j)),
        ],
        out_specs=pl.BlockSpec((TM, tn), lambda g, j: (g, j)),
        out_shape=jax.ShapeDtypeStruct((N_TOK, n), F32),
        scratch_shapes=[pltpu.VMEM((TM, D_MODEL), BF16)],
        compiler_params=_cparams(("arbitrary", "arbitrary")),
        name="gla_proj",
    )(x, norm_g, mod, w, colscale)


_CTX_PART = pl.BlockSpec((TM, D_MODEL), lambda g: (jnp.minimum(g, N_CTX_GROUPS - 1), 0))
_LAT_PART = pl.BlockSpec((TM, D_MODEL), lambda g: (jnp.maximum(g - N_CTX_GROUPS, 0), 0))


def _group_part(ctx_ref, lat_ref, rows):
    return jnp.where(pl.program_id(0) < N_CTX_GROUPS, ctx_ref[rows, :], lat_ref[rows, :])


def _row_slabs():
    return [slice(s * PROJ_SLAB, (s + 1) * PROJ_SLAB) for s in range(TM // PROJ_SLAB)]


def _oproj_kernel(x_ref, a_ctx_ref, a_lat_ref, w_ref, mod_ref, o_ref):
    for rows in _row_slabs():
        r = jnp.dot(_group_part(a_ctx_ref, a_lat_ref, rows), w_ref[...], preferred_element_type=F32)
        o_ref[rows, :] = x_ref[rows, :] + _mod_slice(mod_ref, 2) * r


def _out_proj(x, a_ctx, a_lat, w, mod):
    return pl.pallas_call(
        _oproj_kernel,
        grid=(N_GROUPS,),
        in_specs=[
            pl.BlockSpec((TM, D_MODEL), lambda g: (g, 0)),
            _CTX_PART, _LAT_PART,
            pl.BlockSpec((D_MODEL, D_MODEL), lambda g: (0, 0)),
            pl.BlockSpec((1, 1, 6 * D_MODEL), lambda g: (g, 0, 0)),
        ],
        out_specs=pl.BlockSpec((TM, D_MODEL), lambda g: (g, 0)),
        out_shape=jax.ShapeDtypeStruct((N_TOK, D_MODEL), F32),
        compiler_params=_cparams(("arbitrary",)),
        name="out_proj",
    )(x, a_ctx, a_lat, w, mod)


def _gla_oproj_kernel(x_ref, o_ctx_ref, o_lat_ref, gz_ref, og_ref, w_ref, mod_ref, out_ref):
    og = og_ref[...]
    for rows in _row_slabs():
        o = _group_part(o_ctx_ref, o_lat_ref, rows)
        parts = []
        for h in range(GLA_HEADS):
            oh = o[:, h * GLA_DV:(h + 1) * GLA_DV]
            ms = jnp.mean(oh * oh, axis=-1, keepdims=True)
            parts.append(oh * lax.rsqrt(ms + NORM_EPS) * og)
        y = jnp.concatenate(parts, axis=1)
        gz = gz_ref[rows, :]
        a = (y * (gz * _sigmoid(gz))).astype(BF16)
        r = jnp.dot(a, w_ref[...], preferred_element_type=F32)
        out_ref[rows, :] = x_ref[rows, :] + _mod_slice(mod_ref, 2) * r


def _gla_out_proj(x, o_ctx, o_lat, y, o_norm, w, mod):
    return pl.pallas_call(
        _gla_oproj_kernel,
        grid=(N_GROUPS,),
        in_specs=[
            pl.BlockSpec((TM, D_MODEL), lambda g: (g, 0)),
            _CTX_PART, _LAT_PART,
            pl.BlockSpec((TM, GLA_VAL_DIM), lambda g: (g, 2)),
            pl.BlockSpec((1, GLA_DV), lambda g: (0, 0)),
            pl.BlockSpec((GLA_VAL_DIM, D_MODEL), lambda g: (0, 0)),
            pl.BlockSpec((1, 1, 6 * D_MODEL), lambda g: (g, 0, 0)),
        ],
        out_specs=pl.BlockSpec((TM, D_MODEL), lambda g: (g, 0)),
        out_shape=jax.ShapeDtypeStruct((N_TOK, D_MODEL), F32),
        compiler_params=_cparams(("arbitrary",)),
        name="gla_out_proj",
    )(x, o_ctx, o_lat, y, o_norm, w, mod)


def _ffn_kernel(x_ref, g_ref, mod_ref, wup_ref, bup_ref, wdw_ref, bdw_ref, wdn_ref, bdn_ref,
                out_ref, h_ref, acc_ref, ua0_ref, ug0_ref, ua1_ref, ug1_ref, act0_ref, act1_ref):
    x = x_ref[...]
    h_ref[...] = _norm_mod(x, g_ref[...], _mod_slice(mod_ref, 3), _mod_slice(mod_ref, 4)).astype(BF16)
    is_lat = pl.program_id(0) >= N_CTX_GROUPS
    up_bufs = ((ua0_ref, ug0_ref), (ua1_ref, ug1_ref))
    act_bufs = (act0_ref, act1_ref)
    n_slabs = TM // FFN_SLAB
    win = FFN_SLAB + 2 * FFN_HALO

    def chunk_cols(c):
        return pl.multiple_of(c * TF, TF), pl.multiple_of(D_FF + c * TF, TF)

    def up_slab(c, slot, s):
        h = h_ref[s * FFN_SLAB:(s + 1) * FFN_SLAB, :]
        for buf, col in zip(up_bufs[slot], chunk_cols(c)):
            res = jnp.dot(h, wup_ref[:, pl.ds(col, TF)], preferred_element_type=F32)
            pad = jnp.broadcast_to(-bup_ref[:, pl.ds(col, TF)], (FFN_HALO, TF))
            base = s * win
            buf[base + FFN_HALO:base + FFN_HALO + FFN_SLAB, :] = res
            if s == 0:
                buf[0:FFN_HALO, :] = pad
            else:
                buf[base - FFN_HALO:base, :] = jnp.where(is_lat, res[:FFN_HALO], pad)
            if s == n_slabs - 1:
                buf[base + win - FFN_HALO:base + win, :] = pad
            else:
                buf[base + win:base + win + FFN_HALO, :] = jnp.where(is_lat, res[FFN_SLAB - FFN_HALO:], pad)

    def conv_slab(buf, col, s):
        r0 = s * win + FFN_HALO
        prev = buf[r0 - 1:r0 - 1 + FFN_SLAB, :]
        mid = buf[r0:r0 + FFN_SLAB, :]
        nxt = buf[r0 + 1:r0 + 1 + FFN_SLAB, :]
        w = wdw_ref[:, pl.ds(col, TF)]
        bias = bdw_ref[:, pl.ds(col, TF)] + bup_ref[:, pl.ds(col, TF)] * (w[0:1] + w[1:2] + w[2:3])
        return w[0:1] * prev + w[1:2] * mid + w[2:3] * nxt + bias

    def gate_slab(c, slot, s):
        ca, cg = chunk_cols(c)
        a = conv_slab(up_bufs[slot][0], ca, s)
        g = conv_slab(up_bufs[slot][1], cg, s)
        act_bufs[slot][s * FFN_SLAB:(s + 1) * FFN_SLAB, :] = (g * _sigmoid(g) * a).astype(BF16)

    def down_slab(c, slot, s):
        rows = slice(s * FFN_SLAB, (s + 1) * FFN_SLAB)
        acc_ref[rows, :] += jnp.dot(act_bufs[slot][rows, :], wdn_ref[pl.ds(chunk_cols(c)[0], TF), :],
                                    preferred_element_type=F32)

    n_chunks = D_FF // TF
    acc_ref[...] = jnp.zeros_like(acc_ref)
    for s in range(n_slabs):
        up_slab(0, 0, s)

    def body(i, carry):
        for slot in range(2):
            c = 2 * i + slot
            for s in range(n_slabs):
                up_slab(c + 1, 1 - slot, s)
                gate_slab(c, slot, s)
                down_slab(c, slot, s)
        return carry

    lax.fori_loop(0, (n_chunks - 1) // 2, body, 0)
    for s in range(n_slabs):
        gate_slab(n_chunks - 1, 0, s)
        down_slab(n_chunks - 1, 0, s)
    out_ref[...] = x + _mod_slice(mod_ref, 5) * (acc_ref[...] + bdn_ref[...])


def _ffn(x, norm_g, mod, layer, w_up, b_up, w_dw, b_dw, w_down, b_down):
    const = lambda g: (0, 0)
    layer_block = lambda g: (layer, 0, 0)
    resident = dict(pipeline_mode=pl.Buffered(1))
    return pl.pallas_call(
        _ffn_kernel,
        grid=(N_GROUPS,),
        in_specs=[
            pl.BlockSpec((TM, D_MODEL), lambda g: (g, 0)),
            pl.BlockSpec((1, D_MODEL), const),
            pl.BlockSpec((1, 1, 6 * D_MODEL), lambda g: (g, 0, 0)),
            pl.BlockSpec((None, D_MODEL, 2 * D_FF), layer_block, **resident),
            pl.BlockSpec((1, 2 * D_FF), const),
            pl.BlockSpec((3, 2 * D_FF), const),
            pl.BlockSpec((1, 2 * D_FF), const),
            pl.BlockSpec((None, D_FF, D_MODEL), layer_block, **resident),
            pl.BlockSpec((1, D_MODEL), const),
        ],
        out_specs=pl.BlockSpec((TM, D_MODEL), lambda g: (g, 0)),
        out_shape=jax.ShapeDtypeStruct((N_TOK, D_MODEL), F32),
        scratch_shapes=[pltpu.VMEM((TM, D_MODEL), BF16), pltpu.VMEM((TM, D_MODEL), F32)]
        + [pltpu.VMEM((TM // FFN_SLAB * (FFN_SLAB + 2 * FFN_HALO), TF), F32)] * 4
        + [pltpu.VMEM((TM, TF), BF16)] * 2,
        compiler_params=_cparams(("arbitrary",)),
        name="conv_ffn",
    )(x, norm_g, mod, w_up, b_up.reshape(1, -1), w_dw, b_dw.reshape(1, -1), w_down, b_down.reshape(1, -1))


def _pair_queries(q):
    lane = lax.broadcasted_iota(jnp.int32, q.shape, 1)
    zero = jnp.zeros_like(q)
    return jnp.concatenate([jnp.where(lane < NA_HEAD_DIM, q, zero),
                            jnp.where(lane < NA_HEAD_DIM, zero, q)], axis=0)


def _pair_merge(o2):
    n = o2.shape[0] // 2
    lane = lax.broadcasted_iota(jnp.int32, (n, o2.shape[1]), 1)
    return jnp.where(lane < NA_HEAD_DIM, o2[:n], o2[n:])


def _nt_dot(a, b):
    return lax.dot_general(a, b, (((1,), (1,)), ((), ())), preferred_element_type=F32)


def _ctx_attn_kernel(q_ref, k_ref, v_ref, o_ref):
    for b in range(TM // SEQ):
        rows = slice(b * SEQ, (b + 1) * SEQ)
        q2 = _pair_queries(q_ref[rows, :])
        s = _nt_dot(q2, k_ref[rows, :].astype(BF16))
        p = jnp.exp(s - jnp.max(s, axis=-1, keepdims=True))
        l = jnp.sum(p, axis=-1, keepdims=True)
        o2 = jnp.dot(p.astype(BF16), v_ref[rows, :].astype(BF16), preferred_element_type=F32) / l
        o_ref[rows, :] = _pair_merge(o2).astype(o_ref.dtype)


def _ctx_attention(q, k, v):
    spec = pl.BlockSpec((TM, 128), lambda g, hp: (g, hp))
    return pl.pallas_call(
        _ctx_attn_kernel,
        grid=(N_CTX_GROUPS, NA_HEADS // 2),
        in_specs=[spec, spec, spec],
        out_specs=spec,
        out_shape=jax.ShapeDtypeStruct((N_CTX_TOK, D_MODEL), BF16),
        compiler_params=_cparams(("arbitrary", "arbitrary")),
        name="ctx_attention",
    )(q, k, v)


_NA_BLOCKS = ((0, 8, 0), (0, 12, 1), (4, 12, 1), (8, 8, 2))


def _na_attn_kernel(q_ref, k_ref, v_ref, ck_ref, cv_ref, ba_ref, bm_ref, bc_ref, o_ref):
    bias_refs = (ba_ref, bm_ref, bc_ref)
    ck = ck_ref[0, 0].astype(BF16)
    cv = cv_ref[0, 0].astype(BF16)
    for blk, (row0, nrows, bidx) in enumerate(_NA_BLOCKS):
        rows = slice(blk * 4 * GRID_W, (blk + 1) * 4 * GRID_W)
        keys = slice(row0 * GRID_W, (row0 + nrows) * GRID_W)
        q2 = _pair_queries(q_ref[rows, :])
        bias = bias_refs[bidx][...]
        s_loc = _nt_dot(q2, k_ref[keys, :]) + bias.reshape(2 * 4 * GRID_W, nrows * GRID_W)
        s_ctx = _nt_dot(q2, ck)
        m = jnp.maximum(jnp.max(s_loc, axis=-1, keepdims=True), jnp.max(s_ctx, axis=-1, keepdims=True))
        p_loc = jnp.exp(s_loc - m)
        p_ctx = jnp.exp(s_ctx - m)
        l = jnp.sum(p_loc, axis=-1, keepdims=True) + jnp.sum(p_ctx, axis=-1, keepdims=True)
        o2 = (jnp.dot(p_loc.astype(BF16), v_ref[keys, :], preferred_element_type=F32)
              + jnp.dot(p_ctx.astype(BF16), cv, preferred_element_type=F32)) / l
        o_ref[rows, :] = _pair_merge(o2).astype(o_ref.dtype)


N_RPB_R = 2 * NA_WIN_R - 1
N_RPB_C = 2 * NA_WIN_C - 1


def _na_bias_kernel(rpb_ref, ba_ref, bm_ref, bc_ref):
    base = pl.program_id(0) * (N_RPB_R * N_RPB_C)
    qc = lax.broadcasted_iota(jnp.int32, (GRID_W, 2 * GRID_W), 0)
    lane = lax.broadcasted_iota(jnp.int32, (GRID_W, 2 * GRID_W), 1)
    kc = lane & (GRID_W - 1)
    right = lane >= GRID_W
    dcol = kc - qc + (NA_WIN_C - 1)
    c_start = jnp.clip(qc - NA_WIN_C // 2, 0, GRID_W - NA_WIN_C)
    in_win = (kc >= c_start) & (kc < c_start + NA_WIN_C)
    neg = jnp.full((GRID_W, 2 * GRID_W), NEG_INF, F32)
    tiles = {}

    def rpb_at(dr, j):
        return rpb_ref[base + dr * N_RPB_C + j] if 0 <= dr < N_RPB_R else jnp.float32(0.0)

    def pair_tile(dr):
        if dr not in tiles:
            acc = jnp.zeros((GRID_W, 2 * GRID_W), F32)
            for j in range(N_RPB_C):
                acc = jnp.where(dcol == j, jnp.where(right, rpb_at(dr + 1, j), rpb_at(dr, j)), acc)
            tiles[dr] = acc
        return tiles[dr]

    rows_total = DEC_SEQ // GRID_W
    for ref, blk in ((ba_ref, 0), (bm_ref, 1), (bc_ref, 3)):
        row0, nrows, _ = _NA_BLOCKS[blk]
        for rr in range(4):
            r = blk * 4 + rr
            r_start = min(max(r - NA_WIN_R // 2, 0), rows_total - NA_WIN_R)
            for ip in range(nrows // 2):
                krow = row0 + 2 * ip
                ok_l = r_start <= krow < r_start + NA_WIN_R
                ok_r = r_start <= krow + 1 < r_start + NA_WIN_R
                if ok_l or ok_r:
                    mask = in_win
                    if not ok_l:
                        mask = mask & right
                    if not ok_r:
                        mask = mask & jnp.logical_not(right)
                    tile = jnp.where(mask, pair_tile(krow - r + NA_WIN_R - 1), neg)
                else:
                    tile = neg
                ref[0, rr * GRID_W:(rr + 1) * GRID_W, ip * 2 * GRID_W:(ip + 1) * 2 * GRID_W] = tile


def _na_bias_tables(rpb):
    out = lambda n: pl.BlockSpec((1, 4 * GRID_W, n), lambda h: (h, 0, 0))
    shape = lambda n: jax.ShapeDtypeStruct((NA_HEADS, 4 * GRID_W, n), F32)
    return pl.pallas_call(
        _na_bias_kernel,
        grid=(NA_HEADS,),
        in_specs=[pl.BlockSpec(memory_space=pltpu.SMEM)],
        out_specs=[out(8 * GRID_W), out(12 * GRID_W), out(8 * GRID_W)],
        out_shape=[shape(8 * GRID_W), shape(12 * GRID_W), shape(8 * GRID_W)],
        compiler_params=_cparams(("arbitrary",)),
        name="na_bias",
    )(rpb.astype(F32).reshape(-1))


def _na_attention(q, k, v, cache_k, cache_v, layer_j, bias_tables):
    ba, bm, bc = bias_tables
    tok = pl.BlockSpec((TM, 128), lambda hp, b: (b, hp))
    cache = pl.BlockSpec((1, 1, PAST_LEN, 128), lambda hp, b: (b, layer_j, 0, hp))
    bias = lambda n: pl.BlockSpec((2, 4 * GRID_W, n), lambda hp, b: (hp, 0, 0))
    return pl.pallas_call(
        _na_attn_kernel,
        grid=(NA_HEADS // 2, DEC_BATCH),
        in_specs=[tok, tok, tok, cache, cache, bias(8 * GRID_W), bias(12 * GRID_W), bias(8 * GRID_W)],
        out_specs=tok,
        out_shape=jax.ShapeDtypeStruct((N_LAT_TOK, D_MODEL), BF16),
        compiler_params=_cparams(("arbitrary", "arbitrary")),
        name="na_attention",
    )(q, k, v, cache_k, cache_v, ba, bm, bc)


CONV_PAD = 16
CONV_ROWS = 128
CONV_LANES = 128


def _conv_tail_kernel(x_ref, u_ref, wdw_ref, bdw_ref, lng_ref, lnb_ref, w_ref, b_ref, mod_ref,
                      o_ref, pad_ref, a_ref):
    half = CONV_WIDTH // 2
    win = CONV_ROWS + 8
    zeros = jnp.zeros((CONV_PAD, D_MODEL), F32)

    def conv_sequence(base, seq_len):
        pad_ref[0:CONV_PAD, :] = zeros
        pad_ref[CONV_PAD + seq_len:2 * CONV_PAD + seq_len, :] = zeros
        pad_ref[CONV_PAD:CONV_PAD + seq_len, :] = u_ref[base:base + seq_len, :]

        def rows(i, carry):
            r0 = pl.multiple_of(i * CONV_ROWS, CONV_ROWS)
            strips = []
            for l0 in range(0, D_MODEL, CONV_LANES):
                lanes = slice(l0, l0 + CONV_LANES)
                acc = jnp.zeros((CONV_ROWS, CONV_LANES), F32) + bdw_ref[:, lanes]
                for b in range(8):
                    z = None
                    for a in range(-2, 2):
                        t = 8 * a + b + half
                        if not 0 <= t < CONV_WIDTH:
                            continue
                        term = wdw_ref[t:t + 1, lanes] * pad_ref[pl.ds(r0 + (CONV_PAD + 8 * a), win), lanes]
                        z = term if z is None else z + term
                    if b:
                        z = pltpu.roll(z, win - b, 0)
                    acc = acc + z[:CONV_ROWS]
                strips.append(acc)
            acc = jnp.concatenate(strips, axis=1)
            mu = jnp.mean(acc, axis=-1, keepdims=True)
            cen = acc - mu
            var = jnp.mean(cen * cen, axis=-1, keepdims=True)
            y = cen * lax.rsqrt(var + NORM_EPS) * lng_ref[...] + lnb_ref[...]
            a_ref[pl.ds(base + r0, CONV_ROWS), :] = (y * _sigmoid(y)).astype(BF16)
            return carry

        lax.fori_loop(0, seq_len // CONV_ROWS, rows, 0)

    @pl.when(pl.program_id(0) < N_CTX_GROUPS)
    def _():
        for s in range(TM // SEQ):
            conv_sequence(s * SEQ, SEQ)

    @pl.when(pl.program_id(0) >= N_CTX_GROUPS)
    def _():
        conv_sequence(0, DEC_SEQ)

    for rows in _row_slabs():
        r = jnp.dot(a_ref[rows, :], w_ref[...], preferred_element_type=F32) + b_ref[...]
        o_ref[rows, :] = x_ref[rows, :] + _mod_slice(mod_ref, 2) * r


def _conv_tail(x, u, w_dw, b_dw, ln_g, ln_b, w_pw2, b_pw2, mod):
    tok = pl.BlockSpec((TM, D_MODEL), lambda g: (g, 0))
    const = lambda g: (0, 0)
    vec = pl.BlockSpec((1, D_MODEL), const)
    return pl.pallas_call(
        _conv_tail_kernel,
        grid=(N_GROUPS,),
        in_specs=[tok, tok, pl.BlockSpec((CONV_WIDTH, D_MODEL), const), vec, vec, vec,
                  pl.BlockSpec((D_MODEL, D_MODEL), const), vec,
                  pl.BlockSpec((1, 1, 6 * D_MODEL), lambda g: (g, 0, 0))],
        out_specs=tok,
        out_shape=jax.ShapeDtypeStruct((N_TOK, D_MODEL), F32),
        scratch_shapes=[pltpu.VMEM((TM + 2 * CONV_PAD, D_MODEL), F32),
                        pltpu.VMEM((TM, D_MODEL), BF16)],
        compiler_params=_cparams(("arbitrary",)),
        name="conv_tail",
    )(x, u, w_dw, b_dw.reshape(1, -1), ln_g.reshape(1, -1), ln_b.reshape(1, -1), w_pw2,
      b_pw2.reshape(1, -1), mod)


def _log_sigmoid(z):
    return -(jnp.maximum(-z, 0.0) + jnp.log(1.0 + jnp.exp(-jnp.abs(z))))


def _gla_scan_kernel(*refs, seq_len, has_init, emit_state):
    it = iter(refs)
    q_ref, k_ref, v_ref, r_ref, w2_ref, bgk_ref = (next(it) for _ in range(6))
    s0_refs = (next(it), next(it)) if has_init else None
    o_ref = next(it)
    st_out = (next(it), next(it)) if emit_state else None
    g_ref, cum_ref, st_ref = (next(it) for _ in range(3))

    c = GLA_CHUNK
    n_chunks = seq_len // c
    rlow = r_ref[...].astype(BF16)
    row = lax.broadcasted_iota(jnp.int32, (c, 1), 0)
    sub = row & 7
    ri = lax.broadcasted_iota(jnp.int32, (c, c), 0)
    ci = lax.broadcasted_iota(jnp.int32, (c, c), 1)

    for d in range(2):
        reverse = d == 1
        z = jnp.dot(rlow, w2_ref[d].astype(BF16), preferred_element_type=F32) + bgk_ref[d]
        g_ref[...] = _log_sigmoid(z) * (1.0 / GLA_GATE_NORM)
        if has_init:
            st_ref[...] = s0_refs[d][0, 0].T
        else:
            st_ref[...] = jnp.zeros_like(st_ref)

        order = range(n_chunks - 1, -1, -1) if reverse else range(n_chunks)
        for ch in order:
            base = ch * c
            rows = slice(base, base + c)
            q = q_ref[rows, :]
            k = k_ref[rows, :]
            v = v_ref[rows, :].astype(BF16)
            cum = g_ref[rows, :]
            sh = 1
            while sh < c:
                if reverse:
                    cum = cum + jnp.where(row < c - sh, pltpu.roll(cum, c - sh, 0), 0.0)
                else:
                    cum = cum + jnp.where(row >= sh, pltpu.roll(cum, sh, 0), 0.0)
                sh *= 2
            cum_ref[...] = cum

            att = jnp.zeros((c, c), F32)
            for lag in range(8):
                if lag == 0:
                    col = jnp.sum(q * k, axis=-1, keepdims=True)
                else:
                    shift = c - lag if reverse else lag
                    ok = (sub <= 7 - lag) if reverse else (sub >= lag)
                    e = jnp.exp(jnp.where(ok, cum - pltpu.roll(cum, shift, 0), 0.0))
                    col = jnp.sum(jnp.where(ok, q * pltpu.roll(k, shift, 0) * e, 0.0), axis=-1, keepdims=True)
                att = jnp.where(ci == (ri + lag if reverse else ri - lag), col, att)

            m = 16
            while m <= c:
                half = m // 2
                pieces = []
                for blk in range(c // m):
                    arow = blk * m + (half if reverse else half - 1)
                    pieces.append(jnp.broadcast_to(cum_ref[arow:arow + 1, :], (m, GLA_DK)))
                anchor = pieces[0] if len(pieces) == 1 else jnp.concatenate(pieces, axis=0)
                in_first = (row & (m - 1)) < half
                later = in_first if reverse else jnp.logical_not(in_first)
                fac = jnp.exp(jnp.where(later, cum - anchor, anchor - cum))
                qs = jnp.where(later, q * fac, 0.0).astype(BF16)
                ks = jnp.where(later, 0.0, k * fac).astype(BF16)
                p = _nt_dot(qs, ks)
                if m < c:
                    p = jnp.where((ri & -m) == (ci & -m), p, 0.0)
                att = att + p
                m *= 2

            last = 0 if reverse else c - 1
            total = cum_ref[last:last + 1, :]
            q_in = (q * jnp.exp(cum)).astype(BF16)
            k_out = (k * jnp.exp(total - cum)).astype(BF16)
            st = st_ref[...]
            o = (jnp.dot(att.astype(BF16), v, preferred_element_type=F32)
                 + _nt_dot(q_in, st.astype(BF16)))
            if reverse:
                o_ref[rows, :] += o
            else:
                o_ref[rows, :] = o
            kv = lax.dot_general(v, k_out, (((0,), (0,)), ((), ())), preferred_element_type=F32)
            st_ref[...] = st * jnp.exp(total) + kv

        if emit_state:
            st_out[d][0, 0] = st_ref[...].T


def _gla_scan(y, w2pad, b_gk, init_states, seq_len, blk0, n_seq, emit_state):
    has_init = init_states is not None
    qk = lambda off: pl.BlockSpec((seq_len, GLA_DK), lambda b, h: (b + blk0, h + off))
    state = pl.BlockSpec((1, 1, GLA_DK, GLA_DV), lambda b, h: (b, h, 0, 0))
    in_specs = [
        qk(0), qk(GLA_KEY_DIM // GLA_DK),
        pl.BlockSpec((seq_len, GLA_DV), lambda b, h: (b + blk0, h + 2 * GLA_KEY_DIM // GLA_DV)),
        pl.BlockSpec((seq_len, 128), lambda b, h: (b + blk0, (2 * GLA_KEY_DIM + 2 * GLA_VAL_DIM) // 128)),
        pl.BlockSpec((2, 128, GLA_DK), lambda b, h: (0, 0, h)),
        pl.BlockSpec((2, 1, GLA_DK), lambda b, h: (0, 0, h)),
    ]
    args = [y, y, y, y, w2pad, b_gk]
    if has_init:
        in_specs += [state, state]
        args += list(init_states)
    out_specs = [pl.BlockSpec((seq_len, GLA_DV), lambda b, h: (b, h))]
    out_shape = [jax.ShapeDtypeStruct((n_seq * seq_len, GLA_VAL_DIM), F32)]
    if emit_state:
        out_specs += [state, state]
        out_shape += [jax.ShapeDtypeStruct((n_seq, GLA_HEADS, GLA_DK, GLA_DV), F32)] * 2
    return pl.pallas_call(
        functools.partial(_gla_scan_kernel, seq_len=seq_len, has_init=has_init, emit_state=emit_state),
        grid=(n_seq, GLA_HEADS),
        in_specs=in_specs,
        out_specs=out_specs,
        out_shape=out_shape,
        scratch_shapes=[pltpu.VMEM((seq_len, GLA_DK), F32), pltpu.VMEM((GLA_CHUNK, GLA_DK), F32),
                        pltpu.VMEM((GLA_DV, GLA_DK), F32)],
        compiler_params=_cparams(("arbitrary", "arbitrary")),
        name="gla_scan",
    )(*args)


_GROUP_COND_ROW = np.array([0] * N_CTX_GROUPS + list(range(1, DEC_BATCH + 1)))


def kernel(x_prompt, x_sample, c, cache_attn_k, cache_attn_v, state_gla_fwd, state_gla_bwd, c_ctx,
           mod_w, mod_b, norm1_g, norm2_g,
           attn_w_qkv, attn_w_o, attn_q_norm, attn_k_norm, attn_rpb,
           conv_w_pw1, conv_b_pw1, conv_w_dw, conv_b_dw, conv_ln_g, conv_ln_b, conv_w_pw2, conv_b_pw2,
           gla_w_q, gla_w_k, gla_w_v, gla_w_g, gla_w_gk1, gla_w_gk2, gla_b_gk, gla_o_norm, gla_w_o,
           ffn_w_up, ffn_b_up, ffn_w_dw, ffn_b_dw, ffn_w_down, ffn_b_down):
    x = jnp.concatenate([x_prompt.reshape(N_CTX_TOK, D_MODEL), x_sample.reshape(N_LAT_TOK, D_MODEL)], axis=0)
    cond = jnp.concatenate([c_ctx[None, :], c, jnp.zeros((N_COND - 1 - DEC_BATCH, D_MODEL), F32)], axis=0)
    mod_all = _modulation(cond, mod_w, mod_b)
    mod_all = mod_all[:, _GROUP_COND_ROW][:, :, None, :]

    cache_k = cache_attn_k.reshape(DEC_BATCH, -1, PAST_LEN, D_MODEL)
    cache_v = cache_attn_v.reshape(DEC_BATCH, -1, PAST_LEN, D_MODEL)
    ffn_w_up_bf16 = ffn_w_up.astype(BF16)
    ffn_w_down_bf16 = ffn_w_down.astype(BF16)
    new_k, new_v, new_sf, new_sb = [], [], [], []
    for i in range(DEPTH):
        kind, j = i % N_MIXERS, i // N_MIXERS
        mod = mod_all[i]
        n1 = norm1_g[i].reshape(1, D_MODEL)
        if kind == 0:
            w_qkv = attn_w_qkv[j].astype(BF16)
            gain = jnp.concatenate([jnp.tile(attn_q_norm[j], NA_HEADS) * (NA_HEAD_DIM ** -0.5),
                                    jnp.tile(attn_k_norm[j], NA_HEADS),
                                    jnp.ones((D_MODEL,), F32)]).reshape(1, 3 * D_MODEL)
            qp, kp, vp = _qkv_proj(x, n1, mod, w_qkv, gain, 0, N_CTX_GROUPS, F32)
            qs, ks, vs = _qkv_proj(x, n1, mod, w_qkv, gain, N_CTX_GROUPS, N_GROUPS - N_CTX_GROUPS, BF16)
            o_ctx = _ctx_attention(qp, kp, vp)
            o_lat = _na_attention(qs, ks, vs, cache_k, cache_v, j, _na_bias_tables(attn_rpb[j]))
            x = _out_proj(x, o_ctx, o_lat, attn_w_o[j].astype(BF16), mod)
            new_k.append(kp.reshape(BATCH, SEQ, NA_HEADS, NA_HEAD_DIM))
            new_v.append(vp.reshape(BATCH, SEQ, NA_HEADS, NA_HEAD_DIM))
        elif kind == 1:
            u = _glu_proj(x, n1, mod, conv_w_pw1[j].astype(BF16), conv_b_pw1[j])
            x = _conv_tail(x, u, conv_w_dw[j], conv_b_dw[j], conv_ln_g[j], conv_ln_b[j],
                           conv_w_pw2[j].astype(BF16), conv_b_pw2[j], mod)
        else:
            pad = GLA_PROJ_N - 2 * GLA_KEY_DIM - 2 * GLA_VAL_DIM - 2 * GLA_GATE_RANK
            w_cat = jnp.concatenate([gla_w_q[j], gla_w_k[j], gla_w_v[j], gla_w_g[j], gla_w_gk1[j, 0],
                                     gla_w_gk1[j, 1], jnp.zeros((D_MODEL, pad), F32)], axis=1).astype(BF16)
            colscale = jnp.concatenate([jnp.full((GLA_KEY_DIM,), GLA_DK ** -0.5, F32),
                                        jnp.ones((GLA_PROJ_N - GLA_KEY_DIM,), F32)]).reshape(1, GLA_PROJ_N)
            y = _scaled_proj(x, n1, mod, w_cat, colscale)
            w2pad = jnp.zeros((2, 128, GLA_KEY_DIM), F32)
            w2pad = w2pad.at[0, :GLA_GATE_RANK].set(gla_w_gk2[j, 0])
            w2pad = w2pad.at[1, GLA_GATE_RANK:2 * GLA_GATE_RANK].set(gla_w_gk2[j, 1])
            b_gk = gla_b_gk[j].reshape(2, 1, GLA_KEY_DIM)
            o_ctx, sf, sb = _gla_scan(y, w2pad, b_gk, None, SEQ, 0, BATCH, True)
            (o_lat,) = _gla_scan(y, w2pad, b_gk, (state_gla_fwd[:, j], state_gla_bwd[:, j]),
                                 DEC_SEQ, N_CTX_GROUPS, DEC_BATCH, False)
            x = _gla_out_proj(x, o_ctx, o_lat, y, gla_o_norm[j].reshape(1, GLA_DV),
                              gla_w_o[j].astype(BF16), mod)
            new_sf.append(sf)
            new_sb.append(sb)
        x = _ffn(x, norm2_g[i].reshape(1, D_MODEL), mod, i, ffn_w_up_bf16, ffn_b_up[i],
                 ffn_w_dw[i], ffn_b_dw[i], ffn_w_down_bf16, ffn_b_down[i])

    y_prompt = x[:N_CTX_TOK].reshape(BATCH, SEQ, D_MODEL)
    y_sample = x[N_CTX_TOK:].reshape(DEC_BATCH, DEC_SEQ, D_MODEL)
    return (y_prompt, y_sample, jnp.stack(new_k, axis=1), jnp.stack(new_v, axis=1),
            jnp.stack(new_sf, axis=1), jnp.stack(new_sb, axis=1))
```

```python
import functools

import numpy as np
import jax
import jax.numpy as jnp
from jax import lax
from jax.experimental import pallas as pl
from jax.experimental.pallas import tpu as pltpu

F32 = jnp.float32
BF16 = jnp.bfloat16

D_MODEL = 1024
BATCH = 16
SEQ = 256
DEPTH = 4
DEC_BATCH = 4
DEC_SEQ = 1024
PAST_LEN = 512
GRID_W = 64
N_MIXERS = 3
NA_HEADS = 16
NA_HEAD_DIM = 64
NA_WIN_R = 8
NA_WIN_C = 16
CONV_WIDTH = 31
GLA_HEADS = 4
GLA_KEY_DIM = 512
GLA_VAL_DIM = 1024
GLA_DK = 128
GLA_DV = 256
GLA_GATE_RANK = 16
GLA_GATE_NORM = 16.0
D_FF = 2816
NORM_EPS = 1e-6
NEG_INF = -1e30

TM = 1024
N_CTX_TOK = BATCH * SEQ
N_LAT_TOK = DEC_BATCH * DEC_SEQ
N_TOK = N_CTX_TOK + N_LAT_TOK
N_CTX_GROUPS = N_CTX_TOK // TM
N_GROUPS = N_TOK // TM
N_COND = 8
TF = 256
PROJ_SLAB = 256
FFN_SLAB = SEQ
FFN_HALO = 8
GLA_CHUNK = 256
GLA_PROJ_N = 3328
VMEM_LIMIT = 60 * 1024 * 1024


def _cparams(sem, flags=None):
    return pltpu.CompilerParams(dimension_semantics=sem, vmem_limit_bytes=VMEM_LIMIT, flags=flags)


def _sigmoid(x):
    return 1.0 / (1.0 + jnp.exp(-x))


def _norm_mod(x, g, shift, scale):
    ms = jnp.mean(x * x, axis=-1, keepdims=True)
    y = x * lax.rsqrt(ms + NORM_EPS) * g
    return y * (1.0 + scale) + shift


def _mod_slice(mod_ref, idx):
    return mod_ref[0, :, idx * D_MODEL:(idx + 1) * D_MODEL]


def _mod_kernel(cond_ref, w_ref, b_ref, o_ref):
    c = cond_ref[...]
    s = (c * _sigmoid(c)).astype(BF16)
    o_ref[0] = jnp.dot(s, w_ref[0].astype(BF16), preferred_element_type=F32) + b_ref[0]


def _modulation(cond, mod_w, mod_b):
    tn = 1536
    n = 6 * D_MODEL
    return pl.pallas_call(
        _mod_kernel,
        grid=(DEPTH, n // tn),
        in_specs=[
            pl.BlockSpec((N_COND, D_MODEL), lambda l, j: (0, 0)),
            pl.BlockSpec((1, D_MODEL, tn), lambda l, j: (l, 0, j)),
            pl.BlockSpec((1, 1, tn), lambda l, j: (l, 0, j)),
        ],
        out_specs=pl.BlockSpec((1, N_COND, tn), lambda l, j: (l, 0, j)),
        out_shape=jax.ShapeDtypeStruct((DEPTH, N_COND, n), F32),
        compiler_params=_cparams(("arbitrary", "arbitrary")),
        name="modulation",
    )(cond, mod_w, mod_b.reshape(DEPTH, 1, n))


def _proj_slabs(x_ref, g_ref, mod_ref, h_ref, epilogue):
    def run(first):
        for s in range(TM // PROJ_SLAB):
            rows = slice(s * PROJ_SLAB, (s + 1) * PROJ_SLAB)
            if first:
                h = _norm_mod(x_ref[rows, :], g_ref[...], _mod_slice(mod_ref, 0), _mod_slice(mod_ref, 1))
                h_ref[rows, :] = h.astype(BF16)
            epilogue(rows, h_ref[rows, :])

    pl.when(pl.program_id(1) == 0)(lambda: run(True))
    pl.when(pl.program_id(1) != 0)(lambda: run(False))


def _head_rms(acc, gain, hsum_ref, hexp_ref):
    ms = jnp.dot((acc * acc).astype(BF16), hsum_ref[...], preferred_element_type=F32)
    inv = lax.rsqrt(ms + NORM_EPS)
    hi = inv.astype(BF16)
    lo = (inv - hi.astype(F32)).astype(BF16)
    inv_full = jnp.dot(jnp.concatenate([hi, lo], axis=1), hexp_ref[...], preferred_element_type=F32)
    return acc * inv_full * gain


def _qkv_kernel(x_ref, g_ref, mod_ref, w_ref, gain_ref, hsum_ref, hexp_ref,
                q_ref, k_ref, v_ref, h_ref):
    j = pl.program_id(1)

    def column_step(out_ref, normed, first):
        for s in range(TM // PROJ_SLAB):
            rows = slice(s * PROJ_SLAB, (s + 1) * PROJ_SLAB)
            if first:
                h = _norm_mod(x_ref[rows, :], g_ref[...], _mod_slice(mod_ref, 0), _mod_slice(mod_ref, 1))
                h_ref[rows, :] = h.astype(BF16)
            acc = jnp.dot(h_ref[rows, :], w_ref[...], preferred_element_type=F32)
            if normed:
                acc = _head_rms(acc, gain_ref[...], hsum_ref, hexp_ref)
            out_ref[rows, :] = acc.astype(out_ref.dtype)

    pl.when(j == 0)(lambda: column_step(q_ref, True, True))
    pl.when(j == 1)(lambda: column_step(k_ref, True, False))
    pl.when(j == 2)(lambda: column_step(v_ref, False, False))


def _head_matrices():
    lane = np.arange(D_MODEL)
    hsum = np.zeros((D_MODEL, 128), np.float32)
    hsum[lane, lane // NA_HEAD_DIM] = 1.0 / NA_HEAD_DIM
    hexp = np.zeros((128, D_MODEL), np.float32)
    hexp[lane // NA_HEAD_DIM, lane] = 1.0
    return jnp.asarray(hsum, BF16), jnp.asarray(np.concatenate([hexp, hexp], 0), BF16)


def _qkv_proj(x, norm_g, mod, w_qkv, gain, group0, n_groups, kv_dtype):
    hsum, hexp = _head_matrices()
    ntok = n_groups * TM
    row = lambda g, j: (g, 0)
    return pl.pallas_call(
        _qkv_kernel,
        grid=(n_groups, 3),
        in_specs=[
            pl.BlockSpec((TM, D_MODEL), lambda g, j: (g + group0, 0)),
            pl.BlockSpec((1, D_MODEL), lambda g, j: (0, 0)),
            pl.BlockSpec((1, 1, 6 * D_MODEL), lambda g, j: (g + group0, 0, 0)),
            pl.BlockSpec((D_MODEL, D_MODEL), lambda g, j: (0, j)),
            pl.BlockSpec((1, D_MODEL), lambda g, j: (0, j)),
            pl.BlockSpec((D_MODEL, 128), lambda g, j: (0, 0)),
            pl.BlockSpec((256, D_MODEL), lambda g, j: (0, 0)),
        ],
        out_specs=[pl.BlockSpec((TM, D_MODEL), row)] * 3,
        out_shape=[jax.ShapeDtypeStruct((ntok, D_MODEL), BF16),
                   jax.ShapeDtypeStruct((ntok, D_MODEL), kv_dtype),
                   jax.ShapeDtypeStruct((ntok, D_MODEL), kv_dtype)],
        scratch_shapes=[pltpu.VMEM((TM, D_MODEL), BF16)],
        compiler_params=_cparams(("arbitrary", "arbitrary")),
        name="qkv_proj",
    )(x, norm_g, mod, w_qkv, gain, hsum, hexp)


def _glu_kernel(x_ref, g_ref, mod_ref, wa_ref, wg_ref, ba_ref, bg_ref, u_ref, h_ref):
    def epilogue(rows, h):
        a = jnp.dot(h, wa_ref[...], preferred_element_type=F32) + ba_ref[...]
        g = jnp.dot(h, wg_ref[...], preferred_element_type=F32) + bg_ref[...]
        u_ref[rows, :] = a * _sigmoid(g)

    _proj_slabs(x_ref, g_ref, mod_ref, h_ref, epilogue)


def _glu_proj(x, norm_g, mod, w_pw1, b_pw1):
    tn = 512
    nj = D_MODEL // tn
    b = b_pw1.reshape(1, 2 * D_MODEL)
    return pl.pallas_call(
        _glu_kernel,
        grid=(N_GROUPS, nj),
        in_specs=[
            pl.BlockSpec((TM, D_MODEL), lambda g, j: (g, 0)),
            pl.BlockSpec((1, D_MODEL), lambda g, j: (0, 0)),
            pl.BlockSpec((1, 1, 6 * D_MODEL), lambda g, j: (g, 0, 0)),
            pl.BlockSpec((D_MODEL, tn), lambda g, j: (0, j)),
            pl.BlockSpec((D_MODEL, tn), lambda g, j: (0, j + nj)),
            pl.BlockSpec((1, tn), lambda g, j: (0, j)),
            pl.BlockSpec((1, tn), lambda g, j: (0, j + nj)),
        ],
        out_specs=pl.BlockSpec((TM, tn), lambda g, j: (g, j)),
        out_shape=jax.ShapeDtypeStruct((N_TOK, D_MODEL), F32),
        scratch_shapes=[pltpu.VMEM((TM, D_MODEL), BF16)],
        compiler_params=_cparams(("arbitrary", "arbitrary")),
        name="glu_proj",
    )(x, norm_g, mod, w_pw1, w_pw1, b, b)


def _scaled_proj_kernel(x_ref, g_ref, mod_ref, w_ref, cs_ref, y_ref, h_ref):
    def epilogue(rows, h):
        y_ref[rows, :] = jnp.dot(h, w_ref[...], preferred_element_type=F32) * cs_ref[...]

    _proj_slabs(x_ref, g_ref, mod_ref, h_ref, epilogue)


def _scaled_proj(x, norm_g, mod, w, colscale):
    n = w.shape[1]
    tn = n // 2
    return pl.pallas_call(
        _scaled_proj_kernel,
        grid=(N_GROUPS, n // tn),
        in_specs=[
            pl.BlockSpec((TM, D_MODEL), lambda g, j: (g, 0)),
            pl.BlockSpec((1, D_MODEL), lambda g, j: (0, 0)),
            pl.BlockSpec((1, 1, 6 * D_MODEL), lambda g, j: (g, 0, 0)),
            pl.BlockSpec((D_MODEL, tn), lambda g, j: (0, j)),
            pl.BlockSpec((1, tn), lambda g, j: (0, j)),
        ],
        out_specs=pl.BlockSpec((TM, tn), lambda g, j: (g, j)),
        out_shape=jax.ShapeDtypeStruct((N_TOK, n), F32),
        scratch_shapes=[pltpu.VMEM((TM, D_MODEL), BF16)],
        compiler_params=_cparams(("arbitrary", "arbitrary")),
        name="gla_proj",
    )(x, norm_g, mod, w, colscale)


_CTX_PART = pl.BlockSpec((TM, D_MODEL), lambda g: (jnp.minimum(g, N_CTX_GROUPS - 1), 0))
_LAT_PART = pl.BlockSpec((TM, D_MODEL), lambda g: (jnp.maximum(g - N_CTX_GROUPS, 0), 0))


def _group_part(ctx_ref, lat_ref, rows):
    return jnp.where(pl.program_id(0) < N_CTX_GROUPS, ctx_ref[rows, :], lat_ref[rows, :])


def _row_slabs():
    return [slice(s * PROJ_SLAB, (s + 1) * PROJ_SLAB) for s in range(TM // PROJ_SLAB)]


def _oproj_kernel(x_ref, a_ctx_ref, a_lat_ref, w_ref, mod_ref, o_ref):
    for rows in _row_slabs():
        r = jnp.dot(_group_part(a_ctx_ref, a_lat_ref, rows), w_ref[...], preferred_element_type=F32)
        o_ref[rows, :] = x_ref[rows, :] + _mod_slice(mod_ref, 2) * r


def _out_proj(x, a_ctx, a_lat, w, mod):
    return pl.pallas_call(
        _oproj_kernel,
        grid=(N_GROUPS,),
        in_specs=[
            pl.BlockSpec((TM, D_MODEL), lambda g: (g, 0)),
            _CTX_PART, _LAT_PART,
            pl.BlockSpec((D_MODEL, D_MODEL), lambda g: (0, 0)),
            pl.BlockSpec((1, 1, 6 * D_MODEL), lambda g: (g, 0, 0)),
        ],
        out_specs=pl.BlockSpec((TM, D_MODEL), lambda g: (g, 0)),
        out_shape=jax.ShapeDtypeStruct((N_TOK, D_MODEL), F32),
        compiler_params=_cparams(("arbitrary",)),
        name="out_proj",
    )(x, a_ctx, a_lat, w, mod)


def _gla_oproj_kernel(x_ref, o_ctx_ref, o_lat_ref, gz_ref, og_ref, w_ref, mod_ref, out_ref):
    og = og_ref[...]
    for rows in _row_slabs():
        o = _group_part(o_ctx_ref, o_lat_ref, rows)
        parts = []
        for h in range(GLA_HEADS):
            oh = o[:, h * GLA_DV:(h + 1) * GLA_DV]
            ms = jnp.mean(oh * oh, axis=-1, keepdims=True)
            parts.append(oh * lax.rsqrt(ms + NORM_EPS) * og)
        y = jnp.concatenate(parts, axis=1)
        gz = gz_ref[rows, :]
        a = (y * (gz * _sigmoid(gz))).astype(BF16)
        r = jnp.dot(a, w_ref[...], preferred_element_type=F32)
        out_ref[rows, :] = x_ref[rows, :] + _mod_slice(mod_ref, 2) * r


def _gla_out_proj(x, o_ctx, o_lat, y, o_norm, w, mod):
    return pl.pallas_call(
        _gla_oproj_kernel,
        grid=(N_GROUPS,),
        in_specs=[
            pl.BlockSpec((TM, D_MODEL), lambda g: (g, 0)),
            _CTX_PART, _LAT_PART,
            pl.BlockSpec((TM, GLA_VAL_DIM), lambda g: (g, 2)),
            pl.BlockSpec((1, GLA_DV), lambda g: (0, 0)),
            pl.BlockSpec((GLA_VAL_DIM, D_MODEL), lambda g: (0, 0)),
            pl.BlockSpec((1, 1, 6 * D_MODEL), lambda g: (g, 0, 0)),
        ],
        out_specs=pl.BlockSpec((TM, D_MODEL), lambda g: (g, 0)),
        out_shape=jax.ShapeDtypeStruct((N_TOK, D_MODEL), F32),
        compiler_params=_cparams(("arbitrary",)),
        name="gla_out_proj",
    )(x, o_ctx, o_lat, y, o_norm, w, mod)


def _ffn_kernel(x_ref, g_ref, mod_ref, wup_ref, bup_ref, wdw_ref, bdw_ref, wdn_ref, bdn_ref,
                out_ref, h_ref, acc_ref, ua0_ref, ug0_ref, ua1_ref, ug1_ref, act0_ref, act1_ref):
    is_lat = pl.program_id(0) >= N_CTX_GROUPS
    up_bufs = ((ua0_ref, ug0_ref), (ua1_ref, ug1_ref))
    act_bufs = (act0_ref, act1_ref)
    n_slabs = TM // FFN_SLAB
    win = FFN_SLAB + 2 * FFN_HALO

    def chunk_cols(c):
        return pl.multiple_of(c * TF, TF), pl.multiple_of(D_FF + c * TF, TF)

    def up_slab(c, slot, s):
        h = h_ref[s * FFN_SLAB:(s + 1) * FFN_SLAB, :]
        for buf, col in zip(up_bufs[slot], chunk_cols(c)):
            res = jnp.dot(h, wup_ref[:, pl.ds(col, TF)], preferred_element_type=F32)
            pad = jnp.broadcast_to(-bup_ref[:, pl.ds(col, TF)], (FFN_HALO, TF))
            base = s * win
            buf[base + FFN_HALO:base + FFN_HALO + FFN_SLAB, :] = res
            if s == 0:
                buf[0:FFN_HALO, :] = pad
            else:
                buf[base - FFN_HALO:base, :] = jnp.where(is_lat, res[:FFN_HALO], pad)
            if s == n_slabs - 1:
                buf[base + win - FFN_HALO:base + win, :] = pad
            else:
                buf[base + win:base + win + FFN_HALO, :] = jnp.where(is_lat, res[FFN_SLAB - FFN_HALO:], pad)

    def conv_slab(buf, col, s):
        r0 = s * win + FFN_HALO
        prev = buf[r0 - 1:r0 - 1 + FFN_SLAB, :]
        mid = buf[r0:r0 + FFN_SLAB, :]
        nxt = buf[r0 + 1:r0 + 1 + FFN_SLAB, :]
        w = wdw_ref[:, pl.ds(col, TF)]
        bias = bdw_ref[:, pl.ds(col, TF)] + bup_ref[:, pl.ds(col, TF)] * (w[0:1] + w[1:2] + w[2:3])
        return w[0:1] * prev + w[1:2] * mid + w[2:3] * nxt + bias

    def gate_slab(c, slot, s):
        ca, cg = chunk_cols(c)
        a = conv_slab(up_bufs[slot][0], ca, s)
        g = conv_slab(up_bufs[slot][1], cg, s)
        act_bufs[slot][s * FFN_SLAB:(s + 1) * FFN_SLAB, :] = (g * _sigmoid(g) * a).astype(BF16)

    def down_slab(c, slot, s):
        rows = slice(s * FFN_SLAB, (s + 1) * FFN_SLAB)
        acc_ref[rows, :] += jnp.dot(act_bufs[slot][rows, :], wdn_ref[pl.ds(chunk_cols(c)[0], TF), :],
                                    preferred_element_type=F32)

    n_chunks = D_FF // TF
    acc_ref[...] = jnp.zeros_like(acc_ref)
    for s in range(n_slabs):
        rows = slice(s * FFN_SLAB, (s + 1) * FFN_SLAB)
        h = _norm_mod(x_ref[rows, :], g_ref[...], _mod_slice(mod_ref, 3), _mod_slice(mod_ref, 4))
        h_ref[rows, :] = h.astype(BF16)
        up_slab(0, 0, s)

    def body(i, carry):
        for slot in range(2):
            c = 2 * i + slot
            for s in range(n_slabs):
                up_slab(c + 1, 1 - slot, s)
                gate_slab(c, slot, s)
                down_slab(c, slot, s)
        return carry

    lax.fori_loop(0, (n_chunks - 1) // 2, body, 0)
    for s in range(n_slabs):
        rows = slice(s * FFN_SLAB, (s + 1) * FFN_SLAB)
        gate_slab(n_chunks - 1, 0, s)
        down_slab(n_chunks - 1, 0, s)
        out_ref[rows, :] = x_ref[rows, :] + _mod_slice(mod_ref, 5) * (acc_ref[rows, :] + bdn_ref[...])


def _ffn(x, norm_g, mod, layer, w_up, b_up, w_dw, b_dw, w_down, b_down):
    const = lambda g: (0, 0)
    layer_block = lambda g: (layer, 0, 0)
    resident = dict(pipeline_mode=pl.Buffered(1))
    return pl.pallas_call(
        _ffn_kernel,
        grid=(N_GROUPS,),
        in_specs=[
            pl.BlockSpec((TM, D_MODEL), lambda g: (g, 0)),
            pl.BlockSpec((1, D_MODEL), const),
            pl.BlockSpec((1, 1, 6 * D_MODEL), lambda g: (g, 0, 0)),
            pl.BlockSpec((None, D_MODEL, 2 * D_FF), layer_block, **resident),
            pl.BlockSpec((1, 2 * D_FF), const),
            pl.BlockSpec((3, 2 * D_FF), const),
            pl.BlockSpec((1, 2 * D_FF), const),
            pl.BlockSpec((None, D_FF, D_MODEL), layer_block, **resident),
            pl.BlockSpec((1, D_MODEL), const),
        ],
        out_specs=pl.BlockSpec((TM, D_MODEL), lambda g: (g, 0)),
        out_shape=jax.ShapeDtypeStruct((N_TOK, D_MODEL), F32),
        scratch_shapes=[pltpu.VMEM((TM, D_MODEL), BF16), pltpu.VMEM((TM, D_MODEL), F32)]
        + [pltpu.VMEM((TM // FFN_SLAB * (FFN_SLAB + 2 * FFN_HALO), TF), F32)] * 4
        + [pltpu.VMEM((TM, TF), BF16)] * 2,
        compiler_params=_cparams(("arbitrary",)),
        name="conv_ffn",
    )(x, norm_g, mod, w_up, b_up.reshape(1, -1), w_dw, b_dw.reshape(1, -1), w_down, b_down.reshape(1, -1))


def _pair_queries(q):
    lane = lax.broadcasted_iota(jnp.int32, q.shape, 1)
    zero = jnp.zeros_like(q)
    return jnp.concatenate([jnp.where(lane < NA_HEAD_DIM, q, zero),
                            jnp.where(lane < NA_HEAD_DIM, zero, q)], axis=0)


def _pair_merge(o2):
    n = o2.shape[0] // 2
    lane = lax.broadcasted_iota(jnp.int32, (n, o2.shape[1]), 1)
    return jnp.where(lane < NA_HEAD_DIM, o2[:n], o2[n:])


def _nt_dot(a, b):
    return lax.dot_general(a, b, (((1,), (1,)), ((), ())), preferred_element_type=F32)


def _ctx_attn_kernel(q_ref, k_ref, v_ref, o_ref):
    for b in range(TM // SEQ):
        rows = slice(b * SEQ, (b + 1) * SEQ)
        q2 = _pair_queries(q_ref[rows, :])
        s = _nt_dot(q2, k_ref[rows, :].astype(BF16))
        p = jnp.exp(s - jnp.max(s, axis=-1, keepdims=True))
        l = jnp.sum(p, axis=-1, keepdims=True)
        o2 = jnp.dot(p.astype(BF16), v_ref[rows, :].astype(BF16), preferred_element_type=F32) / l
        o_ref[rows, :] = _pair_merge(o2).astype(o_ref.dtype)


def _ctx_attention(q, k, v):
    spec = pl.BlockSpec((TM, 128), lambda g, hp: (g, hp))
    return pl.pallas_call(
        _ctx_attn_kernel,
        grid=(N_CTX_GROUPS, NA_HEADS // 2),
        in_specs=[spec, spec, spec],
        out_specs=spec,
        out_shape=jax.ShapeDtypeStruct((N_CTX_TOK, D_MODEL), BF16),
        compiler_params=_cparams(("arbitrary", "arbitrary")),
        name="ctx_attention",
    )(q, k, v)


_NA_BLOCKS = ((0, 8, 0), (0, 12, 1), (4, 12, 1), (8, 8, 2))


def _na_attn_kernel(q_ref, k_ref, v_ref, ck_ref, cv_ref, ba_ref, bm_ref, bc_ref, o_ref):
    bias_refs = (ba_ref, bm_ref, bc_ref)
    ck = ck_ref[0, 0].astype(BF16)
    cv = cv_ref[0, 0].astype(BF16)
    for blk, (row0, nrows, bidx) in enumerate(_NA_BLOCKS):
        rows = slice(blk * 4 * GRID_W, (blk + 1) * 4 * GRID_W)
        keys = slice(row0 * GRID_W, (row0 + nrows) * GRID_W)
        q2 = _pair_queries(q_ref[rows, :])
        bias = bias_refs[bidx][...]
        s_loc = _nt_dot(q2, k_ref[keys, :]) + bias.reshape(2 * 4 * GRID_W, nrows * GRID_W)
        s_ctx = _nt_dot(q2, ck)
        m = jnp.maximum(jnp.max(s_loc, axis=-1, keepdims=True), jnp.max(s_ctx, axis=-1, keepdims=True))
        p_loc = jnp.exp(s_loc - m)
        p_ctx = jnp.exp(s_ctx - m)
        l = jnp.sum(p_loc, axis=-1, keepdims=True) + jnp.sum(p_ctx, axis=-1, keepdims=True)
        o2 = (jnp.dot(p_loc.astype(BF16), v_ref[keys, :], preferred_element_type=F32)
              + jnp.dot(p_ctx.astype(BF16), cv, preferred_element_type=F32)) / l
        o_ref[rows, :] = _pair_merge(o2).astype(o_ref.dtype)


N_RPB_R = 2 * NA_WIN_R - 1
N_RPB_C = 2 * NA_WIN_C - 1


def _na_bias_kernel(rpb_ref, ba_ref, bm_ref, bc_ref):
    base = pl.program_id(0) * (N_RPB_R * N_RPB_C)
    qc = lax.broadcasted_iota(jnp.int32, (GRID_W, 2 * GRID_W), 0)
    lane = lax.broadcasted_iota(jnp.int32, (GRID_W, 2 * GRID_W), 1)
    kc = lane & (GRID_W - 1)
    right = lane >= GRID_W
    dcol = kc - qc + (NA_WIN_C - 1)
    c_start = jnp.clip(qc - NA_WIN_C // 2, 0, GRID_W - NA_WIN_C)
    in_win = (kc >= c_start) & (kc < c_start + NA_WIN_C)
    neg = jnp.full((GRID_W, 2 * GRID_W), NEG_INF, F32)
    tiles = {}

    def rpb_at(dr, j):
        return rpb_ref[base + dr * N_RPB_C + j] if 0 <= dr < N_RPB_R else jnp.float32(0.0)

    def pair_tile(dr):
        if dr not in tiles:
            acc = jnp.zeros((GRID_W, 2 * GRID_W), F32)
            for j in range(N_RPB_C):
                acc = jnp.where(dcol == j, jnp.where(right, rpb_at(dr + 1, j), rpb_at(dr, j)), acc)
            tiles[dr] = acc
        return tiles[dr]

    rows_total = DEC_SEQ // GRID_W
    for ref, blk in ((ba_ref, 0), (bm_ref, 1), (bc_ref, 3)):
        row0, nrows, _ = _NA_BLOCKS[blk]
        for rr in range(4):
            r = blk * 4 + rr
            r_start = min(max(r - NA_WIN_R // 2, 0), rows_total - NA_WIN_R)
            for ip in range(nrows // 2):
                krow = row0 + 2 * ip
                ok_l = r_start <= krow < r_start + NA_WIN_R
                ok_r = r_start <= krow + 1 < r_start + NA_WIN_R
                if ok_l or ok_r:
                    mask = in_win
                    if not ok_l:
                        mask = mask & right
                    if not ok_r:
                        mask = mask & jnp.logical_not(right)
                    tile = jnp.where(mask, pair_tile(krow - r + NA_WIN_R - 1), neg)
                else:
                    tile = neg
                ref[0, rr * GRID_W:(rr + 1) * GRID_W, ip * 2 * GRID_W:(ip + 1) * 2 * GRID_W] = tile


def _na_bias_tables(rpb):
    out = lambda n: pl.BlockSpec((1, 4 * GRID_W, n), lambda h: (h, 0, 0))
    shape = lambda n: jax.ShapeDtypeStruct((NA_HEADS, 4 * GRID_W, n), F32)
    return pl.pallas_call(
        _na_bias_kernel,
        grid=(NA_HEADS,),
        in_specs=[pl.BlockSpec(memory_space=pltpu.SMEM)],
        out_specs=[out(8 * GRID_W), out(12 * GRID_W), out(8 * GRID_W)],
        out_shape=[shape(8 * GRID_W), shape(12 * GRID_W), shape(8 * GRID_W)],
        compiler_params=_cparams(("arbitrary",)),
        name="na_bias",
    )(rpb.astype(F32).reshape(-1))


def _na_attention(q, k, v, cache_k, cache_v, layer_j, bias_tables):
    ba, bm, bc = bias_tables
    tok = pl.BlockSpec((TM, 128), lambda hp, b: (b, hp))
    cache = pl.BlockSpec((1, 1, PAST_LEN, 128), lambda hp, b: (b, layer_j, 0, hp))
    bias = lambda n: pl.BlockSpec((2, 4 * GRID_W, n), lambda hp, b: (hp, 0, 0))
    return pl.pallas_call(
        _na_attn_kernel,
        grid=(NA_HEADS // 2, DEC_BATCH),
        in_specs=[tok, tok, tok, cache, cache, bias(8 * GRID_W), bias(12 * GRID_W), bias(8 * GRID_W)],
        out_specs=tok,
        out_shape=jax.ShapeDtypeStruct((N_LAT_TOK, D_MODEL), BF16),
        compiler_params=_cparams(("arbitrary", "arbitrary")),
        name="na_attention",
    )(q, k, v, cache_k, cache_v, ba, bm, bc)


CONV_PAD = 16
CONV_ROWS = 128
CONV_LANES = 128


def _conv_tail_kernel(x_ref, u_ref, wdw_ref, bdw_ref, lng_ref, lnb_ref, w_ref, b_ref, mod_ref,
                      o_ref, pad_ref, a_ref):
    half = CONV_WIDTH // 2
    win = CONV_ROWS + 8
    zeros = jnp.zeros((CONV_PAD, D_MODEL), F32)

    def conv_sequence(base, seq_len):
        pad_ref[0:CONV_PAD, :] = zeros
        pad_ref[CONV_PAD + seq_len:2 * CONV_PAD + seq_len, :] = zeros
        pad_ref[CONV_PAD:CONV_PAD + seq_len, :] = u_ref[base:base + seq_len, :]

        def rows(i, carry):
            r0 = pl.multiple_of(i * CONV_ROWS, CONV_ROWS)
            strips = []
            for l0 in range(0, D_MODEL, CONV_LANES):
                lanes = slice(l0, l0 + CONV_LANES)
                acc = jnp.zeros((CONV_ROWS, CONV_LANES), F32) + bdw_ref[:, lanes]
                for b in range(8):
                    z = None
                    for a in range(-2, 2):
                        t = 8 * a + b + half
                        if not 0 <= t < CONV_WIDTH:
                            continue
                        term = wdw_ref[t:t + 1, lanes] * pad_ref[pl.ds(r0 + (CONV_PAD + 8 * a), win), lanes]
                        z = term if z is None else z + term
                    if b:
                        z = pltpu.roll(z, win - b, 0)
                    acc = acc + z[:CONV_ROWS]
                strips.append(acc)
            acc = jnp.concatenate(strips, axis=1)
            mu = jnp.mean(acc, axis=-1, keepdims=True)
            cen = acc - mu
            var = jnp.mean(cen * cen, axis=-1, keepdims=True)
            y = cen * lax.rsqrt(var + NORM_EPS) * lng_ref[...] + lnb_ref[...]
            a_ref[pl.ds(base + r0, CONV_ROWS), :] = (y * _sigmoid(y)).astype(BF16)
            return carry

        lax.fori_loop(0, seq_len // CONV_ROWS, rows, 0)

    @pl.when(pl.program_id(0) < N_CTX_GROUPS)
    def _():
        for s in range(TM // SEQ):
            conv_sequence(s * SEQ, SEQ)

    @pl.when(pl.program_id(0) >= N_CTX_GROUPS)
    def _():
        conv_sequence(0, DEC_SEQ)

    for rows in _row_slabs():
        r = jnp.dot(a_ref[rows, :], w_ref[...], preferred_element_type=F32) + b_ref[...]
        o_ref[rows, :] = x_ref[rows, :] + _mod_slice(mod_ref, 2) * r


def _conv_tail(x, u, w_dw, b_dw, ln_g, ln_b, w_pw2, b_pw2, mod):
    tok = pl.BlockSpec((TM, D_MODEL), lambda g: (g, 0))
    const = lambda g: (0, 0)
    vec = pl.BlockSpec((1, D_MODEL), const)
    return pl.pallas_call(
        _conv_tail_kernel,
        grid=(N_GROUPS,),
        in_specs=[tok, tok, pl.BlockSpec((CONV_WIDTH, D_MODEL), const), vec, vec, vec,
                  pl.BlockSpec((D_MODEL, D_MODEL), const), vec,
                  pl.BlockSpec((1, 1, 6 * D_MODEL), lambda g: (g, 0, 0))],
        out_specs=tok,
        out_shape=jax.ShapeDtypeStruct((N_TOK, D_MODEL), F32),
        scratch_shapes=[pltpu.VMEM((TM + 2 * CONV_PAD, D_MODEL), F32),
                        pltpu.VMEM((TM, D_MODEL), BF16)],
        compiler_params=_cparams(("arbitrary",)),
        name="conv_tail",
    )(x, u, w_dw, b_dw.reshape(1, -1), ln_g.reshape(1, -1), ln_b.reshape(1, -1), w_pw2,
      b_pw2.reshape(1, -1), mod)


def _log_sigmoid(z):
    return -(jnp.maximum(-z, 0.0) + jnp.log(1.0 + jnp.exp(-jnp.abs(z))))


def _gla_scan_kernel(*refs, seq_len, has_init, emit_state):
    it = iter(refs)
    q_ref, k_ref, v_ref, r_ref, w2_ref, bgk_ref = (next(it) for _ in range(6))
    s0_refs = (next(it), next(it)) if has_init else None
    o_ref = next(it)
    st_out = (next(it), next(it)) if emit_state else None
    g_ref, cum_ref, st_ref = (next(it) for _ in range(3))

    c = GLA_CHUNK
    n_chunks = seq_len // c
    rlow = r_ref[...].astype(BF16)
    row = lax.broadcasted_iota(jnp.int32, (c, 1), 0)
    sub8 = lax.broadcasted_iota(jnp.int32, (8, 1), 0)
    ri = lax.broadcasted_iota(jnp.int32, (c, c), 0)
    ci = lax.broadcasted_iota(jnp.int32, (c, c), 1)
    pair_xor = ri ^ ci

    for d in range(2):
        reverse = d == 1
        z = jnp.dot(rlow, w2_ref[d].astype(BF16), preferred_element_type=F32) + bgk_ref[d]
        g_ref[...] = _log_sigmoid(z) * (1.0 / GLA_GATE_NORM)
        if has_init:
            st_ref[...] = s0_refs[d][0, 0].T
        else:
            st_ref[...] = jnp.zeros_like(st_ref)

        order = range(n_chunks - 1, -1, -1) if reverse else range(n_chunks)
        for ch in order:
            base = ch * c
            rows = slice(base, base + c)
            q = q_ref[rows, :]
            k = k_ref[rows, :]
            v = v_ref[rows, :].astype(BF16)
            cum = g_ref[rows, :]
            sh = 1
            while sh < c:
                if reverse:
                    cum = cum + jnp.where(row < c - sh, pltpu.roll(cum, c - sh, 0), 0.0)
                else:
                    cum = cum + jnp.where(row >= sh, pltpu.roll(cum, sh, 0), 0.0)
                sh *= 2
            cum_ref[...] = cum

            att = None
            m = c
            while m >= 2:
                half = m // 2
                arow = half if reverse else half - 1
                if m >= 8:
                    pieces = [jnp.broadcast_to(cum_ref[b * m + arow:b * m + arow + 1, :], (m, GLA_DK))
                              for b in range(c // m)]
                else:
                    pieces = []
                    for t in range(c // 8):
                        tile = None
                        for b in range(8 // m - 1, -1, -1):
                            r = t * 8 + b * m + arow
                            cand = jnp.broadcast_to(cum_ref[r:r + 1, :], (8, GLA_DK))
                            tile = cand if tile is None else jnp.where(sub8 < (b + 1) * m, cand, tile)
                        pieces.append(tile)
                anchor = pieces[0] if len(pieces) == 1 else jnp.concatenate(pieces, axis=0)
                fac = jnp.exp(-jnp.abs(cum - anchor))
                p = _nt_dot((q * fac).astype(BF16), (k * fac).astype(BF16))
                att = p if att is None else jnp.where(pair_xor < m, p, att)
                m //= 2
            att = jnp.where(ri == ci, jnp.sum(q * k, axis=-1, keepdims=True), att)
            att = jnp.where((ri <= ci) if reverse else (ri >= ci), att, 0.0)

            last = 0 if reverse else c - 1
            total = cum_ref[last:last + 1, :]
            q_in = (q * jnp.exp(cum)).astype(BF16)
            k_out = (k * jnp.exp(total - cum)).astype(BF16)
            st = st_ref[...]
            o = (jnp.dot(att.astype(BF16), v, preferred_element_type=F32)
                 + _nt_dot(q_in, st.astype(BF16)))
            if reverse:
                o_ref[rows, :] += o
            else:
                o_ref[rows, :] = o
            kv = lax.dot_general(v, k_out, (((0,), (0,)), ((), ())), preferred_element_type=F32)
            st_ref[...] = st * jnp.exp(total) + kv

        if emit_state:
            st_out[d][0, 0] = st_ref[...].T


def _gla_scan(y, w2pad, b_gk, init_states, seq_len, blk0, n_seq, emit_state):
    has_init = init_states is not None
    qk = lambda off: pl.BlockSpec((seq_len, GLA_DK), lambda b, h: (b + blk0, h + off))
    state = pl.BlockSpec((1, 1, GLA_DK, GLA_DV), lambda b, h: (b, h, 0, 0))
    in_specs = [
        qk(0), qk(GLA_KEY_DIM // GLA_DK),
        pl.BlockSpec((seq_len, GLA_DV), lambda b, h: (b + blk0, h + 2 * GLA_KEY_DIM // GLA_DV)),
        pl.BlockSpec((seq_len, 128), lambda b, h: (b + blk0, (2 * GLA_KEY_DIM + 2 * GLA_VAL_DIM) // 128)),
        pl.BlockSpec((2, 128, GLA_DK), lambda b, h: (0, 0, h)),
        pl.BlockSpec((2, 1, GLA_DK), lambda b, h: (0, 0, h)),
    ]
    args = [y, y, y, y, w2pad, b_gk]
    if has_init:
        in_specs += [state, state]
        args += list(init_states)
    out_specs = [pl.BlockSpec((seq_len, GLA_DV), lambda b, h: (b, h))]
    out_shape = [jax.ShapeDtypeStruct((n_seq * seq_len, GLA_VAL_DIM), F32)]
    if emit_state:
        out_specs += [state, state]
        out_shape += [jax.ShapeDtypeStruct((n_seq, GLA_HEADS, GLA_DK, GLA_DV), F32)] * 2
    return pl.pallas_call(
        functools.partial(_gla_scan_kernel, seq_len=seq_len, has_init=has_init, emit_state=emit_state),
        grid=(n_seq, GLA_HEADS),
        in_specs=in_specs,
        out_specs=out_specs,
        out_shape=out_shape,
        scratch_shapes=[pltpu.VMEM((seq_len, GLA_DK), F32), pltpu.VMEM((GLA_CHUNK, GLA_DK), F32),
                        pltpu.VMEM((GLA_DV, GLA_DK), F32)],
        compiler_params=_cparams(("arbitrary", "arbitrary")),
        name="gla_scan",
    )(*args)


_GROUP_COND_ROW = np.array([0] * N_CTX_GROUPS + list(range(1, DEC_BATCH + 1)))


def kernel(x_prompt, x_sample, c, cache_attn_k, cache_attn_v, state_gla_fwd, state_gla_bwd, c_ctx,
           mod_w, mod_b, norm1_g, norm2_g,
           attn_w_qkv, attn_w_o, attn_q_norm, attn_k_norm, attn_rpb,
           conv_w_pw1, conv_b_pw1, conv_w_dw, conv_b_dw, conv_ln_g, conv_ln_b, conv_w_pw2, conv_b_pw2,
           gla_w_q, gla_w_k, gla_w_v, gla_w_g, gla_w_gk1, gla_w_gk2, gla_b_gk, gla_o_norm, gla_w_o,
           ffn_w_up, ffn_b_up, ffn_w_dw, ffn_b_dw, ffn_w_down, ffn_b_down):
    x = jnp.concatenate([x_prompt.reshape(N_CTX_TOK, D_MODEL), x_sample.reshape(N_LAT_TOK, D_MODEL)], axis=0)
    cond = jnp.concatenate([c_ctx[None, :], c, jnp.zeros((N_COND - 1 - DEC_BATCH, D_MODEL), F32)], axis=0)
    mod_all = _modulation(cond, mod_w, mod_b)
    mod_all = mod_all[:, _GROUP_COND_ROW][:, :, None, :]

    cache_k = cache_attn_k.reshape(DEC_BATCH, -1, PAST_LEN, D_MODEL)
    cache_v = cache_attn_v.reshape(DEC_BATCH, -1, PAST_LEN, D_MODEL)
    ffn_w_up_bf16 = ffn_w_up.astype(BF16)
    ffn_w_down_bf16 = ffn_w_down.astype(BF16)
    new_k, new_v, new_sf, new_sb = [], [], [], []
    for i in range(DEPTH):
        kind, j = i % N_MIXERS, i // N_MIXERS
        mod = mod_all[i]
        n1 = norm1_g[i].reshape(1, D_MODEL)
        if kind == 0:
            w_qkv = attn_w_qkv[j].astype(BF16)
            gain = jnp.concatenate([jnp.tile(attn_q_norm[j], NA_HEADS) * (NA_HEAD_DIM ** -0.5),
                                    jnp.tile(attn_k_norm[j], NA_HEADS),
                                    jnp.ones((D_MODEL,), F32)]).reshape(1, 3 * D_MODEL)
            qp, kp, vp = _qkv_proj(x, n1, mod, w_qkv, gain, 0, N_CTX_GROUPS, F32)
            qs, ks, vs = _qkv_proj(x, n1, mod, w_qkv, gain, N_CTX_GROUPS, N_GROUPS - N_CTX_GROUPS, BF16)
            o_ctx = _ctx_attention(qp, kp, vp)
            o_lat = _na_attention(qs, ks, vs, cache_k, cache_v, j, _na_bias_tables(attn_rpb[j]))
            x = _out_proj(x, o_ctx, o_lat, attn_w_o[j].astype(BF16), mod)
            new_k.append(kp.reshape(BATCH, SEQ, NA_HEADS, NA_HEAD_DIM))
            new_v.append(vp.reshape(BATCH, SEQ, NA_HEADS, NA_HEAD_DIM))
        elif kind == 1:
            u = _glu_proj(x, n1, mod, conv_w_pw1[j].astype(BF16), conv_b_pw1[j])
            x = _conv_tail(x, u, conv_w_dw[j], conv_b_dw[j], conv_ln_g[j], conv_ln_b[j],
                           conv_w_pw2[j].astype(BF16), conv_b_pw2[j], mod)
        else:
            pad = GLA_PROJ_N - 2 * GLA_KEY_DIM - 2 * GLA_VAL_DIM - 2 * GLA_GATE_RANK
            w_cat = jnp.concatenate([gla_w_q[j], gla_w_k[j], gla_w_v[j], gla_w_g[j], gla_w_gk1[j, 0],
                                     gla_w_gk1[j, 1], jnp.zeros((D_MODEL, pad), F32)], axis=1).astype(BF16)
            colscale = jnp.concatenate([jnp.full((GLA_KEY_DIM,), GLA_DK ** -0.5, F32),
                                        jnp.ones((GLA_PROJ_N - GLA_KEY_DIM,), F32)]).reshape(1, GLA_PROJ_N)
            y = _scaled_proj(x, n1, mod, w_cat, colscale)
            w2pad = jnp.zeros((2, 128, GLA_KEY_DIM), F32)
            w2pad = w2pad.at[0, :GLA_GATE_RANK].set(gla_w_gk2[j, 0])
            w2pad = w2pad.at[1, GLA_GATE_RANK:2 * GLA_GATE_RANK].set(gla_w_gk2[j, 1])
            b_gk = gla_b_gk[j].reshape(2, 1, GLA_KEY_DIM)
            o_ctx, sf, sb = _gla_scan(y, w2pad, b_gk, None, SEQ, 0, BATCH, True)
            (o_lat,) = _gla_scan(y, w2pad, b_gk, (state_gla_fwd[:, j], state_gla_bwd[:, j]),
                                 DEC_SEQ, N_CTX_GROUPS, DEC_BATCH, False)
            x = _gla_out_proj(x, o_ctx, o_lat, y, gla_o_norm[j].reshape(1, GLA_DV),
                              gla_w_o[j].astype(BF16), mod)
            new_sf.append(sf)
            new_sb.append(sb)
        x = _ffn(x, norm2_g[i].reshape(1, D_MODEL), mod, i, ffn_w_up_bf16, ffn_b_up[i],
                 ffn_w_dw[i], ffn_b_dw[i], ffn_w_down_bf16, ffn_b_down[i])

    y_prompt = x[:N_CTX_TOK].reshape(BATCH, SEQ, D_MODEL)
    y_sample = x[N_CTX_TOK:].reshape(DEC_BATCH, DEC_SEQ, D_MODEL)
    return (y_prompt, y_sample, jnp.stack(new_k, axis=1), jnp.stack(new_v, axis=1),
            jnp.stack(new_sf, axis=1), jnp.stack(new_sb, axis=1))
```

```python
import functools

import numpy as np
import jax
import jax.numpy as jnp
from jax import lax
from jax.experimental import pallas as pl
from jax.experimental.pallas import tpu as pltpu

F32 = jnp.float32
BF16 = jnp.bfloat16

D_MODEL = 1024
BATCH = 16
SEQ = 256
DEPTH = 4
DEC_BATCH = 4
DEC_SEQ = 1024
PAST_LEN = 512
GRID_W = 64
N_MIXERS = 3
NA_HEADS = 16
NA_HEAD_DIM = 64
NA_WIN_R = 8
NA_WIN_C = 16
CONV_WIDTH = 31
GLA_HEADS = 4
GLA_KEY_DIM = 512
GLA_VAL_DIM = 1024
GLA_DK = 128
GLA_DV = 256
GLA_GATE_RANK = 16
GLA_GATE_NORM = 16.0
D_FF = 2816
NORM_EPS = 1e-6
NEG_INF = -1e30

TM = 1024
N_CTX_TOK = BATCH * SEQ
N_LAT_TOK = DEC_BATCH * DEC_SEQ
N_TOK = N_CTX_TOK + N_LAT_TOK
N_CTX_GROUPS = N_CTX_TOK // TM
N_GROUPS = N_TOK // TM
N_COND = 8
TF = 256
PROJ_SLAB = 256
FFN_SLAB = SEQ
FFN_HALO = 8
GLA_CHUNK = 256
GLA_PROJ_N = 3328
VMEM_LIMIT = 60 * 1024 * 1024


def _cparams(sem, flags=None):
    return pltpu.CompilerParams(dimension_semantics=sem, vmem_limit_bytes=VMEM_LIMIT, flags=flags)


def _sigmoid(x):
    return 1.0 / (1.0 + jnp.exp(-x))


def _norm_mod(x, g, shift, scale):
    ms = jnp.mean(x * x, axis=-1, keepdims=True)
    y = x * lax.rsqrt(ms + NORM_EPS) * g
    return y * (1.0 + scale) + shift


def _mod_slice(mod_ref, idx):
    return mod_ref[0, :, idx * D_MODEL:(idx + 1) * D_MODEL]


def _mod_kernel(cond_ref, w_ref, b_ref, o_ref):
    c = cond_ref[...]
    s = (c * _sigmoid(c)).astype(BF16)
    o_ref[0] = jnp.dot(s, w_ref[0].astype(BF16), preferred_element_type=F32) + b_ref[0]


def _modulation(cond, mod_w, mod_b):
    tn = 1536
    n = 6 * D_MODEL
    return pl.pallas_call(
        _mod_kernel,
        grid=(DEPTH, n // tn),
        in_specs=[
            pl.BlockSpec((N_COND, D_MODEL), lambda l, j: (0, 0)),
            pl.BlockSpec((1, D_MODEL, tn), lambda l, j: (l, 0, j)),
            pl.BlockSpec((1, 1, tn), lambda l, j: (l, 0, j)),
        ],
        out_specs=pl.BlockSpec((1, N_COND, tn), lambda l, j: (l, 0, j)),
        out_shape=jax.ShapeDtypeStruct((DEPTH, N_COND, n), F32),
        compiler_params=_cparams(("arbitrary", "arbitrary")),
        name="modulation",
    )(cond, mod_w, mod_b.reshape(DEPTH, 1, n))


def _proj_slabs(x_ref, g_ref, mod_ref, h_ref, epilogue):
    def run(first):
        for s in range(TM // PROJ_SLAB):
            rows = slice(s * PROJ_SLAB, (s + 1) * PROJ_SLAB)
            if first:
                h = _norm_mod(x_ref[rows, :], g_ref[...], _mod_slice(mod_ref, 0), _mod_slice(mod_ref, 1))
                h_ref[rows, :] = h.astype(BF16)
            epilogue(rows, h_ref[rows, :])

    pl.when(pl.program_id(1) == 0)(lambda: run(True))
    pl.when(pl.program_id(1) != 0)(lambda: run(False))


def _head_rms(acc, gain, hsum_ref, hexp_ref):
    ms = jnp.dot((acc * acc).astype(BF16), hsum_ref[...], preferred_element_type=F32)
    inv = lax.rsqrt(ms + NORM_EPS)
    hi = inv.astype(BF16)
    lo = (inv - hi.astype(F32)).astype(BF16)
    inv_full = jnp.dot(jnp.concatenate([hi, lo], axis=1), hexp_ref[...], preferred_element_type=F32)
    return acc * inv_full * gain


def _qkv_kernel(x_ref, g_ref, mod_ref, w_ref, gain_ref, hsum_ref, hexp_ref,
                q_ref, k_ref, v_ref, h_ref, t_ref, *, kv_channel_major):
    j = pl.program_id(1)

    def column_step(out_ref, normed, first, channel_major):
        for s in range(TM // PROJ_SLAB):
            rows = slice(s * PROJ_SLAB, (s + 1) * PROJ_SLAB)
            if first:
                h = _norm_mod(x_ref[rows, :], g_ref[...], _mod_slice(mod_ref, 0), _mod_slice(mod_ref, 1))
                h_ref[rows, :] = h.astype(BF16)
            acc = jnp.dot(h_ref[rows, :], w_ref[...], preferred_element_type=F32)
            if normed:
                acc = _head_rms(acc, gain_ref[...], hsum_ref, hexp_ref)
            if channel_major:
                t_ref[...] = acc
                out_ref[s] = t_ref[...].T
            else:
                out_ref[rows, :] = acc.astype(out_ref.dtype)

    pl.when(j == 0)(lambda: column_step(q_ref, True, True, False))
    pl.when(j == 1)(lambda: column_step(k_ref, True, False, kv_channel_major))
    pl.when(j == 2)(lambda: column_step(v_ref, False, False, kv_channel_major))


def _head_matrices():
    lane = np.arange(D_MODEL)
    hsum = np.zeros((D_MODEL, 128), np.float32)
    hsum[lane, lane // NA_HEAD_DIM] = 1.0 / NA_HEAD_DIM
    hexp = np.zeros((128, D_MODEL), np.float32)
    hexp[lane // NA_HEAD_DIM, lane] = 1.0
    return jnp.asarray(hsum, BF16), jnp.asarray(np.concatenate([hexp, hexp], 0), BF16)


def _qkv_proj(x, norm_g, mod, w_qkv, gain, group0, n_groups, kv_channel_major):
    hsum, hexp = _head_matrices()
    ntok = n_groups * TM
    row = lambda g, j: (g, 0)
    tok_spec = pl.BlockSpec((TM, D_MODEL), row)
    if kv_channel_major:
        kv_spec = pl.BlockSpec((TM // SEQ, D_MODEL, SEQ), lambda g, j: (g, 0, 0))
        kv_shape = jax.ShapeDtypeStruct((ntok // SEQ, D_MODEL, SEQ), F32)
    else:
        kv_spec = tok_spec
        kv_shape = jax.ShapeDtypeStruct((ntok, D_MODEL), BF16)
    return pl.pallas_call(
        functools.partial(_qkv_kernel, kv_channel_major=kv_channel_major),
        grid=(n_groups, 3),
        in_specs=[
            pl.BlockSpec((TM, D_MODEL), lambda g, j: (g + group0, 0)),
            pl.BlockSpec((1, D_MODEL), lambda g, j: (0, 0)),
            pl.BlockSpec((1, 1, 6 * D_MODEL), lambda g, j: (g + group0, 0, 0)),
            pl.BlockSpec((D_MODEL, D_MODEL), lambda g, j: (0, j)),
            pl.BlockSpec((1, D_MODEL), lambda g, j: (0, j)),
            pl.BlockSpec((D_MODEL, 128), lambda g, j: (0, 0)),
            pl.BlockSpec((256, D_MODEL), lambda g, j: (0, 0)),
        ],
        out_specs=[tok_spec, kv_spec, kv_spec],
        out_shape=[jax.ShapeDtypeStruct((ntok, D_MODEL), BF16), kv_shape, kv_shape],
        scratch_shapes=[pltpu.VMEM((TM, D_MODEL), BF16), pltpu.VMEM((PROJ_SLAB, D_MODEL), F32)],
        compiler_params=_cparams(("arbitrary", "arbitrary")),
        name="qkv_proj",
    )(x, norm_g, mod, w_qkv, gain, hsum, hexp)


def _glu_kernel(x_ref, g_ref, mod_ref, wa_ref, wg_ref, ba_ref, bg_ref, u_ref, h_ref):
    def epilogue(rows, h):
        a = jnp.dot(h, wa_ref[...], preferred_element_type=F32) + ba_ref[...]
        g = jnp.dot(h, wg_ref[...], preferred_element_type=F32) + bg_ref[...]
        u_ref[rows, :] = a * _sigmoid(g)

    _proj_slabs(x_ref, g_ref, mod_ref, h_ref, epilogue)


def _glu_proj(x, norm_g, mod, w_pw1, b_pw1):
    tn = 512
    nj = D_MODEL // tn
    b = b_pw1.reshape(1, 2 * D_MODEL)
    return pl.pallas_call(
        _glu_kernel,
        grid=(N_GROUPS, nj),
        in_specs=[
            pl.BlockSpec((TM, D_MODEL), lambda g, j: (g, 0)),
            pl.BlockSpec((1, D_MODEL), lambda g, j: (0, 0)),
            pl.BlockSpec((1, 1, 6 * D_MODEL), lambda g, j: (g, 0, 0)),
            pl.BlockSpec((D_MODEL, tn), lambda g, j: (0, j)),
            pl.BlockSpec((D_MODEL, tn), lambda g, j: (0, j + nj)),
            pl.BlockSpec((1, tn), lambda g, j: (0, j)),
            pl.BlockSpec((1, tn), lambda g, j: (0, j + nj)),
        ],
        out_specs=pl.BlockSpec((TM, tn), lambda g, j: (g, j)),
        out_shape=jax.ShapeDtypeStruct((N_TOK, D_MODEL), F32),
        scratch_shapes=[pltpu.VMEM((TM, D_MODEL), BF16)],
        compiler_params=_cparams(("arbitrary", "arbitrary")),
        name="glu_proj",
    )(x, norm_g, mod, w_pw1, w_pw1, b, b)


def _scaled_proj_kernel(x_ref, g_ref, mod_ref, w_ref, cs_ref, y_ref, h_ref):
    def epilogue(rows, h):
        y_ref[rows, :] = jnp.dot(h, w_ref[...], preferred_element_type=F32) * cs_ref[...]

    _proj_slabs(x_ref, g_ref, mod_ref, h_ref, epilogue)


def _scaled_proj(x, norm_g, mod, w, colscale):
    n = w.shape[1]
    tn = n // 2
    return pl.pallas_call(
        _scaled_proj_kernel,
        grid=(N_GROUPS, n // tn),
        in_specs=[
            pl.BlockSpec((TM, D_MODEL), lambda g, j: (g, 0)),
            pl.BlockSpec((1, D_MODEL), lambda g, j: (0, 0)),
            pl.BlockSpec((1, 1, 6 * D_MODEL), lambda g, j: (g, 0, 0)),
            pl.BlockSpec((D_MODEL, tn), lambda g, j: (0, j)),
            pl.BlockSpec((1, tn), lambda g, j: (0, j)),
        ],
        out_specs=pl.BlockSpec((TM, tn), lambda g, j: (g, j)),
        out_shape=jax.ShapeDtypeStruct((N_TOK, n), F32),
        scratch_shapes=[pltpu.VMEM((TM, D_MODEL), BF16)],
        compiler_params=_cparams(("arbitrary", "arbitrary")),
        name="gla_proj",
    )(x, norm_g, mod, w, colscale)


_CTX_PART = pl.BlockSpec((TM, D_MODEL), lambda g: (jnp.minimum(g, N_CTX_GROUPS - 1), 0))
_LAT_PART = pl.BlockSpec((TM, D_MODEL), lambda g: (jnp.maximum(g - N_CTX_GROUPS, 0), 0))


def _group_part(ctx_ref, lat_ref, rows):
    return jnp.where(pl.program_id(0) < N_CTX_GROUPS, ctx_ref[rows, :], lat_ref[rows, :])


def _row_slabs():
    return [slice(s * PROJ_SLAB, (s + 1) * PROJ_SLAB) for s in range(TM // PROJ_SLAB)]


def _oproj_kernel(x_ref, a_ctx_ref, a_lat_ref, w_ref, mod_ref, o_ref):
    for rows in _row_slabs():
        r = jnp.dot(_group_part(a_ctx_ref, a_lat_ref, rows), w_ref[...], preferred_element_type=F32)
        o_ref[rows, :] = x_ref[rows, :] + _mod_slice(mod_ref, 2) * r


def _out_proj(x, a_ctx, a_lat, w, mod):
    return pl.pallas_call(
        _oproj_kernel,
        grid=(N_GROUPS,),
        in_specs=[
            pl.BlockSpec((TM, D_MODEL), lambda g: (g, 0)),
            _CTX_PART, _LAT_PART,
            pl.BlockSpec((D_MODEL, D_MODEL), lambda g: (0, 0)),
            pl.BlockSpec((1, 1, 6 * D_MODEL), lambda g: (g, 0, 0)),
        ],
        out_specs=pl.BlockSpec((TM, D_MODEL), lambda g: (g, 0)),
        out_shape=jax.ShapeDtypeStruct((N_TOK, D_MODEL), F32),
        compiler_params=_cparams(("arbitrary",)),
        name="out_proj",
    )(x, a_ctx, a_lat, w, mod)


def _gla_oproj_kernel(x_ref, o_ctx_ref, o_lat_ref, gz_ref, og_ref, w_ref, mod_ref, out_ref):
    og = og_ref[...]
    for rows in _row_slabs():
        o = _group_part(o_ctx_ref, o_lat_ref, rows)
        parts = []
        for h in range(GLA_HEADS):
            oh = o[:, h * GLA_DV:(h + 1) * GLA_DV]
            ms = jnp.mean(oh * oh, axis=-1, keepdims=True)
            parts.append(oh * lax.rsqrt(ms + NORM_EPS) * og)
        y = jnp.concatenate(parts, axis=1)
        gz = gz_ref[rows, :]
        a = (y * (gz * _sigmoid(gz))).astype(BF16)
        r = jnp.dot(a, w_ref[...], preferred_element_type=F32)
        out_ref[rows, :] = x_ref[rows, :] + _mod_slice(mod_ref, 2) * r


def _gla_out_proj(x, o_ctx, o_lat, y, o_norm, w, mod):
    return pl.pallas_call(
        _gla_oproj_kernel,
        grid=(N_GROUPS,),
        in_specs=[
            pl.BlockSpec((TM, D_MODEL), lambda g: (g, 0)),
            _CTX_PART, _LAT_PART,
            pl.BlockSpec((TM, GLA_VAL_DIM), lambda g: (g, 2)),
            pl.BlockSpec((1, GLA_DV), lambda g: (0, 0)),
            pl.BlockSpec((GLA_VAL_DIM, D_MODEL), lambda g: (0, 0)),
            pl.BlockSpec((1, 1, 6 * D_MODEL), lambda g: (g, 0, 0)),
        ],
        out_specs=pl.BlockSpec((TM, D_MODEL), lambda g: (g, 0)),
        out_shape=jax.ShapeDtypeStruct((N_TOK, D_MODEL), F32),
        compiler_params=_cparams(("arbitrary",)),
        name="gla_out_proj",
    )(x, o_ctx, o_lat, y, o_norm, w, mod)


def _ffn_kernel(x_ref, g_ref, mod_ref, wup_ref, bup_ref, wdw_ref, bdw_ref, wdn_ref, bdn_ref,
                out_ref, h_ref, acc_ref, ua0_ref, ug0_ref, ua1_ref, ug1_ref, act0_ref, act1_ref):
    is_lat = pl.program_id(0) >= N_CTX_GROUPS
    up_bufs = ((ua0_ref, ug0_ref), (ua1_ref, ug1_ref))
    act_bufs = (act0_ref, act1_ref)
    n_slabs = TM // FFN_SLAB
    win = FFN_SLAB + 2 * FFN_HALO

    def chunk_cols(c):
        return pl.multiple_of(c * TF, TF), pl.multiple_of(D_FF + c * TF, TF)

    def up_slab(c, slot, s):
        h = h_ref[s * FFN_SLAB:(s + 1) * FFN_SLAB, :]
        for buf, col in zip(up_bufs[slot], chunk_cols(c)):
            res = jnp.dot(h, wup_ref[:, pl.ds(col, TF)], preferred_element_type=F32)
            pad = jnp.broadcast_to(-bup_ref[:, pl.ds(col, TF)], (FFN_HALO, TF))
            base = s * win
            buf[base + FFN_HALO:base + FFN_HALO + FFN_SLAB, :] = res
            if s == 0:
                buf[0:FFN_HALO, :] = pad
            else:
                buf[base - FFN_HALO:base, :] = jnp.where(is_lat, res[:FFN_HALO], pad)
            if s == n_slabs - 1:
                buf[base + win - FFN_HALO:base + win, :] = pad
            else:
                buf[base + win:base + win + FFN_HALO, :] = jnp.where(is_lat, res[FFN_SLAB - FFN_HALO:], pad)

    def conv_slab(buf, col, s):
        r0 = s * win + FFN_HALO
        prev = buf[r0 - 1:r0 - 1 + FFN_SLAB, :]
        mid = buf[r0:r0 + FFN_SLAB, :]
        nxt = buf[r0 + 1:r0 + 1 + FFN_SLAB, :]
        w = wdw_ref[:, pl.ds(col, TF)]
        bias = bdw_ref[:, pl.ds(col, TF)] + bup_ref[:, pl.ds(col, TF)] * (w[0:1] + w[1:2] + w[2:3])
        return w[0:1] * prev + w[1:2] * mid + w[2:3] * nxt + bias

    def gate_slab(c, slot, s):
        ca, cg = chunk_cols(c)
        a = conv_slab(up_bufs[slot][0], ca, s)
        g = conv_slab(up_bufs[slot][1], cg, s)
        act_bufs[slot][s * FFN_SLAB:(s + 1) * FFN_SLAB, :] = (g * _sigmoid(g) * a).astype(BF16)

    def down_slab(c, slot, s):
        rows = slice(s * FFN_SLAB, (s + 1) * FFN_SLAB)
        acc_ref[rows, :] += jnp.dot(act_bufs[slot][rows, :], wdn_ref[pl.ds(chunk_cols(c)[0], TF), :],
                                    preferred_element_type=F32)

    n_chunks = D_FF // TF
    acc_ref[...] = jnp.zeros_like(acc_ref)
    for s in range(n_slabs):
        rows = slice(s * FFN_SLAB, (s + 1) * FFN_SLAB)
        h = _norm_mod(x_ref[rows, :], g_ref[...], _mod_slice(mod_ref, 3), _mod_slice(mod_ref, 4))
        h_ref[rows, :] = h.astype(BF16)
        up_slab(0, 0, s)

    def body(i, carry):
        for slot in range(2):
            c = 2 * i + slot
            for s in range(n_slabs):
                up_slab(c + 1, 1 - slot, s)
                gate_slab(c, slot, s)
                down_slab(c, slot, s)
        return carry

    lax.fori_loop(0, (n_chunks - 1) // 2, body, 0)
    for s in range(n_slabs):
        rows = slice(s * FFN_SLAB, (s + 1) * FFN_SLAB)
        gate_slab(n_chunks - 1, 0, s)
        down_slab(n_chunks - 1, 0, s)
        out_ref[rows, :] = x_ref[rows, :] + _mod_slice(mod_ref, 5) * (acc_ref[rows, :] + bdn_ref[...])


def _ffn(x, norm_g, mod, layer, w_up, b_up, w_dw, b_dw, w_down, b_down):
    const = lambda g: (0, 0)
    layer_block = lambda g: (layer, 0, 0)
    resident = dict(pipeline_mode=pl.Buffered(1))
    return pl.pallas_call(
        _ffn_kernel,
        grid=(N_GROUPS,),
        in_specs=[
            pl.BlockSpec((TM, D_MODEL), lambda g: (g, 0)),
            pl.BlockSpec((1, D_MODEL), const),
            pl.BlockSpec((1, 1, 6 * D_MODEL), lambda g: (g, 0, 0)),
            pl.BlockSpec((None, D_MODEL, 2 * D_FF), layer_block, **resident),
            pl.BlockSpec((1, 2 * D_FF), const),
            pl.BlockSpec((3, 2 * D_FF), const),
            pl.BlockSpec((1, 2 * D_FF), const),
            pl.BlockSpec((None, D_FF, D_MODEL), layer_block, **resident),
            pl.BlockSpec((1, D_MODEL), const),
        ],
        out_specs=pl.BlockSpec((TM, D_MODEL), lambda g: (g, 0)),
        out_shape=jax.ShapeDtypeStruct((N_TOK, D_MODEL), F32),
        scratch_shapes=[pltpu.VMEM((TM, D_MODEL), BF16), pltpu.VMEM((TM, D_MODEL), F32)]
        + [pltpu.VMEM((TM // FFN_SLAB * (FFN_SLAB + 2 * FFN_HALO), TF), F32)] * 4
        + [pltpu.VMEM((TM, TF), BF16)] * 2,
        compiler_params=_cparams(("arbitrary",)),
        name="conv_ffn",
    )(x, norm_g, mod, w_up, b_up.reshape(1, -1), w_dw, b_dw.reshape(1, -1), w_down, b_down.reshape(1, -1))


def _pair_queries(q):
    lane = lax.broadcasted_iota(jnp.int32, q.shape, 1)
    zero = jnp.zeros_like(q)
    return jnp.concatenate([jnp.where(lane < NA_HEAD_DIM, q, zero),
                            jnp.where(lane < NA_HEAD_DIM, zero, q)], axis=0)


def _pair_merge(o2):
    n = o2.shape[0] // 2
    lane = lax.broadcasted_iota(jnp.int32, (n, o2.shape[1]), 1)
    return jnp.where(lane < NA_HEAD_DIM, o2[:n], o2[n:])


def _nt_dot(a, b):
    return lax.dot_general(a, b, (((1,), (1,)), ((), ())), preferred_element_type=F32)


def _ctx_attn_kernel(q_ref, kt_ref, vt_ref, o_ref):
    for b in range(TM // SEQ):
        rows = slice(b * SEQ, (b + 1) * SEQ)
        q2 = _pair_queries(q_ref[rows, :])
        s = jnp.dot(q2, kt_ref[b].astype(BF16), preferred_element_type=F32)
        p = jnp.exp(s - jnp.max(s, axis=-1, keepdims=True))
        l = jnp.sum(p, axis=-1, keepdims=True)
        o2 = _nt_dot(p.astype(BF16), vt_ref[b].astype(BF16)) / l
        o_ref[rows, :] = _pair_merge(o2).astype(o_ref.dtype)


def _ctx_attention(q, kt, vt):
    spec = pl.BlockSpec((TM, 128), lambda g, hp: (g, hp))
    kv_spec = pl.BlockSpec((TM // SEQ, 128, SEQ), lambda g, hp: (g, hp, 0))
    return pl.pallas_call(
        _ctx_attn_kernel,
        grid=(N_CTX_GROUPS, NA_HEADS // 2),
        in_specs=[spec, kv_spec, kv_spec],
        out_specs=spec,
        out_shape=jax.ShapeDtypeStruct((N_CTX_TOK, D_MODEL), BF16),
        compiler_params=_cparams(("arbitrary", "arbitrary")),
        name="ctx_attention",
    )(q, kt, vt)


_NA_BLOCKS = ((0, 8, 0), (0, 12, 1), (4, 12, 1), (8, 8, 2))


def _na_attn_kernel(q_ref, k_ref, v_ref, ck_ref, cv_ref, ba_ref, bm_ref, bc_ref, o_ref):
    bias_refs = (ba_ref, bm_ref, bc_ref)
    ck = ck_ref[0, 0].astype(BF16)
    cv = cv_ref[0, 0].astype(BF16)
    for blk, (row0, nrows, bidx) in enumerate(_NA_BLOCKS):
        rows = slice(blk * 4 * GRID_W, (blk + 1) * 4 * GRID_W)
        keys = slice(row0 * GRID_W, (row0 + nrows) * GRID_W)
        q2 = _pair_queries(q_ref[rows, :])
        bias = bias_refs[bidx][...]
        s_loc = _nt_dot(q2, k_ref[keys, :]) + bias.reshape(2 * 4 * GRID_W, nrows * GRID_W)
        s_ctx = _nt_dot(q2, ck)
        m = jnp.maximum(jnp.max(s_loc, axis=-1, keepdims=True), jnp.max(s_ctx, axis=-1, keepdims=True))
        p_loc = jnp.exp(s_loc - m)
        p_ctx = jnp.exp(s_ctx - m)
        l = jnp.sum(p_loc, axis=-1, keepdims=True) + jnp.sum(p_ctx, axis=-1, keepdims=True)
        o2 = (jnp.dot(p_loc.astype(BF16), v_ref[keys, :], preferred_element_type=F32)
              + jnp.dot(p_ctx.astype(BF16), cv, preferred_element_type=F32)) / l
        o_ref[rows, :] = _pair_merge(o2).astype(o_ref.dtype)


N_RPB_R = 2 * NA_WIN_R - 1
N_RPB_C = 2 * NA_WIN_C - 1


def _na_bias_kernel(rpb_ref, ba_ref, bm_ref, bc_ref):
    base = pl.program_id(0) * (N_RPB_R * N_RPB_C)
    qc = lax.broadcasted_iota(jnp.int32, (GRID_W, 2 * GRID_W), 0)
    lane = lax.broadcasted_iota(jnp.int32, (GRID_W, 2 * GRID_W), 1)
    kc = lane & (GRID_W - 1)
    right = lane >= GRID_W
    dcol = kc - qc + (NA_WIN_C - 1)
    c_start = jnp.clip(qc - NA_WIN_C // 2, 0, GRID_W - NA_WIN_C)
    in_win = (kc >= c_start) & (kc < c_start + NA_WIN_C)
    neg = jnp.full((GRID_W, 2 * GRID_W), NEG_INF, F32)
    tiles = {}

    def rpb_at(dr, j):
        return rpb_ref[base + dr * N_RPB_C + j] if 0 <= dr < N_RPB_R else jnp.float32(0.0)

    def pair_tile(dr):
        if dr not in tiles:
            acc = jnp.zeros((GRID_W, 2 * GRID_W), F32)
            for j in range(N_RPB_C):
                acc = jnp.where(dcol == j, jnp.where(right, rpb_at(dr + 1, j), rpb_at(dr, j)), acc)
            tiles[dr] = acc
        return tiles[dr]

    rows_total = DEC_SEQ // GRID_W
    for ref, blk in ((ba_ref, 0), (bm_ref, 1), (bc_ref, 3)):
        row0, nrows, _ = _NA_BLOCKS[blk]
        for rr in range(4):
            r = blk * 4 + rr
            r_start = min(max(r - NA_WIN_R // 2, 0), rows_total - NA_WIN_R)
            for ip in range(nrows // 2):
                krow = row0 + 2 * ip
                ok_l = r_start <= krow < r_start + NA_WIN_R
                ok_r = r_start <= krow + 1 < r_start + NA_WIN_R
                if ok_l or ok_r:
                    mask = in_win
                    if not ok_l:
                        mask = mask & right
                    if not ok_r:
                        mask = mask & jnp.logical_not(right)
                    tile = jnp.where(mask, pair_tile(krow - r + NA_WIN_R - 1), neg)
                else:
                    tile = neg
                ref[0, rr * GRID_W:(rr + 1) * GRID_W, ip * 2 * GRID_W:(ip + 1) * 2 * GRID_W] = tile


def _na_bias_tables(rpb):
    out = lambda n: pl.BlockSpec((1, 4 * GRID_W, n), lambda h: (h, 0, 0))
    shape = lambda n: jax.ShapeDtypeStruct((NA_HEADS, 4 * GRID_W, n), F32)
    return pl.pallas_call(
        _na_bias_kernel,
        grid=(NA_HEADS,),
        in_specs=[pl.BlockSpec(memory_space=pltpu.SMEM)],
        out_specs=[out(8 * GRID_W), out(12 * GRID_W), out(8 * GRID_W)],
        out_shape=[shape(8 * GRID_W), shape(12 * GRID_W), shape(8 * GRID_W)],
        compiler_params=_cparams(("arbitrary",)),
        name="na_bias",
    )(rpb.astype(F32).reshape(-1))


def _na_attention(q, k, v, cache_k, cache_v, layer_j, bias_tables):
    ba, bm, bc = bias_tables
    tok = pl.BlockSpec((TM, 128), lambda hp, b: (b, hp))
    cache = pl.BlockSpec((1, 1, PAST_LEN, 128), lambda hp, b: (b, layer_j, 0, hp))
    bias = lambda n: pl.BlockSpec((2, 4 * GRID_W, n), lambda hp, b: (hp, 0, 0))
    return pl.pallas_call(
        _na_attn_kernel,
        grid=(NA_HEADS // 2, DEC_BATCH),
        in_specs=[tok, tok, tok, cache, cache, bias(8 * GRID_W), bias(12 * GRID_W), bias(8 * GRID_W)],
        out_specs=tok,
        out_shape=jax.ShapeDtypeStruct((N_LAT_TOK, D_MODEL), BF16),
        compiler_params=_cparams(("arbitrary", "arbitrary")),
        name="na_attention",
    )(q, k, v, cache_k, cache_v, ba, bm, bc)


CONV_PAD = 16
CONV_ROWS = 128
CONV_LANES = 128


def _conv_tail_kernel(x_ref, u_ref, wdw_ref, bdw_ref, lng_ref, lnb_ref, w_ref, b_ref, mod_ref,
                      o_ref, pad_ref, a_ref):
    half = CONV_WIDTH // 2
    win = CONV_ROWS + 8
    zeros = jnp.zeros((CONV_PAD, D_MODEL), F32)

    def conv_sequence(base, seq_len):
        pad_ref[0:CONV_PAD, :] = zeros
        pad_ref[CONV_PAD + seq_len:2 * CONV_PAD + seq_len, :] = zeros
        pad_ref[CONV_PAD:CONV_PAD + seq_len, :] = u_ref[base:base + seq_len, :]

        def rows(i, carry):
            r0 = pl.multiple_of(i * CONV_ROWS, CONV_ROWS)
            strips = []
            for l0 in range(0, D_MODEL, CONV_LANES):
                lanes = slice(l0, l0 + CONV_LANES)
                acc = jnp.zeros((CONV_ROWS, CONV_LANES), F32) + bdw_ref[:, lanes]
                for b in range(8):
                    z = None
                    for a in range(-2, 2):
                        t = 8 * a + b + half
                        if not 0 <= t < CONV_WIDTH:
                            continue
                        term = wdw_ref[t:t + 1, lanes] * pad_ref[pl.ds(r0 + (CONV_PAD + 8 * a), win), lanes]
                        z = term if z is None else z + term
                    if b:
                        z = pltpu.roll(z, win - b, 0)
                    acc = acc + z[:CONV_ROWS]
                strips.append(acc)
            acc = jnp.concatenate(strips, axis=1)
            mu = jnp.mean(acc, axis=-1, keepdims=True)
            cen = acc - mu
            var = jnp.mean(cen * cen, axis=-1, keepdims=True)
            y = cen * lax.rsqrt(var + NORM_EPS) * lng_ref[...] + lnb_ref[...]
            a_ref[pl.ds(base + r0, CONV_ROWS), :] = (y * _sigmoid(y)).astype(BF16)
            return carry

        lax.fori_loop(0, seq_len // CONV_ROWS, rows, 0)

    @pl.when(pl.program_id(0) < N_CTX_GROUPS)
    def _():
        for s in range(TM // SEQ):
            conv_sequence(s * SEQ, SEQ)

    @pl.when(pl.program_id(0) >= N_CTX_GROUPS)
    def _():
        conv_sequence(0, DEC_SEQ)

    for rows in _row_slabs():
        r = jnp.dot(a_ref[rows, :], w_ref[...], preferred_element_type=F32) + b_ref[...]
        o_ref[rows, :] = x_ref[rows, :] + _mod_slice(mod_ref, 2) * r


def _conv_tail(x, u, w_dw, b_dw, ln_g, ln_b, w_pw2, b_pw2, mod):
    tok = pl.BlockSpec((TM, D_MODEL), lambda g: (g, 0))
    const = lambda g: (0, 0)
    vec = pl.BlockSpec((1, D_MODEL), const)
    return pl.pallas_call(
        _conv_tail_kernel,
        grid=(N_GROUPS,),
        in_specs=[tok, tok, pl.BlockSpec((CONV_WIDTH, D_MODEL), const), vec, vec, vec,
                  pl.BlockSpec((D_MODEL, D_MODEL), const), vec,
                  pl.BlockSpec((1, 1, 6 * D_MODEL), lambda g: (g, 0, 0))],
        out_specs=tok,
        out_shape=jax.ShapeDtypeStruct((N_TOK, D_MODEL), F32),
        scratch_shapes=[pltpu.VMEM((TM + 2 * CONV_PAD, D_MODEL), F32),
                        pltpu.VMEM((TM, D_MODEL), BF16)],
        compiler_params=_cparams(("arbitrary",)),
        name="conv_tail",
    )(x, u, w_dw, b_dw.reshape(1, -1), ln_g.reshape(1, -1), ln_b.reshape(1, -1), w_pw2,
      b_pw2.reshape(1, -1), mod)


def _log_sigmoid(z):
    return -(jnp.maximum(-z, 0.0) + jnp.log(1.0 + jnp.exp(-jnp.abs(z))))


def _gla_scan_kernel(*refs, seq_len, has_init, emit_state):
    it = iter(refs)
    q_ref, k_ref, v_ref, r_ref, w2_ref, bgk_ref = (next(it) for _ in range(6))
    s0_refs = (next(it), next(it)) if has_init else None
    o_ref = next(it)
    st_out = (next(it), next(it)) if emit_state else None
    g_ref, cum_ref, st_ref = (next(it) for _ in range(3))

    c = GLA_CHUNK
    n_chunks = seq_len // c
    rlow = r_ref[...].astype(BF16)
    row = lax.broadcasted_iota(jnp.int32, (c, 1), 0)
    sub8 = lax.broadcasted_iota(jnp.int32, (8, 1), 0)
    ri = lax.broadcasted_iota(jnp.int32, (c, c), 0)
    ci = lax.broadcasted_iota(jnp.int32, (c, c), 1)
    pair_xor = ri ^ ci

    for d in range(2):
        reverse = d == 1
        z = jnp.dot(rlow, w2_ref[d].astype(BF16), preferred_element_type=F32) + bgk_ref[d]
        g_ref[...] = _log_sigmoid(z) * (1.0 / GLA_GATE_NORM)
        if has_init:
            st_ref[...] = s0_refs[d][0, 0].T
        else:
            st_ref[...] = jnp.zeros_like(st_ref)

        order = range(n_chunks - 1, -1, -1) if reverse else range(n_chunks)
        for ch in order:
            base = ch * c
            rows = slice(base, base + c)
            q = q_ref[rows, :]
            k = k_ref[rows, :]
            v = v_ref[rows, :].astype(BF16)
            cum = g_ref[rows, :]
            sh = 1
            while sh < c:
                if reverse:
                    cum = cum + jnp.where(row < c - sh, pltpu.roll(cum, c - sh, 0), 0.0)
                else:
                    cum = cum + jnp.where(row >= sh, pltpu.roll(cum, sh, 0), 0.0)
                sh *= 2
            cum_ref[...] = cum

            att = None
            m = c
            while m >= 2:
                half = m // 2
                arow = half if reverse else half - 1
                if m >= 8:
                    pieces = [jnp.broadcast_to(cum_ref[b * m + arow:b * m + arow + 1, :], (m, GLA_DK))
                              for b in range(c // m)]
                else:
                    pieces = []
                    for t in range(c // 8):
                        tile = None
                        for b in range(8 // m - 1, -1, -1):
                            r = t * 8 + b * m + arow
                            cand = jnp.broadcast_to(cum_ref[r:r + 1, :], (8, GLA_DK))
                            tile = cand if tile is None else jnp.where(sub8 < (b + 1) * m, cand, tile)
                        pieces.append(tile)
                anchor = pieces[0] if len(pieces) == 1 else jnp.concatenate(pieces, axis=0)
                fac = jnp.exp(-jnp.abs(cum - anchor))
                p = _nt_dot((q * fac).astype(BF16), (k * fac).astype(BF16))
                att = p if att is None else jnp.where(pair_xor < m, p, att)
                m //= 2
            att = jnp.where(ri == ci, jnp.sum(q * k, axis=-1, keepdims=True), att)
            att = jnp.where((ri <= ci) if reverse else (ri >= ci), att, 0.0)

            last = 0 if reverse else c - 1
            total = cum_ref[last:last + 1, :]
            q_in = (q * jnp.exp(cum)).astype(BF16)
            k_out = (k * jnp.exp(total - cum)).astype(BF16)
            st = st_ref[...]
            o = (jnp.dot(att.astype(BF16), v, preferred_element_type=F32)
                 + _nt_dot(q_in, st.astype(BF16)))
            if reverse:
                o_ref[rows, :] += o
            else:
                o_ref[rows, :] = o
            kv = lax.dot_general(v, k_out, (((0,), (0,)), ((), ())), preferred_element_type=F32)
            st_ref[...] = st * jnp.exp(total) + kv

        if emit_state:
            st_out[d][0, 0] = st_ref[...].T


def _gla_scan(y, w2pad, b_gk, init_states, seq_len, blk0, n_seq, emit_state):
    has_init = init_states is not None
    qk = lambda off: pl.BlockSpec((seq_len, GLA_DK), lambda b, h: (b + blk0, h + off))
    state = pl.BlockSpec((1, 1, GLA_DK, GLA_DV), lambda b, h: (b, h, 0, 0))
    in_specs = [
        qk(0), qk(GLA_KEY_DIM // GLA_DK),
        pl.BlockSpec((seq_len, GLA_DV), lambda b, h: (b + blk0, h + 2 * GLA_KEY_DIM // GLA_DV)),
        pl.BlockSpec((seq_len, 128), lambda b, h: (b + blk0, (2 * GLA_KEY_DIM + 2 * GLA_VAL_DIM) // 128)),
        pl.BlockSpec((2, 128, GLA_DK), lambda b, h: (0, 0, h)),
        pl.BlockSpec((2, 1, GLA_DK), lambda b, h: (0, 0, h)),
    ]
    args = [y, y, y, y, w2pad, b_gk]
    if has_init:
        in_specs += [state, state]
        args += list(init_states)
    out_specs = [pl.BlockSpec((seq_len, GLA_DV), lambda b, h: (b, h))]
    out_shape = [jax.ShapeDtypeStruct((n_seq * seq_len, GLA_VAL_DIM), F32)]
    if emit_state:
        out_specs += [state, state]
        out_shape += [jax.ShapeDtypeStruct((n_seq, GLA_HEADS, GLA_DK, GLA_DV), F32)] * 2
    return pl.pallas_call(
        functools.partial(_gla_scan_kernel, seq_len=seq_len, has_init=has_init, emit_state=emit_state),
        grid=(n_seq, GLA_HEADS),
        in_specs=in_specs,
        out_specs=out_specs,
        out_shape=out_shape,
        scratch_shapes=[pltpu.VMEM((seq_len, GLA_DK), F32), pltpu.VMEM((GLA_CHUNK, GLA_DK), F32),
                        pltpu.VMEM((GLA_DV, GLA_DK), F32)],
        compiler_params=_cparams(("arbitrary", "arbitrary")),
        name="gla_scan",
    )(*args)


_GROUP_COND_ROW = np.array([0] * N_CTX_GROUPS + list(range(1, DEC_BATCH + 1)))


def kernel(x_prompt, x_sample, c, cache_attn_k, cache_attn_v, state_gla_fwd, state_gla_bwd, c_ctx,
           mod_w, mod_b, norm1_g, norm2_g,
           attn_w_qkv, attn_w_o, attn_q_norm, attn_k_norm, attn_rpb,
           conv_w_pw1, conv_b_pw1, conv_w_dw, conv_b_dw, conv_ln_g, conv_ln_b, conv_w_pw2, conv_b_pw2,
           gla_w_q, gla_w_k, gla_w_v, gla_w_g, gla_w_gk1, gla_w_gk2, gla_b_gk, gla_o_norm, gla_w_o,
           ffn_w_up, ffn_b_up, ffn_w_dw, ffn_b_dw, ffn_w_down, ffn_b_down):
    x = jnp.concatenate([x_prompt.reshape(N_CTX_TOK, D_MODEL), x_sample.reshape(N_LAT_TOK, D_MODEL)], axis=0)
    cond = jnp.concatenate([c_ctx[None, :], c, jnp.zeros((N_COND - 1 - DEC_BATCH, D_MODEL), F32)], axis=0)
    mod_all = _modulation(cond, mod_w, mod_b)
    mod_all = mod_all[:, _GROUP_COND_ROW][:, :, None, :]

    cache_k = cache_attn_k.reshape(DEC_BATCH, -1, PAST_LEN, D_MODEL)
    cache_v = cache_attn_v.reshape(DEC_BATCH, -1, PAST_LEN, D_MODEL)
    ffn_w_up_bf16 = ffn_w_up.astype(BF16)
    ffn_w_down_bf16 = ffn_w_down.astype(BF16)
    new_k, new_v, new_sf, new_sb = [], [], [], []
    for i in range(DEPTH):
        kind, j = i % N_MIXERS, i // N_MIXERS
        mod = mod_all[i]
        n1 = norm1_g[i].reshape(1, D_MODEL)
        if kind == 0:
            w_qkv = attn_w_qkv[j].astype(BF16)
            gain = jnp.concatenate([jnp.tile(attn_q_norm[j], NA_HEADS) * (NA_HEAD_DIM ** -0.5),
                                    jnp.tile(attn_k_norm[j], NA_HEADS),
                                    jnp.ones((D_MODEL,), F32)]).reshape(1, 3 * D_MODEL)
            qp, kp, vp = _qkv_proj(x, n1, mod, w_qkv, gain, 0, N_CTX_GROUPS, True)
            qs, ks, vs = _qkv_proj(x, n1, mod, w_qkv, gain, N_CTX_GROUPS, N_GROUPS - N_CTX_GROUPS, False)
            o_ctx = _ctx_attention(qp, kp, vp)
            o_lat = _na_attention(qs, ks, vs, cache_k, cache_v, j, _na_bias_tables(attn_rpb[j]))
            x = _out_proj(x, o_ctx, o_lat, attn_w_o[j].astype(BF16), mod)
            new_k.append(kp)
            new_v.append(vp)
        elif kind == 1:
            u = _glu_proj(x, n1, mod, conv_w_pw1[j].astype(BF16), conv_b_pw1[j])
            x = _conv_tail(x, u, conv_w_dw[j], conv_b_dw[j], conv_ln_g[j], conv_ln_b[j],
                           conv_w_pw2[j].astype(BF16), conv_b_pw2[j], mod)
        else:
            pad = GLA_PROJ_N - 2 * GLA_KEY_DIM - 2 * GLA_VAL_DIM - 2 * GLA_GATE_RANK
            w_cat = jnp.concatenate([gla_w_q[j], gla_w_k[j], gla_w_v[j], gla_w_g[j], gla_w_gk1[j, 0],
                                     gla_w_gk1[j, 1], jnp.zeros((D_MODEL, pad), F32)], axis=1).astype(BF16)
            colscale = jnp.concatenate([jnp.full((GLA_KEY_DIM,), GLA_DK ** -0.5, F32),
                                        jnp.ones((GLA_PROJ_N - GLA_KEY_DIM,), F32)]).reshape(1, GLA_PROJ_N)
            y = _scaled_proj(x, n1, mod, w_cat, colscale)
            w2pad = jnp.zeros((2, 128, GLA_KEY_DIM), F32)
            w2pad = w2pad.at[0, :GLA_GATE_RANK].set(gla_w_gk2[j, 0])
            w2pad = w2pad.at[1, GLA_GATE_RANK:2 * GLA_GATE_RANK].set(gla_w_gk2[j, 1])
            b_gk = gla_b_gk[j].reshape(2, 1, GLA_KEY_DIM)
            o_ctx, sf, sb = _gla_scan(y, w2pad, b_gk, None, SEQ, 0, BATCH, True)
            (o_lat,) = _gla_scan(y, w2pad, b_gk, (state_gla_fwd[:, j], state_gla_bwd[:, j]),
                                 DEC_SEQ, N_CTX_GROUPS, DEC_BATCH, False)
            x = _gla_out_proj(x, o_ctx, o_lat, y, gla_o_norm[j].reshape(1, GLA_DV),
                              gla_w_o[j].astype(BF16), mod)
            new_sf.append(sf)
            new_sb.append(sb)
        x = _ffn(x, norm2_g[i].reshape(1, D_MODEL), mod, i, ffn_w_up_bf16, ffn_b_up[i],
                 ffn_w_dw[i], ffn_b_dw[i], ffn_w_down_bf16, ffn_b_down[i])

    y_prompt = x[:N_CTX_TOK].reshape(BATCH, SEQ, D_MODEL)
    y_sample = x[N_CTX_TOK:].reshape(DEC_BATCH, DEC_SEQ, D_MODEL)
    def cache_layout(parts):
        stacked = jnp.stack(parts, axis=1).reshape(BATCH, len(parts), NA_HEADS, NA_HEAD_DIM, SEQ)
        return stacked.transpose(0, 1, 4, 2, 3)

    return (y_prompt, y_sample, cache_layout(new_k), cache_layout(new_v),
            jnp.stack(new_sf, axis=1), jnp.stack(new_sb, axis=1))
```

```python
import functools

import numpy as np
import jax
import jax.numpy as jnp
from jax import lax
from jax.experimental import pallas as pl
from jax.experimental.pallas import tpu as pltpu

F32 = jnp.float32
BF16 = jnp.bfloat16

D_MODEL = 1024
BATCH = 16
SEQ = 256
DEPTH = 4
DEC_BATCH = 4
DEC_SEQ = 1024
PAST_LEN = 512
GRID_W = 64
N_MIXERS = 3
NA_HEADS = 16
NA_HEAD_DIM = 64
NA_WIN_R = 8
NA_WIN_C = 16
CONV_WIDTH = 31
GLA_HEADS = 4
GLA_KEY_DIM = 512
GLA_VAL_DIM = 1024
GLA_DK = 128
GLA_DV = 256
GLA_GATE_RANK = 16
GLA_GATE_NORM = 16.0
D_FF = 2816
NORM_EPS = 1e-6
NEG_INF = -1e30

TM = 1024
N_CTX_TOK = BATCH * SEQ
N_LAT_TOK = DEC_BATCH * DEC_SEQ
N_TOK = N_CTX_TOK + N_LAT_TOK
N_CTX_GROUPS = N_CTX_TOK // TM
N_GROUPS = N_TOK // TM
N_COND = 8
TF = 256
PROJ_SLAB = 256
FFN_SLAB = SEQ
FFN_HALO = 8
GLA_CHUNK = 256
GLA_PROJ_N = 3328
VMEM_LIMIT = 60 * 1024 * 1024


def _cparams(sem, flags=None):
    return pltpu.CompilerParams(dimension_semantics=sem, vmem_limit_bytes=VMEM_LIMIT, flags=flags)


def _sigmoid(x):
    return 1.0 / (1.0 + jnp.exp(-x))


def _norm_mod(x, g, shift, scale):
    ms = jnp.mean(x * x, axis=-1, keepdims=True)
    y = x * lax.rsqrt(ms + NORM_EPS) * g
    return y * (1.0 + scale) + shift


def _mod_slice(mod_ref, idx):
    return mod_ref[0, :, idx * D_MODEL:(idx + 1) * D_MODEL]


def _mod_kernel(cond_ref, w_ref, b_ref, o_ref):
    c = cond_ref[...]
    s = (c * _sigmoid(c)).astype(BF16)
    o_ref[0] = jnp.dot(s, w_ref[0].astype(BF16), preferred_element_type=F32) + b_ref[0]


def _modulation(cond, mod_w, mod_b):
    tn = 1536
    n = 6 * D_MODEL
    return pl.pallas_call(
        _mod_kernel,
        grid=(DEPTH, n // tn),
        in_specs=[
            pl.BlockSpec((N_COND, D_MODEL), lambda l, j: (0, 0)),
            pl.BlockSpec((1, D_MODEL, tn), lambda l, j: (l, 0, j)),
            pl.BlockSpec((1, 1, tn), lambda l, j: (l, 0, j)),
        ],
        out_specs=pl.BlockSpec((1, N_COND, tn), lambda l, j: (l, 0, j)),
        out_shape=jax.ShapeDtypeStruct((DEPTH, N_COND, n), F32),
        compiler_params=_cparams(("arbitrary", "arbitrary")),
        name="modulation",
    )(cond, mod_w, mod_b.reshape(DEPTH, 1, n))


def _proj_slabs(x_ref, g_ref, mod_ref, h_ref, epilogue):
    def run(first):
        for s in range(TM // PROJ_SLAB):
            rows = slice(s * PROJ_SLAB, (s + 1) * PROJ_SLAB)
            if first:
                h = _norm_mod(x_ref[rows, :], g_ref[...], _mod_slice(mod_ref, 0), _mod_slice(mod_ref, 1))
                h_ref[rows, :] = h.astype(BF16)
            epilogue(rows, h_ref[rows, :])

    pl.when(pl.program_id(1) == 0)(lambda: run(True))
    pl.when(pl.program_id(1) != 0)(lambda: run(False))


def _head_rms(acc, gain, hsum_ref, hexp_ref):
    ms = jnp.dot((acc * acc).astype(BF16), hsum_ref[...], preferred_element_type=F32)
    inv = lax.rsqrt(ms + NORM_EPS)
    hi = inv.astype(BF16)
    lo = (inv - hi.astype(F32)).astype(BF16)
    inv_full = jnp.dot(jnp.concatenate([hi, lo], axis=1), hexp_ref[...], preferred_element_type=F32)
    return acc * inv_full * gain


def _qkv_kernel(x_ref, g_ref, mod_ref, w_ref, gain_ref, hsum_ref, hexp_ref,
                q_ref, k_ref, v_ref, h_ref, t_ref, *, kv_channel_major):
    j = pl.program_id(1)

    def column_step(out_ref, normed, first, channel_major):
        for s in range(TM // PROJ_SLAB):
            rows = slice(s * PROJ_SLAB, (s + 1) * PROJ_SLAB)
            if first:
                h = _norm_mod(x_ref[rows, :], g_ref[...], _mod_slice(mod_ref, 0), _mod_slice(mod_ref, 1))
                h_ref[rows, :] = h.astype(BF16)
            acc = jnp.dot(h_ref[rows, :], w_ref[...], preferred_element_type=F32)
            if normed:
                acc = _head_rms(acc, gain_ref[...], hsum_ref, hexp_ref)
            if channel_major:
                t_ref[...] = acc
                out_ref[s] = t_ref[...].T
            else:
                out_ref[rows, :] = acc.astype(out_ref.dtype)

    pl.when(j == 0)(lambda: column_step(q_ref, True, True, False))
    pl.when(j == 1)(lambda: column_step(k_ref, True, False, kv_channel_major))
    pl.when(j == 2)(lambda: column_step(v_ref, False, False, kv_channel_major))


def _head_matrices():
    lane = np.arange(D_MODEL)
    hsum = np.zeros((D_MODEL, 128), np.float32)
    hsum[lane, lane // NA_HEAD_DIM] = 1.0 / NA_HEAD_DIM
    hexp = np.zeros((128, D_MODEL), np.float32)
    hexp[lane // NA_HEAD_DIM, lane] = 1.0
    return jnp.asarray(hsum, BF16), jnp.asarray(np.concatenate([hexp, hexp], 0), BF16)


def _qkv_proj(x, norm_g, mod, w_qkv, gain, x_group0, mod_group0, n_groups, kv_channel_major):
    hsum, hexp = _head_matrices()
    ntok = n_groups * TM
    row = lambda g, j: (g, 0)
    tok_spec = pl.BlockSpec((TM, D_MODEL), row)
    if kv_channel_major:
        kv_spec = pl.BlockSpec((TM // SEQ, D_MODEL, SEQ), lambda g, j: (g, 0, 0))
        kv_shape = jax.ShapeDtypeStruct((ntok // SEQ, D_MODEL, SEQ), F32)
    else:
        kv_spec = tok_spec
        kv_shape = jax.ShapeDtypeStruct((ntok, D_MODEL), BF16)
    return pl.pallas_call(
        functools.partial(_qkv_kernel, kv_channel_major=kv_channel_major),
        grid=(n_groups, 3),
        in_specs=[
            pl.BlockSpec((TM, D_MODEL), lambda g, j: (g + x_group0, 0)),
            pl.BlockSpec((1, D_MODEL), lambda g, j: (0, 0)),
            pl.BlockSpec((1, 1, 6 * D_MODEL), lambda g, j: (g + mod_group0, 0, 0)),
            pl.BlockSpec((D_MODEL, D_MODEL), lambda g, j: (0, j)),
            pl.BlockSpec((1, D_MODEL), lambda g, j: (0, j)),
            pl.BlockSpec((D_MODEL, 128), lambda g, j: (0, 0)),
            pl.BlockSpec((256, D_MODEL), lambda g, j: (0, 0)),
        ],
        out_specs=[tok_spec, kv_spec, kv_spec],
        out_shape=[jax.ShapeDtypeStruct((ntok, D_MODEL), BF16), kv_shape, kv_shape],
        scratch_shapes=[pltpu.VMEM((TM, D_MODEL), BF16), pltpu.VMEM((PROJ_SLAB, D_MODEL), F32)],
        compiler_params=_cparams(("arbitrary", "arbitrary")),
        name="qkv_proj",
    )(x, norm_g, mod, w_qkv, gain, hsum, hexp)


def _glu_kernel(x_ref, g_ref, mod_ref, wa_ref, wg_ref, ba_ref, bg_ref, u_ref, h_ref):
    def epilogue(rows, h):
        a = jnp.dot(h, wa_ref[...], preferred_element_type=F32) + ba_ref[...]
        g = jnp.dot(h, wg_ref[...], preferred_element_type=F32) + bg_ref[...]
        u_ref[rows, :] = a * _sigmoid(g)

    _proj_slabs(x_ref, g_ref, mod_ref, h_ref, epilogue)


def _glu_proj(x, norm_g, mod, w_pw1, b_pw1):
    tn = 512
    nj = D_MODEL // tn
    b = b_pw1.reshape(1, 2 * D_MODEL)
    return pl.pallas_call(
        _glu_kernel,
        grid=(N_GROUPS, nj),
        in_specs=[
            pl.BlockSpec((TM, D_MODEL), lambda g, j: (g, 0)),
            pl.BlockSpec((1, D_MODEL), lambda g, j: (0, 0)),
            pl.BlockSpec((1, 1, 6 * D_MODEL), lambda g, j: (g, 0, 0)),
            pl.BlockSpec((D_MODEL, tn), lambda g, j: (0, j)),
            pl.BlockSpec((D_MODEL, tn), lambda g, j: (0, j + nj)),
            pl.BlockSpec((1, tn), lambda g, j: (0, j)),
            pl.BlockSpec((1, tn), lambda g, j: (0, j + nj)),
        ],
        out_specs=pl.BlockSpec((TM, tn), lambda g, j: (g, j)),
        out_shape=jax.ShapeDtypeStruct((N_TOK, D_MODEL), F32),
        scratch_shapes=[pltpu.VMEM((TM, D_MODEL), BF16)],
        compiler_params=_cparams(("arbitrary", "arbitrary")),
        name="glu_proj",
    )(x, norm_g, mod, w_pw1, w_pw1, b, b)


def _scaled_proj_kernel(x_ref, g_ref, mod_ref, w_ref, cs_ref, y_ref, h_ref):
    def epilogue(rows, h):
        y_ref[rows, :] = jnp.dot(h, w_ref[...], preferred_element_type=F32) * cs_ref[...]

    _proj_slabs(x_ref, g_ref, mod_ref, h_ref, epilogue)


def _scaled_proj(x, norm_g, mod, w, colscale):
    n = w.shape[1]
    tn = n // 2
    return pl.pallas_call(
        _scaled_proj_kernel,
        grid=(N_GROUPS, n // tn),
        in_specs=[
            pl.BlockSpec((TM, D_MODEL), lambda g, j: (g, 0)),
            pl.BlockSpec((1, D_MODEL), lambda g, j: (0, 0)),
            pl.BlockSpec((1, 1, 6 * D_MODEL), lambda g, j: (g, 0, 0)),
            pl.BlockSpec((D_MODEL, tn), lambda g, j: (0, j)),
            pl.BlockSpec((1, tn), lambda g, j: (0, j)),
        ],
        out_specs=pl.BlockSpec((TM, tn), lambda g, j: (g, j)),
        out_shape=jax.ShapeDtypeStruct((N_TOK, n), F32),
        scratch_shapes=[pltpu.VMEM((TM, D_MODEL), BF16)],
        compiler_params=_cparams(("arbitrary", "arbitrary")),
        name="gla_proj",
    )(x, norm_g, mod, w, colscale)


_CTX_PART = pl.BlockSpec((TM, D_MODEL), lambda g: (jnp.minimum(g, N_CTX_GROUPS - 1), 0))
_LAT_PART = pl.BlockSpec((TM, D_MODEL), lambda g: (jnp.maximum(g - N_CTX_GROUPS, 0), 0))


def _group_part(ctx_ref, lat_ref, rows):
    return jnp.where(pl.program_id(0) < N_CTX_GROUPS, ctx_ref[rows, :], lat_ref[rows, :])


def _row_slabs():
    return [slice(s * PROJ_SLAB, (s + 1) * PROJ_SLAB) for s in range(TM // PROJ_SLAB)]


def _oproj_kernel(x_ctx_ref, x_lat_ref, a_ctx_ref, a_lat_ref, w_ref, mod_ref, o_ref):
    for rows in _row_slabs():
        r = jnp.dot(_group_part(a_ctx_ref, a_lat_ref, rows), w_ref[...], preferred_element_type=F32)
        o_ref[rows, :] = _group_part(x_ctx_ref, x_lat_ref, rows) + _mod_slice(mod_ref, 2) * r


def _out_proj(x_ctx, x_lat, a_ctx, a_lat, w, mod):
    if x_lat is x_ctx:
        x_lat_spec = pl.BlockSpec((TM, D_MODEL), lambda g: (jnp.maximum(g, N_CTX_GROUPS), 0))
    else:
        x_lat_spec = _LAT_PART
    return pl.pallas_call(
        _oproj_kernel,
        grid=(N_GROUPS,),
        in_specs=[
            _CTX_PART, x_lat_spec, _CTX_PART, _LAT_PART,
            pl.BlockSpec((D_MODEL, D_MODEL), lambda g: (0, 0)),
            pl.BlockSpec((1, 1, 6 * D_MODEL), lambda g: (g, 0, 0)),
        ],
        out_specs=pl.BlockSpec((TM, D_MODEL), lambda g: (g, 0)),
        out_shape=jax.ShapeDtypeStruct((N_TOK, D_MODEL), F32),
        compiler_params=_cparams(("arbitrary",)),
        name="out_proj",
    )(x_ctx, x_lat, a_ctx, a_lat, w, mod)


def _gla_oproj_kernel(x_ref, o_ctx_ref, o_lat_ref, gz_ref, og_ref, w_ref, mod_ref, out_ref):
    og = og_ref[...]
    for rows in _row_slabs():
        o = _group_part(o_ctx_ref, o_lat_ref, rows)
        parts = []
        for h in range(GLA_HEADS):
            oh = o[:, h * GLA_DV:(h + 1) * GLA_DV]
            ms = jnp.mean(oh * oh, axis=-1, keepdims=True)
            parts.append(oh * lax.rsqrt(ms + NORM_EPS) * og)
        y = jnp.concatenate(parts, axis=1)
        gz = gz_ref[rows, :]
        a = (y * (gz * _sigmoid(gz))).astype(BF16)
        r = jnp.dot(a, w_ref[...], preferred_element_type=F32)
        out_ref[rows, :] = x_ref[rows, :] + _mod_slice(mod_ref, 2) * r


def _gla_out_proj(x, o_ctx, o_lat, y, o_norm, w, mod):
    return pl.pallas_call(
        _gla_oproj_kernel,
        grid=(N_GROUPS,),
        in_specs=[
            pl.BlockSpec((TM, D_MODEL), lambda g: (g, 0)),
            _CTX_PART, _LAT_PART,
            pl.BlockSpec((TM, GLA_VAL_DIM), lambda g: (g, 2)),
            pl.BlockSpec((1, GLA_DV), lambda g: (0, 0)),
            pl.BlockSpec((GLA_VAL_DIM, D_MODEL), lambda g: (0, 0)),
            pl.BlockSpec((1, 1, 6 * D_MODEL), lambda g: (g, 0, 0)),
        ],
        out_specs=pl.BlockSpec((TM, D_MODEL), lambda g: (g, 0)),
        out_shape=jax.ShapeDtypeStruct((N_TOK, D_MODEL), F32),
        compiler_params=_cparams(("arbitrary",)),
        name="gla_out_proj",
    )(x, o_ctx, o_lat, y, o_norm, w, mod)


def _ffn_kernel(x_ref, g_ref, mod_ref, wup_ref, bup_ref, wdw_ref, bdw_ref, wdn_ref, bdn_ref,
                out_ref, h_ref, acc_ref, ua0_ref, ug0_ref, ua1_ref, ug1_ref, act0_ref, act1_ref,
                *, group0):
    is_lat = pl.program_id(0) + group0 >= N_CTX_GROUPS
    up_bufs = ((ua0_ref, ug0_ref), (ua1_ref, ug1_ref))
    act_bufs = (act0_ref, act1_ref)
    n_slabs = TM // FFN_SLAB
    win = FFN_SLAB + 2 * FFN_HALO

    def chunk_cols(c):
        return pl.multiple_of(c * TF, TF), pl.multiple_of(D_FF + c * TF, TF)

    def up_slab(c, slot, s):
        h = h_ref[s * FFN_SLAB:(s + 1) * FFN_SLAB, :]
        for buf, col in zip(up_bufs[slot], chunk_cols(c)):
            res = jnp.dot(h, wup_ref[:, pl.ds(col, TF)], preferred_element_type=F32)
            pad = jnp.broadcast_to(-bup_ref[:, pl.ds(col, TF)], (FFN_HALO, TF))
            base = s * win
            buf[base + FFN_HALO:base + FFN_HALO + FFN_SLAB, :] = res
            if s == 0:
                buf[0:FFN_HALO, :] = pad
            else:
                buf[base - FFN_HALO:base, :] = jnp.where(is_lat, res[:FFN_HALO], pad)
            if s == n_slabs - 1:
                buf[base + win - FFN_HALO:base + win, :] = pad
            else:
                buf[base + win:base + win + FFN_HALO, :] = jnp.where(is_lat, res[FFN_SLAB - FFN_HALO:], pad)

    def conv_slab(buf, col, s):
        r0 = s * win + FFN_HALO
        prev = buf[r0 - 1:r0 - 1 + FFN_SLAB, :]
        mid = buf[r0:r0 + FFN_SLAB, :]
        nxt = buf[r0 + 1:r0 + 1 + FFN_SLAB, :]
        w = wdw_ref[:, pl.ds(col, TF)]
        bias = bdw_ref[:, pl.ds(col, TF)] + bup_ref[:, pl.ds(col, TF)] * (w[0:1] + w[1:2] + w[2:3])
        return w[0:1] * prev + w[1:2] * mid + w[2:3] * nxt + bias

    def gate_slab(c, slot, s):
        ca, cg = chunk_cols(c)
        a = conv_slab(up_bufs[slot][0], ca, s)
        g = conv_slab(up_bufs[slot][1], cg, s)
        act_bufs[slot][s * FFN_SLAB:(s + 1) * FFN_SLAB, :] = (g * _sigmoid(g) * a).astype(BF16)

    def down_slab(c, slot, s):
        rows = slice(s * FFN_SLAB, (s + 1) * FFN_SLAB)
        acc_ref[rows, :] += jnp.dot(act_bufs[slot][rows, :], wdn_ref[pl.ds(chunk_cols(c)[0], TF), :],
                                    preferred_element_type=F32)

    n_chunks = D_FF // TF
    acc_ref[...] = jnp.zeros_like(acc_ref)
    for s in range(n_slabs):
        rows = slice(s * FFN_SLAB, (s + 1) * FFN_SLAB)
        h = _norm_mod(x_ref[rows, :], g_ref[...], _mod_slice(mod_ref, 3), _mod_slice(mod_ref, 4))
        h_ref[rows, :] = h.astype(BF16)
        up_slab(0, 0, s)

    def body(i, carry):
        for slot in range(2):
            c = 2 * i + slot
            for s in range(n_slabs):
                up_slab(c + 1, 1 - slot, s)
                gate_slab(c, slot, s)
                down_slab(c, slot, s)
        return carry

    lax.fori_loop(0, (n_chunks - 1) // 2, body, 0)
    for s in range(n_slabs):
        rows = slice(s * FFN_SLAB, (s + 1) * FFN_SLAB)
        gate_slab(n_chunks - 1, 0, s)
        down_slab(n_chunks - 1, 0, s)
        out_ref[rows, :] = x_ref[rows, :] + _mod_slice(mod_ref, 5) * (acc_ref[rows, :] + bdn_ref[...])


def _ffn(x, norm_g, mod, layer, w_up, b_up, w_dw, b_dw, w_down, b_down, group0=0, n_groups=N_GROUPS):
    const = lambda g: (0, 0)
    layer_block = lambda g: (layer, 0, 0)
    resident = dict(pipeline_mode=pl.Buffered(1))
    return pl.pallas_call(
        functools.partial(_ffn_kernel, group0=group0),
        grid=(n_groups,),
        in_specs=[
            pl.BlockSpec((TM, D_MODEL), lambda g: (g + group0, 0)),
            pl.BlockSpec((1, D_MODEL), const),
            pl.BlockSpec((1, 1, 6 * D_MODEL), lambda g: (g + group0, 0, 0)),
            pl.BlockSpec((None, D_MODEL, 2 * D_FF), layer_block, **resident),
            pl.BlockSpec((1, 2 * D_FF), const),
            pl.BlockSpec((3, 2 * D_FF), const),
            pl.BlockSpec((1, 2 * D_FF), const),
            pl.BlockSpec((None, D_FF, D_MODEL), layer_block, **resident),
            pl.BlockSpec((1, D_MODEL), const),
        ],
        out_specs=pl.BlockSpec((TM, D_MODEL), lambda g: (g, 0)),
        out_shape=jax.ShapeDtypeStruct((n_groups * TM, D_MODEL), F32),
        scratch_shapes=[pltpu.VMEM((TM, D_MODEL), BF16), pltpu.VMEM((TM, D_MODEL), F32)]
        + [pltpu.VMEM((TM // FFN_SLAB * (FFN_SLAB + 2 * FFN_HALO), TF), F32)] * 4
        + [pltpu.VMEM((TM, TF), BF16)] * 2,
        compiler_params=_cparams(("arbitrary",)),
        name="conv_ffn",
    )(x, norm_g, mod, w_up, b_up.reshape(1, -1), w_dw, b_dw.reshape(1, -1), w_down, b_down.reshape(1, -1))


def _pair_queries(q):
    lane = lax.broadcasted_iota(jnp.int32, q.shape, 1)
    zero = jnp.zeros_like(q)
    return jnp.concatenate([jnp.where(lane < NA_HEAD_DIM, q, zero),
                            jnp.where(lane < NA_HEAD_DIM, zero, q)], axis=0)


def _pair_merge(o2):
    n = o2.shape[0] // 2
    lane = lax.broadcasted_iota(jnp.int32, (n, o2.shape[1]), 1)
    return jnp.where(lane < NA_HEAD_DIM, o2[:n], o2[n:])


def _nt_dot(a, b):
    return lax.dot_general(a, b, (((1,), (1,)), ((), ())), preferred_element_type=F32)


def _ctx_attn_kernel(q_ref, kt_ref, vt_ref, o_ref):
    for b in range(TM // SEQ):
        rows = slice(b * SEQ, (b + 1) * SEQ)
        q2 = _pair_queries(q_ref[rows, :])
        s = jnp.dot(q2, kt_ref[b].astype(BF16), preferred_element_type=F32)
        p = jnp.exp(s - jnp.max(s, axis=-1, keepdims=True))
        l = jnp.sum(p, axis=-1, keepdims=True)
        o2 = _nt_dot(p.astype(BF16), vt_ref[b].astype(BF16)) / l
        o_ref[rows, :] = _pair_merge(o2).astype(o_ref.dtype)


def _ctx_attention(q, kt, vt):
    spec = pl.BlockSpec((TM, 128), lambda g, hp: (g, hp))
    kv_spec = pl.BlockSpec((TM // SEQ, 128, SEQ), lambda g, hp: (g, hp, 0))
    return pl.pallas_call(
        _ctx_attn_kernel,
        grid=(N_CTX_GROUPS, NA_HEADS // 2),
        in_specs=[spec, kv_spec, kv_spec],
        out_specs=spec,
        out_shape=jax.ShapeDtypeStruct((N_CTX_TOK, D_MODEL), BF16),
        compiler_params=_cparams(("arbitrary", "arbitrary")),
        name="ctx_attention",
    )(q, kt, vt)


_NA_BLOCKS = ((0, 8, 0), (0, 12, 1), (4, 12, 1), (8, 8, 2))


def _na_attn_kernel(q_ref, k_ref, v_ref, ck_ref, cv_ref, ba_ref, bm_ref, bc_ref, o_ref):
    bias_refs = (ba_ref, bm_ref, bc_ref)
    ck = ck_ref[0, 0].astype(BF16)
    cv = cv_ref[0, 0].astype(BF16)
    for blk, (row0, nrows, bidx) in enumerate(_NA_BLOCKS):
        rows = slice(blk * 4 * GRID_W, (blk + 1) * 4 * GRID_W)
        keys = slice(row0 * GRID_W, (row0 + nrows) * GRID_W)
        q2 = _pair_queries(q_ref[rows, :])
        bias = bias_refs[bidx][...]
        s_loc = _nt_dot(q2, k_ref[keys, :]) + bias.reshape(2 * 4 * GRID_W, nrows * GRID_W)
        s_ctx = _nt_dot(q2, ck)
        m = jnp.maximum(jnp.max(s_loc, axis=-1, keepdims=True), jnp.max(s_ctx, axis=-1, keepdims=True))
        p_loc = jnp.exp(s_loc - m)
        p_ctx = jnp.exp(s_ctx - m)
        l = jnp.sum(p_loc, axis=-1, keepdims=True) + jnp.sum(p_ctx, axis=-1, keepdims=True)
        o2 = (jnp.dot(p_loc.astype(BF16), v_ref[keys, :], preferred_element_type=F32)
              + jnp.dot(p_ctx.astype(BF16), cv, preferred_element_type=F32)) / l
        o_ref[rows, :] = _pair_merge(o2).astype(o_ref.dtype)


N_RPB_R = 2 * NA_WIN_R - 1
N_RPB_C = 2 * NA_WIN_C - 1


def _na_bias_kernel(rpb_ref, ba_ref, bm_ref, bc_ref):
    base = pl.program_id(0) * (N_RPB_R * N_RPB_C)
    qc = lax.broadcasted_iota(jnp.int32, (GRID_W, 2 * GRID_W), 0)
    lane = lax.broadcasted_iota(jnp.int32, (GRID_W, 2 * GRID_W), 1)
    kc = lane & (GRID_W - 1)
    right = lane >= GRID_W
    dcol = kc - qc + (NA_WIN_C - 1)
    c_start = jnp.clip(qc - NA_WIN_C // 2, 0, GRID_W - NA_WIN_C)
    in_win = (kc >= c_start) & (kc < c_start + NA_WIN_C)
    neg = jnp.full((GRID_W, 2 * GRID_W), NEG_INF, F32)
    row_tiles, tiles = {}, {}

    def row_tile(dr):
        if dr not in row_tiles:
            acc = jnp.zeros((GRID_W, 2 * GRID_W), F32)
            if 0 <= dr < N_RPB_R:
                for j in range(N_RPB_C):
                    acc = jnp.where(dcol == j, rpb_ref[base + dr * N_RPB_C + j], acc)
            row_tiles[dr] = acc
        return row_tiles[dr]

    def pair_tile(dr):
        if dr not in tiles:
            tiles[dr] = jnp.where(right, row_tile(dr + 1), row_tile(dr))
        return tiles[dr]

    rows_total = DEC_SEQ // GRID_W
    for ref, blk in ((ba_ref, 0), (bm_ref, 1), (bc_ref, 3)):
        row0, nrows, _ = _NA_BLOCKS[blk]
        for rr in range(4):
            r = blk * 4 + rr
            r_start = min(max(r - NA_WIN_R // 2, 0), rows_total - NA_WIN_R)
            for ip in range(nrows // 2):
                krow = row0 + 2 * ip
                ok_l = r_start <= krow < r_start + NA_WIN_R
                ok_r = r_start <= krow + 1 < r_start + NA_WIN_R
                if ok_l or ok_r:
                    mask = in_win
                    if not ok_l:
                        mask = mask & right
                    if not ok_r:
                        mask = mask & jnp.logical_not(right)
                    tile = jnp.where(mask, pair_tile(krow - r + NA_WIN_R - 1), neg)
                else:
                    tile = neg
                ref[0, rr * GRID_W:(rr + 1) * GRID_W, ip * 2 * GRID_W:(ip + 1) * 2 * GRID_W] = tile


def _na_bias_tables(rpb):
    out = lambda n: pl.BlockSpec((1, 4 * GRID_W, n), lambda h: (h, 0, 0))
    shape = lambda n: jax.ShapeDtypeStruct((NA_HEADS, 4 * GRID_W, n), F32)
    return pl.pallas_call(
        _na_bias_kernel,
        grid=(NA_HEADS,),
        in_specs=[pl.BlockSpec(memory_space=pltpu.SMEM)],
        out_specs=[out(8 * GRID_W), out(12 * GRID_W), out(8 * GRID_W)],
        out_shape=[shape(8 * GRID_W), shape(12 * GRID_W), shape(8 * GRID_W)],
        compiler_params=_cparams(("arbitrary",)),
        name="na_bias",
    )(rpb.astype(F32).reshape(-1))


def _na_attention(q, k, v, cache_k, cache_v, layer_j, bias_tables):
    ba, bm, bc = bias_tables
    tok = pl.BlockSpec((TM, 128), lambda hp, b: (b, hp))
    cache = pl.BlockSpec((1, 1, PAST_LEN, 128), lambda hp, b: (b, layer_j, 0, hp))
    bias = lambda n: pl.BlockSpec((2, 4 * GRID_W, n), lambda hp, b: (hp, 0, 0))
    return pl.pallas_call(
        _na_attn_kernel,
        grid=(NA_HEADS // 2, DEC_BATCH),
        in_specs=[tok, tok, tok, cache, cache, bias(8 * GRID_W), bias(12 * GRID_W), bias(8 * GRID_W)],
        out_specs=tok,
        out_shape=jax.ShapeDtypeStruct((N_LAT_TOK, D_MODEL), BF16),
        compiler_params=_cparams(("arbitrary", "arbitrary")),
        name="na_attention",
    )(q, k, v, cache_k, cache_v, ba, bm, bc)


CONV_PAD = 16
CONV_ROWS = 128
CONV_LANES = 128


def _conv_tail_kernel(x_ref, u_ref, wdw_ref, bdw_ref, lng_ref, lnb_ref, w_ref, b_ref, mod_ref,
                      o_ref, pad_ref, a_ref):
    half = CONV_WIDTH // 2
    win = CONV_ROWS + 8
    zeros = jnp.zeros((CONV_PAD, D_MODEL), F32)

    def conv_sequence(base, seq_len):
        pad_ref[0:CONV_PAD, :] = zeros
        pad_ref[CONV_PAD + seq_len:2 * CONV_PAD + seq_len, :] = zeros
        pad_ref[CONV_PAD:CONV_PAD + seq_len, :] = u_ref[base:base + seq_len, :]

        def rows(i, carry):
            r0 = pl.multiple_of(i * CONV_ROWS, CONV_ROWS)
            strips = []
            for l0 in range(0, D_MODEL, CONV_LANES):
                lanes = slice(l0, l0 + CONV_LANES)
                acc = jnp.zeros((CONV_ROWS, CONV_LANES), F32) + bdw_ref[:, lanes]
                for b in range(8):
                    z = None
                    for a in range(-2, 2):
                        t = 8 * a + b + half
                        if not 0 <= t < CONV_WIDTH:
                            continue
                        term = wdw_ref[t:t + 1, lanes] * pad_ref[pl.ds(r0 + (CONV_PAD + 8 * a), win), lanes]
                        z = term if z is None else z + term
                    if b:
                        z = pltpu.roll(z, win - b, 0)
                    acc = acc + z[:CONV_ROWS]
                strips.append(acc)
            acc = jnp.concatenate(strips, axis=1)
            mu = jnp.mean(acc, axis=-1, keepdims=True)
            cen = acc - mu
            var = jnp.mean(cen * cen, axis=-1, keepdims=True)
            y = cen * lax.rsqrt(var + NORM_EPS) * lng_ref[...] + lnb_ref[...]
            a_ref[pl.ds(base + r0, CONV_ROWS), :] = (y * _sigmoid(y)).astype(BF16)
            return carry

        lax.fori_loop(0, seq_len // CONV_ROWS, rows, 0)

    @pl.when(pl.program_id(0) < N_CTX_GROUPS)
    def _():
        for s in range(TM // SEQ):
            conv_sequence(s * SEQ, SEQ)

    @pl.when(pl.program_id(0) >= N_CTX_GROUPS)
    def _():
        conv_sequence(0, DEC_SEQ)

    for rows in _row_slabs():
        r = jnp.dot(a_ref[rows, :], w_ref[...], preferred_element_type=F32) + b_ref[...]
        o_ref[rows, :] = x_ref[rows, :] + _mod_slice(mod_ref, 2) * r


def _conv_tail(x, u, w_dw, b_dw, ln_g, ln_b, w_pw2, b_pw2, mod):
    tok = pl.BlockSpec((TM, D_MODEL), lambda g: (g, 0))
    const = lambda g: (0, 0)
    vec = pl.BlockSpec((1, D_MODEL), const)
    return pl.pallas_call(
        _conv_tail_kernel,
        grid=(N_GROUPS,),
        in_specs=[tok, tok, pl.BlockSpec((CONV_WIDTH, D_MODEL), const), vec, vec, vec,
                  pl.BlockSpec((D_MODEL, D_MODEL), const), vec,
                  pl.BlockSpec((1, 1, 6 * D_MODEL), lambda g: (g, 0, 0))],
        out_specs=tok,
        out_shape=jax.ShapeDtypeStruct((N_TOK, D_MODEL), F32),
        scratch_shapes=[pltpu.VMEM((TM + 2 * CONV_PAD, D_MODEL), F32),
                        pltpu.VMEM((TM, D_MODEL), BF16)],
        compiler_params=_cparams(("arbitrary",)),
        name="conv_tail",
    )(x, u, w_dw, b_dw.reshape(1, -1), ln_g.reshape(1, -1), ln_b.reshape(1, -1), w_pw2,
      b_pw2.reshape(1, -1), mod)


def _log_sigmoid(z):
    return -(jnp.maximum(-z, 0.0) + jnp.log(1.0 + jnp.exp(-jnp.abs(z))))


def _gla_scan_kernel(*refs, seq_len, has_init, emit_state):
    it = iter(refs)
    q_ref, k_ref, v_ref, r_ref, w2_ref, bgk_ref = (next(it) for _ in range(6))
    s0_refs = (next(it), next(it)) if has_init else None
    o_ref = next(it)
    st_out = (next(it), next(it)) if emit_state else None
    g_ref, cum_ref, st_ref = (next(it) for _ in range(3))

    c = GLA_CHUNK
    n_chunks = seq_len // c
    rlow = r_ref[...].astype(BF16)
    row = lax.broadcasted_iota(jnp.int32, (c, 1), 0)
    sub8 = lax.broadcasted_iota(jnp.int32, (8, 1), 0)
    ri = lax.broadcasted_iota(jnp.int32, (c, c), 0)
    ci = lax.broadcasted_iota(jnp.int32, (c, c), 1)
    pair_xor = ri ^ ci

    for d in range(2):
        reverse = d == 1
        z = jnp.dot(rlow, w2_ref[d].astype(BF16), preferred_element_type=F32) + bgk_ref[d]
        g_ref[...] = _log_sigmoid(z) * (1.0 / GLA_GATE_NORM)
        if has_init:
            st_ref[...] = s0_refs[d][0, 0].T
        else:
            st_ref[...] = jnp.zeros_like(st_ref)

        order = range(n_chunks - 1, -1, -1) if reverse else range(n_chunks)
        for ch in order:
            base = ch * c
            rows = slice(base, base + c)
            q = q_ref[rows, :]
            k = k_ref[rows, :]
            v = v_ref[rows, :].astype(BF16)
            cum = g_ref[rows, :]
            sh = 1
            while sh < c:
                if reverse:
                    cum = cum + jnp.where(row < c - sh, pltpu.roll(cum, c - sh, 0), 0.0)
                else:
                    cum = cum + jnp.where(row >= sh, pltpu.roll(cum, sh, 0), 0.0)
                sh *= 2
            cum_ref[...] = cum

            att = None
            m = c
            while m >= 2:
                half = m // 2
                arow = half if reverse else half - 1
                if m >= 8:
                    pieces = [jnp.broadcast_to(cum_ref[b * m + arow:b * m + arow + 1, :], (m, GLA_DK))
                              for b in range(c // m)]
                else:
                    pieces = []
                    for t in range(c // 8):
                        tile = None
                        for b in range(8 // m - 1, -1, -1):
                            r = t * 8 + b * m + arow
                            cand = jnp.broadcast_to(cum_ref[r:r + 1, :], (8, GLA_DK))
                            tile = cand if tile is None else jnp.where(sub8 < (b + 1) * m, cand, tile)
                        pieces.append(tile)
                anchor = pieces[0] if len(pieces) == 1 else jnp.concatenate(pieces, axis=0)
                fac = jnp.exp(-jnp.abs(cum - anchor))
                p = _nt_dot((q * fac).astype(BF16), (k * fac).astype(BF16))
                att = p if att is None else jnp.where(pair_xor < m, p, att)
                m //= 2
            att = jnp.where(ri == ci, jnp.sum(q * k, axis=-1, keepdims=True), att)
            att = jnp.where((ri <= ci) if reverse else (ri >= ci), att, 0.0)

            last = 0 if reverse else c - 1
            total = cum_ref[last:last + 1, :]
            q_in = (q * jnp.exp(cum)).astype(BF16)
            k_out = (k * jnp.exp(total - cum)).astype(BF16)
            st = st_ref[...]
            o = (jnp.dot(att.astype(BF16), v, preferred_element_type=F32)
                 + _nt_dot(q_in, st.astype(BF16)))
            if reverse:
                o_ref[rows, :] += o
            else:
                o_ref[rows, :] = o
            kv = lax.dot_general(v, k_out, (((0,), (0,)), ((), ())), preferred_element_type=F32)
            st_ref[...] = st * jnp.exp(total) + kv

        if emit_state:
            st_out[d][0, 0] = st_ref[...].T


def _gla_scan(y, w2pad, b_gk, init_states, seq_len, blk0, n_seq, emit_state):
    has_init = init_states is not None
    qk = lambda off: pl.BlockSpec((seq_len, GLA_DK), lambda b, h: (b + blk0, h + off))
    state = pl.BlockSpec((1, 1, GLA_DK, GLA_DV), lambda b, h: (b, h, 0, 0))
    in_specs = [
        qk(0), qk(GLA_KEY_DIM // GLA_DK),
        pl.BlockSpec((seq_len, GLA_DV), lambda b, h: (b + blk0, h + 2 * GLA_KEY_DIM // GLA_DV)),
        pl.BlockSpec((seq_len, 128), lambda b, h: (b + blk0, (2 * GLA_KEY_DIM + 2 * GLA_VAL_DIM) // 128)),
        pl.BlockSpec((2, 128, GLA_DK), lambda b, h: (0, 0, h)),
        pl.BlockSpec((2, 1, GLA_DK), lambda b, h: (0, 0, h)),
    ]
    args = [y, y, y, y, w2pad, b_gk]
    if has_init:
        in_specs += [state, state]
        args += list(init_states)
    out_specs = [pl.BlockSpec((seq_len, GLA_DV), lambda b, h: (b, h))]
    out_shape = [jax.ShapeDtypeStruct((n_seq * seq_len, GLA_VAL_DIM), F32)]
    if emit_state:
        out_specs += [state, state]
        out_shape += [jax.ShapeDtypeStruct((n_seq, GLA_HEADS, GLA_DK, GLA_DV), F32)] * 2
    return pl.pallas_call(
        functools.partial(_gla_scan_kernel, seq_len=seq_len, has_init=has_init, emit_state=emit_state),
        grid=(n_seq, GLA_HEADS),
        in_specs=in_specs,
        out_specs=out_specs,
        out_shape=out_shape,
        scratch_shapes=[pltpu.VMEM((seq_len, GLA_DK), F32), pltpu.VMEM((GLA_CHUNK, GLA_DK), F32),
                        pltpu.VMEM((GLA_DV, GLA_DK), F32)],
        compiler_params=_cparams(("arbitrary", "arbitrary")),
        name="gla_scan",
    )(*args)


_GROUP_COND_ROW = np.array([0] * N_CTX_GROUPS + list(range(1, DEC_BATCH + 1)))


def kernel(x_prompt, x_sample, c, cache_attn_k, cache_attn_v, state_gla_fwd, state_gla_bwd, c_ctx,
           mod_w, mod_b, norm1_g, norm2_g,
           attn_w_qkv, attn_w_o, attn_q_norm, attn_k_norm, attn_rpb,
           conv_w_pw1, conv_b_pw1, conv_w_dw, conv_b_dw, conv_ln_g, conv_ln_b, conv_w_pw2, conv_b_pw2,
           gla_w_q, gla_w_k, gla_w_v, gla_w_g, gla_w_gk1, gla_w_gk2, gla_b_gk, gla_o_norm, gla_w_o,
           ffn_w_up, ffn_b_up, ffn_w_dw, ffn_b_dw, ffn_w_down, ffn_b_down):
    x_ctx0 = x_prompt.reshape(N_CTX_TOK, D_MODEL)
    x_lat0 = x_sample.reshape(N_LAT_TOK, D_MODEL)
    x = None
    cond =jnp.concatenate([c_ctx[None, :], c, jnp.zeros((N_COND - 1 - DEC_BATCH, D_MODEL), F32)], axis=0)
    mod_all = _modulation(cond, mod_w, mod_b)
    mod_all = mod_all[:, _GROUP_COND_ROW][:, :, None, :]

    cache_k = cache_attn_k.reshape(DEC_BATCH, -1, PAST_LEN, D_MODEL)
    cache_v = cache_attn_v.reshape(DEC_BATCH, -1, PAST_LEN, D_MODEL)
    ffn_w_up_bf16 = ffn_w_up.astype(BF16)
    ffn_w_down_bf16 = ffn_w_down.astype(BF16)
    new_k, new_v, new_sf, new_sb = [], [], [], []
    for i in range(DEPTH):
        kind, j = i % N_MIXERS, i // N_MIXERS
        mod = mod_all[i]
        n1 = norm1_g[i].reshape(1, D_MODEL)
        if kind == 0:
            w_qkv = attn_w_qkv[j].astype(BF16)
            gain = jnp.concatenate([jnp.tile(attn_q_norm[j], NA_HEADS) * (NA_HEAD_DIM ** -0.5),
                                    jnp.tile(attn_k_norm[j], NA_HEADS),
                                    jnp.ones((D_MODEL,), F32)]).reshape(1, 3 * D_MODEL)
            if x is None:
                x_ctx, x_lat, lat_block0 = x_ctx0, x_lat0, 0
            else:
                x_ctx, x_lat, lat_block0 = x, x, N_CTX_GROUPS
            n_lat_groups = N_GROUPS - N_CTX_GROUPS
            qp, kp, vp = _qkv_proj(x_ctx, n1, mod, w_qkv, gain, 0, 0, N_CTX_GROUPS, True)
            qs, ks, vs = _qkv_proj(x_lat, n1, mod, w_qkv, gain, lat_block0, N_CTX_GROUPS, n_lat_groups, False)
            o_ctx = _ctx_attention(qp, kp, vp)
            o_lat = _na_attention(qs, ks, vs, cache_k, cache_v, j, _na_bias_tables(attn_rpb[j]))
            x = _out_proj(x_ctx, x_lat, o_ctx, o_lat, attn_w_o[j].astype(BF16), mod)
            new_k.append(kp)
            new_v.append(vp)
        elif kind == 1:
            u = _glu_proj(x, n1, mod, conv_w_pw1[j].astype(BF16), conv_b_pw1[j])
            x = _conv_tail(x, u, conv_w_dw[j], conv_b_dw[j], conv_ln_g[j], conv_ln_b[j],
                           conv_w_pw2[j].astype(BF16), conv_b_pw2[j], mod)
        else:
            pad = GLA_PROJ_N - 2 * GLA_KEY_DIM - 2 * GLA_VAL_DIM - 2 * GLA_GATE_RANK
            w_cat = jnp.concatenate([gla_w_q[j], gla_w_k[j], gla_w_v[j], gla_w_g[j], gla_w_gk1[j, 0],
                                     gla_w_gk1[j, 1], jnp.zeros((D_MODEL, pad), F32)], axis=1).astype(BF16)
            colscale = jnp.concatenate([jnp.full((GLA_KEY_DIM,), GLA_DK ** -0.5, F32),
                                        jnp.ones((GLA_PROJ_N - GLA_KEY_DIM,), F32)]).reshape(1, GLA_PROJ_N)
            y = _scaled_proj(x, n1, mod, w_cat, colscale)
            w2pad = jnp.zeros((2, 128, GLA_KEY_DIM), F32)
            w2pad = w2pad.at[0, :GLA_GATE_RANK].set(gla_w_gk2[j, 0])
            w2pad = w2pad.at[1, GLA_GATE_RANK:2 * GLA_GATE_RANK].set(gla_w_gk2[j, 1])
            b_gk = gla_b_gk[j].reshape(2, 1, GLA_KEY_DIM)
            o_ctx, sf, sb = _gla_scan(y, w2pad, b_gk, None, SEQ, 0, BATCH, True)
            (o_lat,) = _gla_scan(y, w2pad, b_gk, (state_gla_fwd[:, j], state_gla_bwd[:, j]),
                                 DEC_SEQ, N_CTX_GROUPS, DEC_BATCH, False)
            x = _gla_out_proj(x, o_ctx, o_lat, y, gla_o_norm[j].reshape(1, GLA_DV),
                              gla_w_o[j].astype(BF16), mod)
            new_sf.append(sf)
            new_sb.append(sb)
        ffn_args = (norm2_g[i].reshape(1, D_MODEL), mod, i, ffn_w_up_bf16, ffn_b_up[i],
                    ffn_w_dw[i], ffn_b_dw[i], ffn_w_down_bf16, ffn_b_down[i])
        if i < DEPTH - 1:
            x = _ffn(x, *ffn_args)
        else:
            y_ctx = _ffn(x, *ffn_args, group0=0, n_groups=N_CTX_GROUPS)
            y_lat = _ffn(x, *ffn_args, group0=N_CTX_GROUPS, n_groups=N_GROUPS - N_CTX_GROUPS)

    y_prompt = y_ctx.reshape(BATCH, SEQ, D_MODEL)
    y_sample = y_lat.reshape(DEC_BATCH, DEC_SEQ, D_MODEL)
    def cache_layout(parts):
        stacked = jnp.stack(parts, axis=1).reshape(BATCH, len(parts), NA_HEADS, NA_HEAD_DIM, SEQ)
        return stacked.transpose(0, 1, 4, 2, 3)

    return (y_prompt, y_sample, cache_layout(new_k), cache_layout(new_v),
            jnp.stack(new_sf, axis=1), jnp.stack(new_sb, axis=1))
```

```python
import functools

import numpy as np
import jax
import jax.numpy as jnp
from jax import lax
from jax.experimental import pallas as pl
from jax.experimental.pallas import tpu as pltpu

F32 = jnp.float32
BF16 = jnp.bfloat16

D_MODEL = 1024
BATCH = 16
SEQ = 256
DEPTH = 4
DEC_BATCH = 4
DEC_SEQ = 1024
PAST_LEN = 512
GRID_W = 64
N_MIXERS = 3
NA_HEADS = 16
NA_HEAD_DIM = 64
NA_WIN_R = 8
NA_WIN_C = 16
CONV_WIDTH = 31
GLA_HEADS = 4
GLA_KEY_DIM = 512
GLA_VAL_DIM = 1024
GLA_DK = 128
GLA_DV = 256
GLA_GATE_RANK = 16
GLA_GATE_NORM = 16.0
D_FF = 2816
NORM_EPS = 1e-6
NEG_INF = -1e30

TM = 1024
N_CTX_TOK = BATCH * SEQ
N_LAT_TOK = DEC_BATCH * DEC_SEQ
N_TOK = N_CTX_TOK + N_LAT_TOK
N_CTX_GROUPS = N_CTX_TOK // TM
N_GROUPS = N_TOK // TM
N_COND = 8
TF = 256
PROJ_SLAB = 256
FFN_SLAB = SEQ
FFN_HALO = 8
FFN_STEP_SLABS = 2
GLA_CHUNK = 256
GLA_PROJ_N = 3328
VMEM_LIMIT = 60 * 1024 * 1024


def _cparams(sem, flags=None):
    return pltpu.CompilerParams(dimension_semantics=sem, vmem_limit_bytes=VMEM_LIMIT, flags=flags)


def _sigmoid(x):
    return 1.0 / (1.0 + jnp.exp(-x))


def _norm_mod(x, g, shift, scale):
    ms = jnp.mean(x * x, axis=-1, keepdims=True)
    y = x * lax.rsqrt(ms + NORM_EPS) * g
    return y * (1.0 + scale) + shift


def _mod_slice(mod_ref, idx):
    return mod_ref[0, :, idx * D_MODEL:(idx + 1) * D_MODEL]


def _mod_kernel(cond_ref, w_ref, b_ref, o_ref):
    c = cond_ref[...]
    s = (c * _sigmoid(c)).astype(BF16)
    o_ref[0] = jnp.dot(s, w_ref[0].astype(BF16), preferred_element_type=F32) + b_ref[0]


def _modulation(cond, mod_w, mod_b):
    tn = 1536
    n = 6 * D_MODEL
    return pl.pallas_call(
        _mod_kernel,
        grid=(DEPTH, n // tn),
        in_specs=[
            pl.BlockSpec((N_COND, D_MODEL), lambda l, j: (0, 0)),
            pl.BlockSpec((1, D_MODEL, tn), lambda l, j: (l, 0, j)),
            pl.BlockSpec((1, 1, tn), lambda l, j: (l, 0, j)),
        ],
        out_specs=pl.BlockSpec((1, N_COND, tn), lambda l, j: (l, 0, j)),
        out_shape=jax.ShapeDtypeStruct((DEPTH, N_COND, n), F32),
        compiler_params=_cparams(("arbitrary", "arbitrary")),
        name="modulation",
    )(cond, mod_w, mod_b.reshape(DEPTH, 1, n))


def _proj_slabs(x_ref, g_ref, mod_ref, h_ref, epilogue):
    def run(first):
        for s in range(TM // PROJ_SLAB):
            rows = slice(s * PROJ_SLAB, (s + 1) * PROJ_SLAB)
            if first:
                h = _norm_mod(x_ref[rows, :], g_ref[...], _mod_slice(mod_ref, 0), _mod_slice(mod_ref, 1))
                h_ref[rows, :] = h.astype(BF16)
            epilogue(rows, h_ref[rows, :])

    pl.when(pl.program_id(1) == 0)(lambda: run(True))
    pl.when(pl.program_id(1) != 0)(lambda: run(False))


def _head_rms(acc, gain, hsum_ref, hexp_ref):
    ms = jnp.dot((acc * acc).astype(BF16), hsum_ref[...], preferred_element_type=F32)
    inv = lax.rsqrt(ms + NORM_EPS)
    hi = inv.astype(BF16)
    lo = (inv - hi.astype(F32)).astype(BF16)
    inv_full = jnp.dot(jnp.concatenate([hi, lo], axis=1), hexp_ref[...], preferred_element_type=F32)
    return acc * inv_full * gain


def _qkv_kernel(x_ref, g_ref, mod_ref, w_ref, gain_ref, hsum_ref, hexp_ref,
                q_ref, k_ref, v_ref, h_ref, t_ref, *, kv_channel_major):
    j = pl.program_id(1)

    def column_step(out_ref, normed, first, channel_major):
        for s in range(TM // PROJ_SLAB):
            rows = slice(s * PROJ_SLAB, (s + 1) * PROJ_SLAB)
            if first:
                h = _norm_mod(x_ref[rows, :], g_ref[...], _mod_slice(mod_ref, 0), _mod_slice(mod_ref, 1))
                h_ref[rows, :] = h.astype(BF16)
            acc = jnp.dot(h_ref[rows, :], w_ref[...], preferred_element_type=F32)
            if normed:
                acc = _head_rms(acc, gain_ref[...], hsum_ref, hexp_ref)
            if channel_major:
                t_ref[...] = acc
                out_ref[s] = t_ref[...].T
            else:
                out_ref[rows, :] = acc.astype(out_ref.dtype)

    pl.when(j == 0)(lambda: column_step(q_ref, True, True, False))
    pl.when(j == 1)(lambda: column_step(k_ref, True, False, kv_channel_major))
    pl.when(j == 2)(lambda: column_step(v_ref, False, False, kv_channel_major))


def _head_matrices():
    lane = np.arange(D_MODEL)
    hsum = np.zeros((D_MODEL, 128), np.float32)
    hsum[lane, lane // NA_HEAD_DIM] = 1.0 / NA_HEAD_DIM
    hexp = np.zeros((128, D_MODEL), np.float32)
    hexp[lane // NA_HEAD_DIM, lane] = 1.0
    return jnp.asarray(hsum, BF16), jnp.asarray(np.concatenate([hexp, hexp], 0), BF16)


def _qkv_proj(x, norm_g, mod, w_qkv, gain, x_group0, mod_group0, n_groups, kv_channel_major):
    hsum, hexp = _head_matrices()
    ntok = n_groups * TM
    row = lambda g, j: (g, 0)
    tok_spec = pl.BlockSpec((TM, D_MODEL), row)
    if kv_channel_major:
        kv_spec = pl.BlockSpec((TM // SEQ, D_MODEL, SEQ), lambda g, j: (g, 0, 0))
        kv_shape = jax.ShapeDtypeStruct((ntok // SEQ, D_MODEL, SEQ), F32)
    else:
        kv_spec = tok_spec
        kv_shape = jax.ShapeDtypeStruct((ntok, D_MODEL), BF16)
    return pl.pallas_call(
        functools.partial(_qkv_kernel, kv_channel_major=kv_channel_major),
        grid=(n_groups, 3),
        in_specs=[
            pl.BlockSpec((TM, D_MODEL), lambda g, j: (g + x_group0, 0)),
            pl.BlockSpec((1, D_MODEL), lambda g, j: (0, 0)),
            pl.BlockSpec((1, 1, 6 * D_MODEL), lambda g, j: (g + mod_group0, 0, 0)),
            pl.BlockSpec((D_MODEL, D_MODEL), lambda g, j: (0, j)),
            pl.BlockSpec((1, D_MODEL), lambda g, j: (0, j)),
            pl.BlockSpec((D_MODEL, 128), lambda g, j: (0, 0)),
            pl.BlockSpec((256, D_MODEL), lambda g, j: (0, 0)),
        ],
        out_specs=[tok_spec, kv_spec, kv_spec],
        out_shape=[jax.ShapeDtypeStruct((ntok, D_MODEL), BF16), kv_shape, kv_shape],
        scratch_shapes=[pltpu.VMEM((TM, D_MODEL), BF16), pltpu.VMEM((PROJ_SLAB, D_MODEL), F32)],
        compiler_params=_cparams(("arbitrary", "arbitrary")),
        name="qkv_proj",
    )(x, norm_g, mod, w_qkv, gain, hsum, hexp)


def _glu_kernel(x_ref, g_ref, mod_ref, wa_ref, wg_ref, ba_ref, bg_ref, u_ref, h_ref):
    def epilogue(rows, h):
        a = jnp.dot(h, wa_ref[...], preferred_element_type=F32) + ba_ref[...]
        g = jnp.dot(h, wg_ref[...], preferred_element_type=F32) + bg_ref[...]
        u_ref[rows, :] = a * _sigmoid(g)

    _proj_slabs(x_ref, g_ref, mod_ref, h_ref, epilogue)


def _glu_proj(x, norm_g, mod, w_pw1, b_pw1):
    tn = 512
    nj = D_MODEL // tn
    b = b_pw1.reshape(1, 2 * D_MODEL)
    return pl.pallas_call(
        _glu_kernel,
        grid=(N_GROUPS, nj),
        in_specs=[
            pl.BlockSpec((TM, D_MODEL), lambda g, j: (g, 0)),
            pl.BlockSpec((1, D_MODEL), lambda g, j: (0, 0)),
            pl.BlockSpec((1, 1, 6 * D_MODEL), lambda g, j: (g, 0, 0)),
            pl.BlockSpec((D_MODEL, tn), lambda g, j: (0, j)),
            pl.BlockSpec((D_MODEL, tn), lambda g, j: (0, j + nj)),
            pl.BlockSpec((1, tn), lambda g, j: (0, j)),
            pl.BlockSpec((1, tn), lambda g, j: (0, j + nj)),
        ],
        out_specs=pl.BlockSpec((TM, tn), lambda g, j: (g, j)),
        out_shape=jax.ShapeDtypeStruct((N_TOK, D_MODEL), F32),
        scratch_shapes=[pltpu.VMEM((TM, D_MODEL), BF16)],
        compiler_params=_cparams(("arbitrary", "arbitrary")),
        name="glu_proj",
    )(x, norm_g, mod, w_pw1, w_pw1, b, b)


def _scaled_proj_kernel(x_ref, g_ref, mod_ref, w_ref, cs_ref, y_ref, h_ref):
    def epilogue(rows, h):
        y_ref[rows, :] = jnp.dot(h, w_ref[...], preferred_element_type=F32) * cs_ref[...]

    _proj_slabs(x_ref, g_ref, mod_ref, h_ref, epilogue)


def _scaled_proj(x, norm_g, mod, w, colscale):
    n = w.shape[1]
    tn = n // 2
    return pl.pallas_call(
        _scaled_proj_kernel,
        grid=(N_GROUPS, n // tn),
        in_specs=[
            pl.BlockSpec((TM, D_MODEL), lambda g, j: (g, 0)),
            pl.BlockSpec((1, D_MODEL), lambda g, j: (0, 0)),
            pl.BlockSpec((1, 1, 6 * D_MODEL), lambda g, j: (g, 0, 0)),
            pl.BlockSpec((D_MODEL, tn), lambda g, j: (0, j)),
            pl.BlockSpec((1, tn), lambda g, j: (0, j)),
        ],
        out_specs=pl.BlockSpec((TM, tn), lambda g, j: (g, j)),
        out_shape=jax.ShapeDtypeStruct((N_TOK, n), F32),
        scratch_shapes=[pltpu.VMEM((TM, D_MODEL), BF16)],
        compiler_params=_cparams(("arbitrary", "arbitrary")),
        name="gla_proj",
    )(x, norm_g, mod, w, colscale)


_CTX_PART = pl.BlockSpec((TM, D_MODEL), lambda g: (jnp.minimum(g, N_CTX_GROUPS - 1), 0))
_LAT_PART = pl.BlockSpec((TM, D_MODEL), lambda g: (jnp.maximum(g - N_CTX_GROUPS, 0), 0))


def _group_part(ctx_ref, lat_ref, rows):
    return jnp.where(pl.program_id(0) < N_CTX_GROUPS, ctx_ref[rows, :], lat_ref[rows, :])


def _row_slabs():
    return [slice(s * PROJ_SLAB, (s + 1) * PROJ_SLAB) for s in range(TM // PROJ_SLAB)]


def _oproj_kernel(x_ctx_ref, x_lat_ref, a_ctx_ref, a_lat_ref, w_ref, mod_ref, o_ref):
    for rows in _row_slabs():
        r = jnp.dot(_group_part(a_ctx_ref, a_lat_ref, rows), w_ref[...], preferred_element_type=F32)
        o_ref[rows, :] = _group_part(x_ctx_ref, x_lat_ref, rows) + _mod_slice(mod_ref, 2) * r


def _out_proj(x_ctx, x_lat, a_ctx, a_lat, w, mod):
    if x_lat is x_ctx:
        x_lat_spec = pl.BlockSpec((TM, D_MODEL), lambda g: (jnp.maximum(g, N_CTX_GROUPS), 0))
    else:
        x_lat_spec = _LAT_PART
    return pl.pallas_call(
        _oproj_kernel,
        grid=(N_GROUPS,),
        in_specs=[
            _CTX_PART, x_lat_spec, _CTX_PART, _LAT_PART,
            pl.BlockSpec((D_MODEL, D_MODEL), lambda g: (0, 0)),
            pl.BlockSpec((1, 1, 6 * D_MODEL), lambda g: (g, 0, 0)),
        ],
        out_specs=pl.BlockSpec((TM, D_MODEL), lambda g: (g, 0)),
        out_shape=jax.ShapeDtypeStruct((N_TOK, D_MODEL), F32),
        compiler_params=_cparams(("arbitrary",)),
        name="out_proj",
    )(x_ctx, x_lat, a_ctx, a_lat, w, mod)


def _gla_oproj_kernel(x_ref, o_ctx_ref, o_lat_ref, gz_ref, og_ref, w_ref, mod_ref, out_ref):
    og = og_ref[...]
    for rows in _row_slabs():
        o = _group_part(o_ctx_ref, o_lat_ref, rows)
        parts = []
        for h in range(GLA_HEADS):
            oh = o[:, h * GLA_DV:(h + 1) * GLA_DV]
            ms = jnp.mean(oh * oh, axis=-1, keepdims=True)
            parts.append(oh * lax.rsqrt(ms + NORM_EPS) * og)
        y = jnp.concatenate(parts, axis=1)
        gz = gz_ref[rows, :]
        a = (y * (gz * _sigmoid(gz))).astype(BF16)
        r = jnp.dot(a, w_ref[...], preferred_element_type=F32)
        out_ref[rows, :] = x_ref[rows, :] + _mod_slice(mod_ref, 2) * r


def _gla_out_proj(x, o_ctx, o_lat, y, o_norm, w, mod):
    return pl.pallas_call(
        _gla_oproj_kernel,
        grid=(N_GROUPS,),
        in_specs=[
            pl.BlockSpec((TM, D_MODEL), lambda g: (g, 0)),
            _CTX_PART, _LAT_PART,
            pl.BlockSpec((TM, GLA_VAL_DIM), lambda g: (g, 2)),
            pl.BlockSpec((1, GLA_DV), lambda g: (0, 0)),
            pl.BlockSpec((GLA_VAL_DIM, D_MODEL), lambda g: (0, 0)),
            pl.BlockSpec((1, 1, 6 * D_MODEL), lambda g: (g, 0, 0)),
        ],
        out_specs=pl.BlockSpec((TM, D_MODEL), lambda g: (g, 0)),
        out_shape=jax.ShapeDtypeStruct((N_TOK, D_MODEL), F32),
        compiler_params=_cparams(("arbitrary",)),
        name="gla_out_proj",
    )(x, o_ctx, o_lat, y, o_norm, w, mod)


def _ffn_kernel(x_ref, g_ref, mod_ref, wup_ref, bup_ref, wdw_ref, bdw_ref, wdn_ref, bdn_ref,
                out_ref, h_ref, acc_ref, ua0_ref, ug0_ref, ua1_ref, ug1_ref, act0_ref, act1_ref,
                *, group0):
    is_lat = pl.program_id(0) + group0 >= N_CTX_GROUPS
    up_bufs = ((ua0_ref, ug0_ref), (ua1_ref, ug1_ref))
    act_bufs = (act0_ref, act1_ref)
    n_slabs = TM // FFN_SLAB
    win = FFN_SLAB + 2 * FFN_HALO

    def chunk_cols(c):
        return pl.multiple_of(c * TF, TF), pl.multiple_of(D_FF + c * TF, TF)

    def step_rows(s0):
        return slice(s0 * FFN_SLAB, (s0 + FFN_STEP_SLABS) * FFN_SLAB)

    def up_step(c, slot, s0):
        h = h_ref[step_rows(s0), :]
        for buf, col in zip(up_bufs[slot], chunk_cols(c)):
            both = jnp.dot(h, wup_ref[:, pl.ds(col, TF)], preferred_element_type=F32)
            pad = jnp.broadcast_to(-bup_ref[:, pl.ds(col, TF)], (FFN_HALO, TF))
            for u in range(FFN_STEP_SLABS):
                s = s0 + u
                res = both[u * FFN_SLAB:(u + 1) * FFN_SLAB]
                base = s * win
                buf[base + FFN_HALO:base + FFN_HALO + FFN_SLAB, :] = res
                if s == 0:
                    buf[0:FFN_HALO, :] = pad
                else:
                    buf[base - FFN_HALO:base, :] = jnp.where(is_lat, res[:FFN_HALO], pad)
                if s == n_slabs - 1:
                    buf[base + win - FFN_HALO:base + win, :] = pad
                else:
                    buf[base + win:base + win + FFN_HALO, :] = jnp.where(
                        is_lat, res[FFN_SLAB - FFN_HALO:], pad)

    def conv_slab(buf, col, s):
        r0 = s * win + FFN_HALO
        prev = buf[r0 - 1:r0 - 1 + FFN_SLAB, :]
        mid = buf[r0:r0 + FFN_SLAB, :]
        nxt = buf[r0 + 1:r0 + 1 + FFN_SLAB, :]
        w = wdw_ref[:, pl.ds(col, TF)]
        bias = bdw_ref[:, pl.ds(col, TF)] + bup_ref[:, pl.ds(col, TF)] * (w[0:1] + w[1:2] + w[2:3])
        return w[0:1] * prev + w[1:2] * mid + w[2:3] * nxt + bias

    def gate_slab(c, slot, s):
        ca, cg = chunk_cols(c)
        a = conv_slab(up_bufs[slot][0], ca, s)
        g = conv_slab(up_bufs[slot][1], cg, s)
        act_bufs[slot][s * FFN_SLAB:(s + 1) * FFN_SLAB, :] = (g * _sigmoid(g) * a).astype(BF16)

    def gate_down_step(c, slot, s0):
        for u in range(FFN_STEP_SLABS):
            gate_slab(c, slot, s0 + u)
        rows = step_rows(s0)
        acc_ref[rows, :] += jnp.dot(act_bufs[slot][rows, :], wdn_ref[pl.ds(chunk_cols(c)[0], TF), :],
                                    preferred_element_type=F32)

    steps = range(0, n_slabs, FFN_STEP_SLABS)

    n_chunks = D_FF // TF
    acc_ref[...] = jnp.zeros_like(acc_ref)
    for s0 in steps:
        rows = step_rows(s0)
        h = _norm_mod(x_ref[rows, :], g_ref[...], _mod_slice(mod_ref, 3), _mod_slice(mod_ref, 4))
        h_ref[rows, :] = h.astype(BF16)
        up_step(0, 0, s0)

    def body(i, carry):
        for slot in range(2):
            c = 2 * i + slot
            for s0 in steps:
                up_step(c + 1, 1 - slot, s0)
                gate_down_step(c, slot, s0)
        return carry

    lax.fori_loop(0, (n_chunks - 1) // 2, body, 0)
    for s0 in steps:
        rows = step_rows(s0)
        gate_down_step(n_chunks - 1, 0, s0)
        out_ref[rows, :] = x_ref[rows, :] + _mod_slice(mod_ref, 5) * (acc_ref[rows, :] + bdn_ref[...])


def _ffn(x, norm_g, mod, layer, w_up, b_up, w_dw, b_dw, w_down, b_down, group0=0, n_groups=N_GROUPS):
    const = lambda g: (0, 0)
    layer_block = lambda g: (layer, 0, 0)
    resident = dict(pipeline_mode=pl.Buffered(1))
    return pl.pallas_call(
        functools.partial(_ffn_kernel, group0=group0),
        grid=(n_groups,),
        in_specs=[
            pl.BlockSpec((TM, D_MODEL), lambda g: (g + group0, 0)),
            pl.BlockSpec((1, D_MODEL), const),
            pl.BlockSpec((1, 1, 6 * D_MODEL), lambda g: (g + group0, 0, 0)),
            pl.BlockSpec((None, D_MODEL, 2 * D_FF), layer_block, **resident),
            pl.BlockSpec((1, 2 * D_FF), const),
            pl.BlockSpec((3, 2 * D_FF), const),
            pl.BlockSpec((1, 2 * D_FF), const),
            pl.BlockSpec((None, D_FF, D_MODEL), layer_block, **resident),
            pl.BlockSpec((1, D_MODEL), const),
        ],
        out_specs=pl.BlockSpec((TM, D_MODEL), lambda g: (g, 0)),
        out_shape=jax.ShapeDtypeStruct((n_groups * TM, D_MODEL), F32),
        scratch_shapes=[pltpu.VMEM((TM, D_MODEL), BF16), pltpu.VMEM((TM, D_MODEL), F32)]
        + [pltpu.VMEM((TM // FFN_SLAB * (FFN_SLAB + 2 * FFN_HALO), TF), F32)] * 4
        + [pltpu.VMEM((TM, TF), BF16)] * 2,
        compiler_params=_cparams(("arbitrary",)),
        name="conv_ffn",
    )(x, norm_g, mod, w_up, b_up.reshape(1, -1), w_dw, b_dw.reshape(1, -1), w_down, b_down.reshape(1, -1))


def _pair_queries(q):
    lane = lax.broadcasted_iota(jnp.int32, q.shape, 1)
    zero = jnp.zeros_like(q)
    return jnp.concatenate([jnp.where(lane < NA_HEAD_DIM, q, zero),
                            jnp.where(lane < NA_HEAD_DIM, zero, q)], axis=0)


def _pair_merge(o2):
    n = o2.shape[0] // 2
    lane = lax.broadcasted_iota(jnp.int32, (n, o2.shape[1]), 1)
    return jnp.where(lane < NA_HEAD_DIM, o2[:n], o2[n:])


def _nt_dot(a, b):
    return lax.dot_general(a, b, (((1,), (1,)), ((), ())), preferred_element_type=F32)


def _ctx_attn_kernel(q_ref, kt_ref, vt_ref, o_ref):
    for b in range(TM // SEQ):
        rows = slice(b * SEQ, (b + 1) * SEQ)
        q2 = _pair_queries(q_ref[rows, :])
        s = jnp.dot(q2, kt_ref[b].astype(BF16), preferred_element_type=F32)
        p = jnp.exp(s - jnp.max(s, axis=-1, keepdims=True))
        l = jnp.sum(p, axis=-1, keepdims=True)
        o2 = _nt_dot(p.astype(BF16), vt_ref[b].astype(BF16)) / l
        o_ref[rows, :] = _pair_merge(o2).astype(o_ref.dtype)


def _ctx_attention(q, kt, vt):
    spec = pl.BlockSpec((TM, 128), lambda g, hp: (g, hp))
    kv_spec = pl.BlockSpec((TM // SEQ, 128, SEQ), lambda g, hp: (g, hp, 0))
    return pl.pallas_call(
        _ctx_attn_kernel,
        grid=(N_CTX_GROUPS, NA_HEADS // 2),
        in_specs=[spec, kv_spec, kv_spec],
        out_specs=spec,
        out_shape=jax.ShapeDtypeStruct((N_CTX_TOK, D_MODEL), BF16),
        compiler_params=_cparams(("arbitrary", "arbitrary")),
        name="ctx_attention",
    )(q, kt, vt)


_NA_BLOCKS = ((0, 8, 0), (0, 12, 1), (4, 12, 1), (8, 8, 2))


def _na_attn_kernel(q_ref, k_ref, v_ref, ck_ref, cv_ref, ba_ref, bm_ref, bc_ref, o_ref):
    bias_refs = (ba_ref, bm_ref, bc_ref)
    ck = ck_ref[0, 0].astype(BF16)
    cv = cv_ref[0, 0].astype(BF16)
    for blk, (row0, nrows, bidx) in enumerate(_NA_BLOCKS):
        rows = slice(blk * 4 * GRID_W, (blk + 1) * 4 * GRID_W)
        keys = slice(row0 * GRID_W, (row0 + nrows) * GRID_W)
        q2 = _pair_queries(q_ref[rows, :])
        bias = bias_refs[bidx][...]
        s_loc = _nt_dot(q2, k_ref[keys, :]) + bias.reshape(2 * 4 * GRID_W, nrows * GRID_W)
        s_ctx = _nt_dot(q2, ck)
        m = jnp.maximum(jnp.max(s_loc, axis=-1, keepdims=True), jnp.max(s_ctx, axis=-1, keepdims=True))
        p_loc = jnp.exp(s_loc - m)
        p_ctx = jnp.exp(s_ctx - m)
        l = jnp.sum(p_loc, axis=-1, keepdims=True) + jnp.sum(p_ctx, axis=-1, keepdims=True)
        o2 = (jnp.dot(p_loc.astype(BF16), v_ref[keys, :], preferred_element_type=F32)
              + jnp.dot(p_ctx.astype(BF16), cv, preferred_element_type=F32)) / l
        o_ref[rows, :] = _pair_merge(o2).astype(o_ref.dtype)


N_RPB_R = 2 * NA_WIN_R - 1
N_RPB_C = 2 * NA_WIN_C - 1


def _na_bias_kernel(rpb_ref, ba_ref, bm_ref, bc_ref):
    base = pl.program_id(0) * (N_RPB_R * N_RPB_C)
    qc = lax.broadcasted_iota(jnp.int32, (GRID_W, 2 * GRID_W), 0)
    lane = lax.broadcasted_iota(jnp.int32, (GRID_W, 2 * GRID_W), 1)
    kc = lane & (GRID_W - 1)
    right = lane >= GRID_W
    dcol = kc - qc + (NA_WIN_C - 1)
    c_start = jnp.clip(qc - NA_WIN_C // 2, 0, GRID_W - NA_WIN_C)
    in_win = (kc >= c_start) & (kc < c_start + NA_WIN_C)
    neg = jnp.full((GRID_W, 2 * GRID_W), NEG_INF, F32)
    row_tiles, tiles = {}, {}

    def row_tile(dr):
        if dr not in row_tiles:
            acc = jnp.zeros((GRID_W, 2 * GRID_W), F32)
            if 0 <= dr < N_RPB_R:
                for j in range(N_RPB_C):
                    acc = jnp.where(dcol == j, rpb_ref[base + dr * N_RPB_C + j], acc)
            row_tiles[dr] = acc
        return row_tiles[dr]

    def pair_tile(dr):
        if dr not in tiles:
            tiles[dr] = jnp.where(right, row_tile(dr + 1), row_tile(dr))
        return tiles[dr]

    rows_total = DEC_SEQ // GRID_W
    for ref, blk in ((ba_ref, 0), (bm_ref, 1), (bc_ref, 3)):
        row0, nrows, _ = _NA_BLOCKS[blk]
        for rr in range(4):
            r = blk * 4 + rr
            r_start = min(max(r - NA_WIN_R // 2, 0), rows_total - NA_WIN_R)
            for ip in range(nrows // 2):
                krow = row0 + 2 * ip
                ok_l = r_start <= krow < r_start + NA_WIN_R
                ok_r = r_start <= krow + 1 < r_start + NA_WIN_R
                if ok_l or ok_r:
                    mask = in_win
                    if not ok_l:
                        mask = mask & right
                    if not ok_r:
                        mask = mask & jnp.logical_not(right)
                    tile = jnp.where(mask, pair_tile(krow - r + NA_WIN_R - 1), neg)
                else:
                    tile = neg
                ref[0, rr * GRID_W:(rr + 1) * GRID_W, ip * 2 * GRID_W:(ip + 1) * 2 * GRID_W] = tile


def _na_bias_tables(rpb):
    out = lambda n: pl.BlockSpec((1, 4 * GRID_W, n), lambda h: (h, 0, 0))
    shape = lambda n: jax.ShapeDtypeStruct((NA_HEADS, 4 * GRID_W, n), F32)
    return pl.pallas_call(
        _na_bias_kernel,
        grid=(NA_HEADS,),
        in_specs=[pl.BlockSpec(memory_space=pltpu.SMEM)],
        out_specs=[out(8 * GRID_W), out(12 * GRID_W), out(8 * GRID_W)],
        out_shape=[shape(8 * GRID_W), shape(12 * GRID_W), shape(8 * GRID_W)],
        compiler_params=_cparams(("arbitrary",)),
        name="na_bias",
    )(rpb.astype(F32).reshape(-1))


def _na_attention(q, k, v, cache_k, cache_v, layer_j, bias_tables):
    ba, bm, bc = bias_tables
    tok = pl.BlockSpec((TM, 128), lambda hp, b: (b, hp))
    cache = pl.BlockSpec((1, 1, PAST_LEN, 128), lambda hp, b: (b, layer_j, 0, hp))
    bias = lambda n: pl.BlockSpec((2, 4 * GRID_W, n), lambda hp, b: (hp, 0, 0))
    return pl.pallas_call(
        _na_attn_kernel,
        grid=(NA_HEADS // 2, DEC_BATCH),
        in_specs=[tok, tok, tok, cache, cache, bias(8 * GRID_W), bias(12 * GRID_W), bias(8 * GRID_W)],
        out_specs=tok,
        out_shape=jax.ShapeDtypeStruct((N_LAT_TOK, D_MODEL), BF16),
        compiler_params=_cparams(("arbitrary", "arbitrary")),
        name="na_attention",
    )(q, k, v, cache_k, cache_v, ba, bm, bc)


CONV_PAD = 16
CONV_ROWS = 128
CONV_LANES = 128


def _conv_tail_kernel(x_ref, u_ref, wdw_ref, bdw_ref, lng_ref, lnb_ref, w_ref, b_ref, mod_ref,
                      o_ref, pad_ref, a_ref):
    half = CONV_WIDTH // 2
    win = CONV_ROWS + 8
    zeros = jnp.zeros((CONV_PAD, D_MODEL), F32)

    def conv_sequence(base, seq_len):
        pad_ref[0:CONV_PAD, :] = zeros
        pad_ref[CONV_PAD + seq_len:2 * CONV_PAD + seq_len, :] = zeros
        pad_ref[CONV_PAD:CONV_PAD + seq_len, :] = u_ref[base:base + seq_len, :]

        def rows(i, carry):
            r0 = pl.multiple_of(i * CONV_ROWS, CONV_ROWS)
            strips = []
            for l0 in range(0, D_MODEL, CONV_LANES):
                lanes = slice(l0, l0 + CONV_LANES)
                acc = jnp.zeros((CONV_ROWS, CONV_LANES), F32) + bdw_ref[:, lanes]
                for b in range(8):
                    z = None
                    for a in range(-2, 2):
                        t = 8 * a + b + half
                        if not 0 <= t < CONV_WIDTH:
                            continue
                        term = wdw_ref[t:t + 1, lanes] * pad_ref[pl.ds(r0 + (CONV_PAD + 8 * a), win), lanes]
                        z = term if z is None else z + term
                    if b:
                        z = pltpu.roll(z, win - b, 0)
                    acc = acc + z[:CONV_ROWS]
                strips.append(acc)
            acc = jnp.concatenate(strips, axis=1)
            mu = jnp.mean(acc, axis=-1, keepdims=True)
            cen = acc - mu
            var = jnp.mean(cen * cen, axis=-1, keepdims=True)
            y = cen * lax.rsqrt(var + NORM_EPS) * lng_ref[...] + lnb_ref[...]
            a_ref[pl.ds(base + r0, CONV_ROWS), :] = (y * _sigmoid(y)).astype(BF16)
            return carry

        lax.fori_loop(0, seq_len // CONV_ROWS, rows, 0)

    @pl.when(pl.program_id(0) < N_CTX_GROUPS)
    def _():
        for s in range(TM // SEQ):
            conv_sequence(s * SEQ, SEQ)

    @pl.when(pl.program_id(0) >= N_CTX_GROUPS)
    def _():
        conv_sequence(0, DEC_SEQ)

    for rows in _row_slabs():
        r = jnp.dot(a_ref[rows, :], w_ref[...], preferred_element_type=F32) + b_ref[...]
        o_ref[rows, :] = x_ref[rows, :] + _mod_slice(mod_ref, 2) * r


def _conv_tail(x, u, w_dw, b_dw, ln_g, ln_b, w_pw2, b_pw2, mod):
    tok = pl.BlockSpec((TM, D_MODEL), lambda g: (g, 0))
    const = lambda g: (0, 0)
    vec = pl.BlockSpec((1, D_MODEL), const)
    return pl.pallas_call(
        _conv_tail_kernel,
        grid=(N_GROUPS,),
        in_specs=[tok, tok, pl.BlockSpec((CONV_WIDTH, D_MODEL), const), vec, vec, vec,
                  pl.BlockSpec((D_MODEL, D_MODEL), const), vec,
                  pl.BlockSpec((1, 1, 6 * D_MODEL), lambda g: (g, 0, 0))],
        out_specs=tok,
        out_shape=jax.ShapeDtypeStruct((N_TOK, D_MODEL), F32),
        scratch_shapes=[pltpu.VMEM((TM + 2 * CONV_PAD, D_MODEL), F32),
                        pltpu.VMEM((TM, D_MODEL), BF16)],
        compiler_params=_cparams(("arbitrary",)),
        name="conv_tail",
    )(x, u, w_dw, b_dw.reshape(1, -1), ln_g.reshape(1, -1), ln_b.reshape(1, -1), w_pw2,
      b_pw2.reshape(1, -1), mod)


def _log_sigmoid(z):
    return -(jnp.maximum(-z, 0.0) + jnp.log(1.0 + jnp.exp(-jnp.abs(z))))


def _gla_scan_kernel(*refs, seq_len, has_init, emit_state):
    it = iter(refs)
    q_ref, k_ref, v_ref, r_ref, w2_ref, bgk_ref = (next(it) for _ in range(6))
    s0_refs = (next(it), next(it)) if has_init else None
    o_ref = next(it)
    st_out = (next(it), next(it)) if emit_state else None
    g_ref, cum_ref, st_ref = (next(it) for _ in range(3))

    c = GLA_CHUNK
    n_chunks = seq_len // c
    rlow = r_ref[...].astype(BF16)
    row = lax.broadcasted_iota(jnp.int32, (c, 1), 0)
    sub8 = lax.broadcasted_iota(jnp.int32, (8, 1), 0)
    ri = lax.broadcasted_iota(jnp.int32, (c, c), 0)
    ci = lax.broadcasted_iota(jnp.int32, (c, c), 1)
    pair_xor = ri ^ ci

    for d in range(2):
        reverse = d == 1
        z = jnp.dot(rlow, w2_ref[d].astype(BF16), preferred_element_type=F32) + bgk_ref[d]
        g_ref[...] = _log_sigmoid(z) * (1.0 / GLA_GATE_NORM)
        if has_init:
            st_ref[...] = s0_refs[d][0, 0].T
        else:
            st_ref[...] = jnp.zeros_like(st_ref)

        order = range(n_chunks - 1, -1, -1) if reverse else range(n_chunks)
        for ch in order:
            base = ch * c
            rows = slice(base, base + c)
            q = q_ref[rows, :]
            k = k_ref[rows, :]
            v = v_ref[rows, :].astype(BF16)
            cum = g_ref[rows, :]
            sh = 1
            while sh < c:
                if reverse:
                    cum = cum + jnp.where(row < c - sh, pltpu.roll(cum, c - sh, 0), 0.0)
                else:
                    cum = cum + jnp.where(row >= sh, pltpu.roll(cum, sh, 0), 0.0)
                sh *= 2
            cum_ref[...] = cum

            att = None
            m = c
            while m >= 2:
                half = m // 2
                arow = half if reverse else half - 1
                if m >= 8:
                    pieces = [jnp.broadcast_to(cum_ref[b * m + arow:b * m + arow + 1, :], (m, GLA_DK))
                              for b in range(c // m)]
                else:
                    pieces = []
                    for t in range(c // 8):
                        tile = None
                        for b in range(8 // m - 1, -1, -1):
                            r = t * 8 + b * m + arow
                            cand = jnp.broadcast_to(cum_ref[r:r + 1, :], (8, GLA_DK))
                            tile = cand if tile is None else jnp.where(sub8 < (b + 1) * m, cand, tile)
                        pieces.append(tile)
                anchor = pieces[0] if len(pieces) == 1 else jnp.concatenate(pieces, axis=0)
                fac = jnp.exp(-jnp.abs(cum - anchor))
                p = _nt_dot((q * fac).astype(BF16), (k * fac).astype(BF16))
                att = p if att is None else jnp.where(pair_xor < m, p, att)
                m //= 2
            att = jnp.where(ri == ci, jnp.sum(q * k, axis=-1, keepdims=True), att)
            att = jnp.where((ri <= ci) if reverse else (ri >= ci), att, 0.0)

            last = 0 if reverse else c - 1
            total = cum_ref[last:last + 1, :]
            q_in = (q * jnp.exp(cum)).astype(BF16)
            k_out = (k * jnp.exp(total - cum)).astype(BF16)
            st = st_ref[...]
            o = (jnp.dot(att.astype(BF16), v, preferred_element_type=F32)
                 + _nt_dot(q_in, st.astype(BF16)))
            if reverse:
                o_ref[rows, :] += o
            else:
                o_ref[rows, :] = o
            kv = lax.dot_general(v, k_out, (((0,), (0,)), ((), ())), preferred_element_type=F32)
            st_ref[...] = st * jnp.exp(total) + kv

        if emit_state:
            st_out[d][0, 0] = st_ref[...].T


def _gla_scan(y, w2pad, b_gk, init_states, seq_len, blk0, n_seq, emit_state):
    has_init = init_states is not None
    qk = lambda off: pl.BlockSpec((seq_len, GLA_DK), lambda b, h: (b + blk0, h + off))
    state = pl.BlockSpec((1, 1, GLA_DK, GLA_DV), lambda b, h: (b, h, 0, 0))
    in_specs = [
        qk(0), qk(GLA_KEY_DIM // GLA_DK),
        pl.BlockSpec((seq_len, GLA_DV), lambda b, h: (b + blk0, h + 2 * GLA_KEY_DIM // GLA_DV)),
        pl.BlockSpec((seq_len, 128), lambda b, h: (b + blk0, (2 * GLA_KEY_DIM + 2 * GLA_VAL_DIM) // 128)),
        pl.BlockSpec((2, 128, GLA_DK), lambda b, h: (0, 0, h)),
        pl.BlockSpec((2, 1, GLA_DK), lambda b, h: (0, 0, h)),
    ]
    args = [y, y, y, y, w2pad, b_gk]
    if has_init:
        in_specs += [state, state]
        args += list(init_states)
    out_specs = [pl.BlockSpec((seq_len, GLA_DV), lambda b, h: (b, h))]
    out_shape = [jax.ShapeDtypeStruct((n_seq * seq_len, GLA_VAL_DIM), F32)]
    if emit_state:
        out_specs += [state, state]
        out_shape += [jax.ShapeDtypeStruct((n_seq, GLA_HEADS, GLA_DK, GLA_DV), F32)] * 2
    return pl.pallas_call(
        functools.partial(_gla_scan_kernel, seq_len=seq_len, has_init=has_init, emit_state=emit_state),
        grid=(n_seq, GLA_HEADS),
        in_specs=in_specs,
        out_specs=out_specs,
        out_shape=out_shape,
        scratch_shapes=[pltpu.VMEM((seq_len, GLA_DK), F32), pltpu.VMEM((GLA_CHUNK, GLA_DK), F32),
                        pltpu.VMEM((GLA_DV, GLA_DK), F32)],
        compiler_params=_cparams(("arbitrary", "arbitrary")),
        name="gla_scan",
    )(*args)


_GROUP_COND_ROW = np.array([0] * N_CTX_GROUPS + list(range(1, DEC_BATCH + 1)))


def kernel(x_prompt, x_sample, c, cache_attn_k, cache_attn_v, state_gla_fwd, state_gla_bwd, c_ctx,
           mod_w, mod_b, norm1_g, norm2_g,
           attn_w_qkv, attn_w_o, attn_q_norm, attn_k_norm, attn_rpb,
           conv_w_pw1, conv_b_pw1, conv_w_dw, conv_b_dw, conv_ln_g, conv_ln_b, conv_w_pw2, conv_b_pw2,
           gla_w_q, gla_w_k, gla_w_v, gla_w_g, gla_w_gk1, gla_w_gk2, gla_b_gk, gla_o_norm, gla_w_o,
           ffn_w_up, ffn_b_up, ffn_w_dw, ffn_b_dw, ffn_w_down, ffn_b_down):
    x_ctx0 = x_prompt.reshape(N_CTX_TOK, D_MODEL)
    x_lat0 = x_sample.reshape(N_LAT_TOK, D_MODEL)
    x = None
    cond =jnp.concatenate([c_ctx[None, :], c, jnp.zeros((N_COND - 1 - DEC_BATCH, D_MODEL), F32)], axis=0)
    mod_all = _modulation(cond, mod_w, mod_b)
    mod_all = mod_all[:, _GROUP_COND_ROW][:, :, None, :]

    cache_k = cache_attn_k.reshape(DEC_BATCH, -1, PAST_LEN, D_MODEL)
    cache_v = cache_attn_v.reshape(DEC_BATCH, -1, PAST_LEN, D_MODEL)
    ffn_w_up_bf16 = ffn_w_up.astype(BF16)
    ffn_w_down_bf16 = ffn_w_down.astype(BF16)
    new_k, new_v, new_sf, new_sb = [], [], [], []
    for i in range(DEPTH):
        kind, j = i % N_MIXERS, i // N_MIXERS
        mod = mod_all[i]
        n1 = norm1_g[i].reshape(1, D_MODEL)
        if kind == 0:
            w_qkv = attn_w_qkv[j].astype(BF16)
            gain = jnp.concatenate([jnp.tile(attn_q_norm[j], NA_HEADS) * (NA_HEAD_DIM ** -0.5),
                                    jnp.tile(attn_k_norm[j], NA_HEADS),
                                    jnp.ones((D_MODEL,), F32)]).reshape(1, 3 * D_MODEL)
            if x is None:
                x_ctx, x_lat, lat_block0 = x_ctx0, x_lat0, 0
            else:
                x_ctx, x_lat, lat_block0 = x, x, N_CTX_GROUPS
            n_lat_groups = N_GROUPS - N_CTX_GROUPS
            qp, kp, vp = _qkv_proj(x_ctx, n1, mod, w_qkv, gain, 0, 0, N_CTX_GROUPS, True)
            qs, ks, vs = _qkv_proj(x_lat, n1, mod, w_qkv, gain, lat_block0, N_CTX_GROUPS, n_lat_groups, False)
            o_ctx = _ctx_attention(qp, kp, vp)
            o_lat = _na_attention(qs, ks, vs, cache_k, cache_v, j, _na_bias_tables(attn_rpb[j]))
            x = _out_proj(x_ctx, x_lat, o_ctx, o_lat, attn_w_o[j].astype(BF16), mod)
            new_k.append(kp)
            new_v.append(vp)
        elif kind == 1:
            u = _glu_proj(x, n1, mod, conv_w_pw1[j].astype(BF16), conv_b_pw1[j])
            x = _conv_tail(x, u, conv_w_dw[j], conv_b_dw[j], conv_ln_g[j], conv_ln_b[j],
                           conv_w_pw2[j].astype(BF16), conv_b_pw2[j], mod)
        else:
            pad = GLA_PROJ_N - 2 * GLA_KEY_DIM - 2 * GLA_VAL_DIM - 2 * GLA_GATE_RANK
            w_cat = jnp.concatenate([gla_w_q[j], gla_w_k[j], gla_w_v[j], gla_w_g[j], gla_w_gk1[j, 0],
                                     gla_w_gk1[j, 1], jnp.zeros((D_MODEL, pad), F32)], axis=1).astype(BF16)
            colscale = jnp.concatenate([jnp.full((GLA_KEY_DIM,), GLA_DK ** -0.5, F32),
                                        jnp.ones((GLA_PROJ_N - GLA_KEY_DIM,), F32)]).reshape(1, GLA_PROJ_N)
            y = _scaled_proj(x, n1, mod, w_cat, colscale)
            w2pad = jnp.zeros((2, 128, GLA_KEY_DIM), F32)
            w2pad = w2pad.at[0, :GLA_GATE_RANK].set(gla_w_gk2[j, 0])
            w2pad = w2pad.at[1, GLA_GATE_RANK:2 * GLA_GATE_RANK].set(gla_w_gk2[j, 1])
            b_gk = gla_b_gk[j].reshape(2, 1, GLA_KEY_DIM)
            o_ctx, sf, sb = _gla_scan(y, w2pad, b_gk, None, SEQ, 0, BATCH, True)
            (o_lat,) = _gla_scan(y, w2pad, b_gk, (state_gla_fwd[:, j], state_gla_bwd[:, j]),
                                 DEC_SEQ, N_CTX_GROUPS, DEC_BATCH, False)
            x = _gla_out_proj(x, o_ctx, o_lat, y, gla_o_norm[j].reshape(1, GLA_DV),
                              gla_w_o[j].astype(BF16), mod)
            new_sf.append(sf)
            new_sb.append(sb)
        ffn_args = (norm2_g[i].reshape(1, D_MODEL), mod, i, ffn_w_up_bf16, ffn_b_up[i],
                    ffn_w_dw[i], ffn_b_dw[i], ffn_w_down_bf16, ffn_b_down[i])
        if i < DEPTH - 1:
            x = _ffn(x, *ffn_args)
        else:
            y_ctx = _ffn(x, *ffn_args, group0=0, n_groups=N_CTX_GROUPS)
            y_lat = _ffn(x, *ffn_args, group0=N_CTX_GROUPS, n_groups=N_GROUPS - N_CTX_GROUPS)

    y_prompt = y_ctx.reshape(BATCH, SEQ, D_MODEL)
    y_sample = y_lat.reshape(DEC_BATCH, DEC_SEQ, D_MODEL)
    def cache_layout(parts):
        stacked = jnp.stack(parts, axis=1).reshape(BATCH, len(parts), NA_HEADS, NA_HEAD_DIM, SEQ)
        return stacked.transpose(0, 1, 4, 2, 3)

    return (y_prompt, y_sample, cache_layout(new_k), cache_layout(new_v),
            jnp.stack(new_sf, axis=1), jnp.stack(new_sb, axis=1))
```

```python
import functools

import numpy as np
import jax
import jax.numpy as jnp
from jax import lax
from jax.experimental import pallas as pl
from jax.experimental.pallas import tpu as pltpu

F32 = jnp.float32
BF16 = jnp.bfloat16

D_MODEL = 1024
BATCH = 16
SEQ = 256
DEPTH = 4
DEC_BATCH = 4
DEC_SEQ = 1024
PAST_LEN = 512
GRID_W = 64
N_MIXERS = 3
NA_HEADS = 16
NA_HEAD_DIM = 64
NA_WIN_R = 8
NA_WIN_C = 16
CONV_WIDTH = 31
GLA_HEADS = 4
GLA_KEY_DIM = 512
GLA_VAL_DIM = 1024
GLA_DK = 128
GLA_DV = 256
GLA_GATE_RANK = 16
GLA_GATE_NORM = 16.0
D_FF = 2816
NORM_EPS = 1e-6
NEG_INF = -1e30

TM = 1024
N_CTX_TOK = BATCH * SEQ
N_LAT_TOK = DEC_BATCH * DEC_SEQ
N_TOK = N_CTX_TOK + N_LAT_TOK
N_CTX_GROUPS = N_CTX_TOK // TM
N_GROUPS = N_TOK // TM
N_COND = 8
TF = 256
PROJ_SLAB = 256
FFN_SLAB = SEQ
FFN_HALO = 8
FFN_STEP_SLABS = 4
GLA_CHUNK = 256
GLA_PROJ_N = 3328
VMEM_LIMIT = 60 * 1024 * 1024


def _cparams(sem, flags=None):
    return pltpu.CompilerParams(dimension_semantics=sem, vmem_limit_bytes=VMEM_LIMIT, flags=flags)


def _sigmoid(x):
    return 1.0 / (1.0 + jnp.exp(-x))


def _norm_mod(x, g, shift, scale):
    ms = jnp.mean(x * x, axis=-1, keepdims=True)
    y = x * lax.rsqrt(ms + NORM_EPS) * g
    return y * (1.0 + scale) + shift


def _mod_slice(mod_ref, idx):
    return mod_ref[0, :, idx * D_MODEL:(idx + 1) * D_MODEL]


def _mod_kernel(cond_ref, w_ref, b_ref, o_ref):
    c = cond_ref[...]
    s = (c * _sigmoid(c)).astype(BF16)
    o_ref[0] = jnp.dot(s, w_ref[0].astype(BF16), preferred_element_type=F32) + b_ref[0]


def _modulation(cond, mod_w, mod_b):
    tn = 1536
    n = 6 * D_MODEL
    return pl.pallas_call(
        _mod_kernel,
        grid=(DEPTH, n // tn),
        in_specs=[
            pl.BlockSpec((N_COND, D_MODEL), lambda l, j: (0, 0)),
            pl.BlockSpec((1, D_MODEL, tn), lambda l, j: (l, 0, j)),
            pl.BlockSpec((1, 1, tn), lambda l, j: (l, 0, j)),
        ],
        out_specs=pl.BlockSpec((1, N_COND, tn), lambda l, j: (l, 0, j)),
        out_shape=jax.ShapeDtypeStruct((DEPTH, N_COND, n), F32),
        compiler_params=_cparams(("arbitrary", "arbitrary")),
        name="modulation",
    )(cond, mod_w, mod_b.reshape(DEPTH, 1, n))


def _proj_slabs(x_ref, g_ref, mod_ref, h_ref, epilogue):
    def run(first):
        for s in range(TM // PROJ_SLAB):
            rows = slice(s * PROJ_SLAB, (s + 1) * PROJ_SLAB)
            if first:
                h = _norm_mod(x_ref[rows, :], g_ref[...], _mod_slice(mod_ref, 0), _mod_slice(mod_ref, 1))
                h_ref[rows, :] = h.astype(BF16)
            epilogue(rows, h_ref[rows, :])

    pl.when(pl.program_id(1) == 0)(lambda: run(True))
    pl.when(pl.program_id(1) != 0)(lambda: run(False))


def _head_rms(acc, gain, hsum_ref, hexp_ref):
    ms = jnp.dot((acc * acc).astype(BF16), hsum_ref[...], preferred_element_type=F32)
    inv = lax.rsqrt(ms + NORM_EPS)
    hi = inv.astype(BF16)
    lo = (inv - hi.astype(F32)).astype(BF16)
    inv_full = jnp.dot(jnp.concatenate([hi, lo], axis=1), hexp_ref[...], preferred_element_type=F32)
    return acc * inv_full * gain


def _qkv_kernel(x_ref, g_ref, mod_ref, w_ref, gain_ref, hsum_ref, hexp_ref,
                q_ref, k_ref, v_ref, h_ref, t_ref, *, kv_channel_major):
    j = pl.program_id(1)

    def column_step(out_ref, normed, first, channel_major):
        for s in range(TM // PROJ_SLAB):
            rows = slice(s * PROJ_SLAB, (s + 1) * PROJ_SLAB)
            if first:
                h = _norm_mod(x_ref[rows, :], g_ref[...], _mod_slice(mod_ref, 0), _mod_slice(mod_ref, 1))
                h_ref[rows, :] = h.astype(BF16)
            acc = jnp.dot(h_ref[rows, :], w_ref[...], preferred_element_type=F32)
            if normed:
                acc = _head_rms(acc, gain_ref[...], hsum_ref, hexp_ref)
            if channel_major:
                t_ref[...] = acc
                out_ref[s] = t_ref[...].T
            else:
                out_ref[rows, :] = acc.astype(out_ref.dtype)

    pl.when(j == 0)(lambda: column_step(q_ref, True, True, False))
    pl.when(j == 1)(lambda: column_step(k_ref, True, False, kv_channel_major))
    pl.when(j == 2)(lambda: column_step(v_ref, False, False, kv_channel_major))


def _head_matrices():
    lane = np.arange(D_MODEL)
    hsum = np.zeros((D_MODEL, 128), np.float32)
    hsum[lane, lane // NA_HEAD_DIM] = 1.0 / NA_HEAD_DIM
    hexp = np.zeros((128, D_MODEL), np.float32)
    hexp[lane // NA_HEAD_DIM, lane] = 1.0
    return jnp.asarray(hsum, BF16), jnp.asarray(np.concatenate([hexp, hexp], 0), BF16)


def _qkv_proj(x, norm_g, mod, w_qkv, gain, x_group0, mod_group0, n_groups, kv_channel_major):
    hsum, hexp = _head_matrices()
    ntok = n_groups * TM
    row = lambda g, j: (g, 0)
    tok_spec = pl.BlockSpec((TM, D_MODEL), row)
    if kv_channel_major:
        kv_spec = pl.BlockSpec((TM // SEQ, D_MODEL, SEQ), lambda g, j: (g, 0, 0))
        kv_shape = jax.ShapeDtypeStruct((ntok // SEQ, D_MODEL, SEQ), F32)
    else:
        kv_spec = tok_spec
        kv_shape = jax.ShapeDtypeStruct((ntok, D_MODEL), BF16)
    return pl.pallas_call(
        functools.partial(_qkv_kernel, kv_channel_major=kv_channel_major),
        grid=(n_groups, 3),
        in_specs=[
            pl.BlockSpec((TM, D_MODEL), lambda g, j: (g + x_group0, 0)),
            pl.BlockSpec((1, D_MODEL), lambda g, j: (0, 0)),
            pl.BlockSpec((1, 1, 6 * D_MODEL), lambda g, j: (g + mod_group0, 0, 0)),
            pl.BlockSpec((D_MODEL, D_MODEL), lambda g, j: (0, j)),
            pl.BlockSpec((1, D_MODEL), lambda g, j: (0, j)),
            pl.BlockSpec((D_MODEL, 128), lambda g, j: (0, 0)),
            pl.BlockSpec((256, D_MODEL), lambda g, j: (0, 0)),
        ],
        out_specs=[tok_spec, kv_spec, kv_spec],
        out_shape=[jax.ShapeDtypeStruct((ntok, D_MODEL), BF16), kv_shape, kv_shape],
        scratch_shapes=[pltpu.VMEM((TM, D_MODEL), BF16), pltpu.VMEM((PROJ_SLAB, D_MODEL), F32)],
        compiler_params=_cparams(("arbitrary", "arbitrary")),
        name="qkv_proj",
    )(x, norm_g, mod, w_qkv, gain, hsum, hexp)


def _glu_kernel(x_ref, g_ref, mod_ref, wa_ref, wg_ref, ba_ref, bg_ref, u_ref, h_ref):
    def epilogue(rows, h):
        a = jnp.dot(h, wa_ref[...], preferred_element_type=F32) + ba_ref[...]
        g = jnp.dot(h, wg_ref[...], preferred_element_type=F32) + bg_ref[...]
        u_ref[rows, :] = a * _sigmoid(g)

    _proj_slabs(x_ref, g_ref, mod_ref, h_ref, epilogue)


def _glu_proj(x, norm_g, mod, w_pw1, b_pw1):
    tn = 512
    nj = D_MODEL // tn
    b = b_pw1.reshape(1, 2 * D_MODEL)
    return pl.pallas_call(
        _glu_kernel,
        grid=(N_GROUPS, nj),
        in_specs=[
            pl.BlockSpec((TM, D_MODEL), lambda g, j: (g, 0)),
            pl.BlockSpec((1, D_MODEL), lambda g, j: (0, 0)),
            pl.BlockSpec((1, 1, 6 * D_MODEL), lambda g, j: (g, 0, 0)),
            pl.BlockSpec((D_MODEL, tn), lambda g, j: (0, j)),
            pl.BlockSpec((D_MODEL, tn), lambda g, j: (0, j + nj)),
            pl.BlockSpec((1, tn), lambda g, j: (0, j)),
            pl.BlockSpec((1, tn), lambda g, j: (0, j + nj)),
        ],
        out_specs=pl.BlockSpec((TM, tn), lambda g, j: (g, j)),
        out_shape=jax.ShapeDtypeStruct((N_TOK, D_MODEL), F32),
        scratch_shapes=[pltpu.VMEM((TM, D_MODEL), BF16)],
        compiler_params=_cparams(("arbitrary", "arbitrary")),
        name="glu_proj",
    )(x, norm_g, mod, w_pw1, w_pw1, b, b)


def _scaled_proj_kernel(x_ref, g_ref, mod_ref, w_ref, cs_ref, y_ref, h_ref):
    def epilogue(rows, h):
        y_ref[rows, :] = jnp.dot(h, w_ref[...], preferred_element_type=F32) * cs_ref[...]

    _proj_slabs(x_ref, g_ref, mod_ref, h_ref, epilogue)


def _scaled_proj(x, norm_g, mod, w, colscale):
    n = w.shape[1]
    tn = n // 2
    return pl.pallas_call(
        _scaled_proj_kernel,
        grid=(N_GROUPS, n // tn),
        in_specs=[
            pl.BlockSpec((TM, D_MODEL), lambda g, j: (g, 0)),
            pl.BlockSpec((1, D_MODEL), lambda g, j: (0, 0)),
            pl.BlockSpec((1, 1, 6 * D_MODEL), lambda g, j: (g, 0, 0)),
            pl.BlockSpec((D_MODEL, tn), lambda g, j: (0, j)),
            pl.BlockSpec((1, tn), lambda g, j: (0, j)),
        ],
        out_specs=pl.BlockSpec((TM, tn), lambda g, j: (g, j)),
        out_shape=jax.ShapeDtypeStruct((N_TOK, n), F32),
        scratch_shapes=[pltpu.VMEM((TM, D_MODEL), BF16)],
        compiler_params=_cparams(("arbitrary", "arbitrary")),
        name="gla_proj",
    )(x, norm_g, mod, w, colscale)


_CTX_PART = pl.BlockSpec((TM, D_MODEL), lambda g: (jnp.minimum(g, N_CTX_GROUPS - 1), 0))
_LAT_PART = pl.BlockSpec((TM, D_MODEL), lambda g: (jnp.maximum(g - N_CTX_GROUPS, 0), 0))


def _group_part(ctx_ref, lat_ref, rows):
    return jnp.where(pl.program_id(0) < N_CTX_GROUPS, ctx_ref[rows, :], lat_ref[rows, :])


def _row_slabs():
    return [slice(s * PROJ_SLAB, (s + 1) * PROJ_SLAB) for s in range(TM // PROJ_SLAB)]


def _oproj_kernel(x_ctx_ref, x_lat_ref, a_ctx_ref, a_lat_ref, w_ref, mod_ref, o_ref):
    for rows in _row_slabs():
        r = jnp.dot(_group_part(a_ctx_ref, a_lat_ref, rows), w_ref[...], preferred_element_type=F32)
        o_ref[rows, :] = _group_part(x_ctx_ref, x_lat_ref, rows) + _mod_slice(mod_ref, 2) * r


def _out_proj(x_ctx, x_lat, a_ctx, a_lat, w, mod):
    if x_lat is x_ctx:
        x_lat_spec = pl.BlockSpec((TM, D_MODEL), lambda g: (jnp.maximum(g, N_CTX_GROUPS), 0))
    else:
        x_lat_spec = _LAT_PART
    return pl.pallas_call(
        _oproj_kernel,
        grid=(N_GROUPS,),
        in_specs=[
            _CTX_PART, x_lat_spec, _CTX_PART, _LAT_PART,
            pl.BlockSpec((D_MODEL, D_MODEL), lambda g: (0, 0)),
            pl.BlockSpec((1, 1, 6 * D_MODEL), lambda g: (g, 0, 0)),
        ],
        out_specs=pl.BlockSpec((TM, D_MODEL), lambda g: (g, 0)),
        out_shape=jax.ShapeDtypeStruct((N_TOK, D_MODEL), F32),
        compiler_params=_cparams(("arbitrary",)),
        name="out_proj",
    )(x_ctx, x_lat, a_ctx, a_lat, w, mod)


def _gla_oproj_kernel(x_ref, o_ctx_ref, o_lat_ref, gz_ref, og_ref, w_ref, mod_ref, out_ref):
    og = og_ref[...]
    for rows in _row_slabs():
        o = _group_part(o_ctx_ref, o_lat_ref, rows)
        parts = []
        for h in range(GLA_HEADS):
            oh = o[:, h * GLA_DV:(h + 1) * GLA_DV]
            ms = jnp.mean(oh * oh, axis=-1, keepdims=True)
            parts.append(oh * lax.rsqrt(ms + NORM_EPS) * og)
        y = jnp.concatenate(parts, axis=1)
        gz = gz_ref[rows, :]
        a = (y * (gz * _sigmoid(gz))).astype(BF16)
        r = jnp.dot(a, w_ref[...], preferred_element_type=F32)
        out_ref[rows, :] = x_ref[rows, :] + _mod_slice(mod_ref, 2) * r


def _gla_out_proj(x, o_ctx, o_lat, y, o_norm, w, mod):
    return pl.pallas_call(
        _gla_oproj_kernel,
        grid=(N_GROUPS,),
        in_specs=[
            pl.BlockSpec((TM, D_MODEL), lambda g: (g, 0)),
            _CTX_PART, _LAT_PART,
            pl.BlockSpec((TM, GLA_VAL_DIM), lambda g: (g, 2)),
            pl.BlockSpec((1, GLA_DV), lambda g: (0, 0)),
            pl.BlockSpec((GLA_VAL_DIM, D_MODEL), lambda g: (0, 0)),
            pl.BlockSpec((1, 1, 6 * D_MODEL), lambda g: (g, 0, 0)),
        ],
        out_specs=pl.BlockSpec((TM, D_MODEL), lambda g: (g, 0)),
        out_shape=jax.ShapeDtypeStruct((N_TOK, D_MODEL), F32),
        compiler_params=_cparams(("arbitrary",)),
        name="gla_out_proj",
    )(x, o_ctx, o_lat, y, o_norm, w, mod)


def _ffn_kernel(x_ref, g_ref, mod_ref, wup_ref, bup_ref, wdw_ref, bdw_ref, wdn_ref, bdn_ref,
                out_ref, h_ref, acc_ref, ua0_ref, ug0_ref, ua1_ref, ug1_ref, act0_ref, act1_ref,
                *, group0):
    is_lat = pl.program_id(0) + group0 >= N_CTX_GROUPS
    up_bufs = ((ua0_ref, ug0_ref), (ua1_ref, ug1_ref))
    act_bufs = (act0_ref, act1_ref)
    n_slabs = TM // FFN_SLAB
    win = FFN_SLAB + 2 * FFN_HALO

    def chunk_cols(c):
        return pl.multiple_of(c * TF, TF), pl.multiple_of(D_FF + c * TF, TF)

    def step_rows(s0):
        return slice(s0 * FFN_SLAB, (s0 + FFN_STEP_SLABS) * FFN_SLAB)

    def up_step(c, slot, s0):
        h = h_ref[step_rows(s0), :]
        for buf, col in zip(up_bufs[slot], chunk_cols(c)):
            both = jnp.dot(h, wup_ref[:, pl.ds(col, TF)], preferred_element_type=F32)
            pad = jnp.broadcast_to(-bup_ref[:, pl.ds(col, TF)], (FFN_HALO, TF))
            for u in range(FFN_STEP_SLABS):
                s = s0 + u
                res = both[u * FFN_SLAB:(u + 1) * FFN_SLAB]
                base = s * win
                buf[base + FFN_HALO:base + FFN_HALO + FFN_SLAB, :] = res
                if s == 0:
                    buf[0:FFN_HALO, :] = pad
                else:
                    buf[base - FFN_HALO:base, :] = jnp.where(is_lat, res[:FFN_HALO], pad)
                if s == n_slabs - 1:
                    buf[base + win - FFN_HALO:base + win, :] = pad
                else:
                    buf[base + win:base + win + FFN_HALO, :] = jnp.where(
                        is_lat, res[FFN_SLAB - FFN_HALO:], pad)

    def conv_slab(buf, col, s):
        r0 = s * win + FFN_HALO
        prev = buf[r0 - 1:r0 - 1 + FFN_SLAB, :]
        mid = buf[r0:r0 + FFN_SLAB, :]
        nxt = buf[r0 + 1:r0 + 1 + FFN_SLAB, :]
        w = wdw_ref[:, pl.ds(col, TF)]
        bias = bdw_ref[:, pl.ds(col, TF)] + bup_ref[:, pl.ds(col, TF)] * (w[0:1] + w[1:2] + w[2:3])
        return w[0:1] * prev + w[1:2] * mid + w[2:3] * nxt + bias

    def gate_slab(c, slot, s):
        ca, cg = chunk_cols(c)
        a = conv_slab(up_bufs[slot][0], ca, s)
        g = conv_slab(up_bufs[slot][1], cg, s)
        act_bufs[slot][s * FFN_SLAB:(s + 1) * FFN_SLAB, :] = (g * _sigmoid(g) * a).astype(BF16)

    def gate_down_step(c, slot, s0):
        for u in range(FFN_STEP_SLABS):
            gate_slab(c, slot, s0 + u)
        rows = step_rows(s0)
        acc_ref[rows, :] += jnp.dot(act_bufs[slot][rows, :], wdn_ref[pl.ds(chunk_cols(c)[0], TF), :],
                                    preferred_element_type=F32)

    steps = range(0, n_slabs, FFN_STEP_SLABS)

    n_chunks = D_FF // TF
    acc_ref[...] = jnp.zeros_like(acc_ref)
    for s0 in steps:
        rows = step_rows(s0)
        h = _norm_mod(x_ref[rows, :], g_ref[...], _mod_slice(mod_ref, 3), _mod_slice(mod_ref, 4))
        h_ref[rows, :] = h.astype(BF16)
        up_step(0, 0, s0)

    def body(i, carry):
        for slot in range(2):
            c = 2 * i + slot
            for s0 in steps:
                up_step(c + 1, 1 - slot, s0)
                gate_down_step(c, slot, s0)
        return carry

    lax.fori_loop(0, (n_chunks - 1) // 2, body, 0)
    for s0 in steps:
        rows = step_rows(s0)
        gate_down_step(n_chunks - 1, 0, s0)
        out_ref[rows, :] = x_ref[rows, :] + _mod_slice(mod_ref, 5) * (acc_ref[rows, :] + bdn_ref[...])


def _ffn(x, norm_g, mod, layer, w_up, b_up, w_dw, b_dw, w_down, b_down, group0=0, n_groups=N_GROUPS):
    const = lambda g: (0, 0)
    layer_block = lambda g: (layer, 0, 0)
    resident = dict(pipeline_mode=pl.Buffered(1))
    return pl.pallas_call(
        functools.partial(_ffn_kernel, group0=group0),
        grid=(n_groups,),
        in_specs=[
            pl.BlockSpec((TM, D_MODEL), lambda g: (g + group0, 0)),
            pl.BlockSpec((1, D_MODEL), const),
            pl.BlockSpec((1, 1, 6 * D_MODEL), lambda g: (g + group0, 0, 0)),
            pl.BlockSpec((None, D_MODEL, 2 * D_FF), layer_block, **resident),
            pl.BlockSpec((1, 2 * D_FF), const),
            pl.BlockSpec((3, 2 * D_FF), const),
            pl.BlockSpec((1, 2 * D_FF), const),
            pl.BlockSpec((None, D_FF, D_MODEL), layer_block, **resident),
            pl.BlockSpec((1, D_MODEL), const),
        ],
        out_specs=pl.BlockSpec((TM, D_MODEL), lambda g: (g, 0)),
        out_shape=jax.ShapeDtypeStruct((n_groups * TM, D_MODEL), F32),
        scratch_shapes=[pltpu.VMEM((TM, D_MODEL), BF16), pltpu.VMEM((TM, D_MODEL), F32)]
        + [pltpu.VMEM((TM // FFN_SLAB * (FFN_SLAB + 2 * FFN_HALO), TF), F32)] * 4
        + [pltpu.VMEM((TM, TF), BF16)] * 2,
        compiler_params=_cparams(("arbitrary",)),
        name="conv_ffn",
    )(x, norm_g, mod, w_up, b_up.reshape(1, -1), w_dw, b_dw.reshape(1, -1), w_down, b_down.reshape(1, -1))


def _pair_queries(q):
    lane = lax.broadcasted_iota(jnp.int32, q.shape, 1)
    zero = jnp.zeros_like(q)
    return jnp.concatenate([jnp.where(lane < NA_HEAD_DIM, q, zero),
                            jnp.where(lane < NA_HEAD_DIM, zero, q)], axis=0)


def _pair_merge(o2):
    n = o2.shape[0] // 2
    lane = lax.broadcasted_iota(jnp.int32, (n, o2.shape[1]), 1)
    return jnp.where(lane < NA_HEAD_DIM, o2[:n], o2[n:])


def _nt_dot(a, b):
    return lax.dot_general(a, b, (((1,), (1,)), ((), ())), preferred_element_type=F32)


def _ctx_attn_kernel(q_ref, kt_ref, vt_ref, o_ref):
    for b in range(TM // SEQ):
        rows = slice(b * SEQ, (b + 1) * SEQ)
        q2 = _pair_queries(q_ref[rows, :])
        s = jnp.dot(q2, kt_ref[b].astype(BF16), preferred_element_type=F32)
        p = jnp.exp(s - jnp.max(s, axis=-1, keepdims=True))
        l = jnp.sum(p, axis=-1, keepdims=True)
        o2 = _nt_dot(p.astype(BF16), vt_ref[b].astype(BF16)) / l
        o_ref[rows, :] = _pair_merge(o2).astype(o_ref.dtype)


def _ctx_attention(q, kt, vt):
    spec = pl.BlockSpec((TM, 128), lambda g, hp: (g, hp))
    kv_spec = pl.BlockSpec((TM // SEQ, 128, SEQ), lambda g, hp: (g, hp, 0))
    return pl.pallas_call(
        _ctx_attn_kernel,
        grid=(N_CTX_GROUPS, NA_HEADS // 2),
        in_specs=[spec, kv_spec, kv_spec],
        out_specs=spec,
        out_shape=jax.ShapeDtypeStruct((N_CTX_TOK, D_MODEL), BF16),
        compiler_params=_cparams(("arbitrary", "arbitrary")),
        name="ctx_attention",
    )(q, kt, vt)


_NA_BLOCKS = ((0, 8, 0), (0, 12, 1), (4, 12, 1), (8, 8, 2))


def _na_attn_kernel(q_ref, k_ref, v_ref, ck_ref, cv_ref, ba_ref, bm_ref, bc_ref, o_ref):
    bias_refs = (ba_ref, bm_ref, bc_ref)
    ck = ck_ref[0, 0].astype(BF16)
    cv = cv_ref[0, 0].astype(BF16)
    for blk, (row0, nrows, bidx) in enumerate(_NA_BLOCKS):
        rows = slice(blk * 4 * GRID_W, (blk + 1) * 4 * GRID_W)
        keys = slice(row0 * GRID_W, (row0 + nrows) * GRID_W)
        q2 = _pair_queries(q_ref[rows, :])
        bias = bias_refs[bidx][...]
        s_loc = _nt_dot(q2, k_ref[keys, :]) + bias.reshape(2 * 4 * GRID_W, nrows * GRID_W)
        s_ctx = _nt_dot(q2, ck)
        m = jnp.maximum(jnp.max(s_loc, axis=-1, keepdims=True), jnp.max(s_ctx, axis=-1, keepdims=True))
        p_loc = jnp.exp(s_loc - m)
        p_ctx = jnp.exp(s_ctx - m)
        l = jnp.sum(p_loc, axis=-1, keepdims=True) + jnp.sum(p_ctx, axis=-1, keepdims=True)
        o2 = (jnp.dot(p_loc.astype(BF16), v_ref[keys, :], preferred_element_type=F32)
              + jnp.dot(p_ctx.astype(BF16), cv, preferred_element_type=F32)) / l
        o_ref[rows, :] = _pair_merge(o2).astype(o_ref.dtype)


N_RPB_R = 2 * NA_WIN_R - 1
N_RPB_C = 2 * NA_WIN_C - 1


def _na_bias_kernel(rpb_ref, ba_ref, bm_ref, bc_ref):
    base = pl.program_id(0) * (N_RPB_R * N_RPB_C)
    qc = lax.broadcasted_iota(jnp.int32, (GRID_W, 2 * GRID_W), 0)
    lane = lax.broadcasted_iota(jnp.int32, (GRID_W, 2 * GRID_W), 1)
    kc = lane & (GRID_W - 1)
    right = lane >= GRID_W
    dcol = kc - qc + (NA_WIN_C - 1)
    c_start = jnp.clip(qc - NA_WIN_C // 2, 0, GRID_W - NA_WIN_C)
    in_win = (kc >= c_start) & (kc < c_start + NA_WIN_C)
    neg = jnp.full((GRID_W, 2 * GRID_W), NEG_INF, F32)
    row_tiles, tiles = {}, {}

    def row_tile(dr):
        if dr not in row_tiles:
            acc = jnp.zeros((GRID_W, 2 * GRID_W), F32)
            if 0 <= dr < N_RPB_R:
                for j in range(N_RPB_C):
                    acc = jnp.where(dcol == j, rpb_ref[base + dr * N_RPB_C + j], acc)
            row_tiles[dr] = acc
        return row_tiles[dr]

    def pair_tile(dr):
        if dr not in tiles:
            tiles[dr] = jnp.where(right, row_tile(dr + 1), row_tile(dr))
        return tiles[dr]

    rows_total = DEC_SEQ // GRID_W
    for ref, blk in ((ba_ref, 0), (bm_ref, 1), (bc_ref, 3)):
        row0, nrows, _ = _NA_BLOCKS[blk]
        for rr in range(4):
            r = blk * 4 + rr
            r_start = min(max(r - NA_WIN_R // 2, 0), rows_total - NA_WIN_R)
            for ip in range(nrows // 2):
                krow = row0 + 2 * ip
                ok_l = r_start <= krow < r_start + NA_WIN_R
                ok_r = r_start <= krow + 1 < r_start + NA_WIN_R
                if ok_l or ok_r:
                    mask = in_win
                    if not ok_l:
                        mask = mask & right
                    if not ok_r:
                        mask = mask & jnp.logical_not(right)
                    tile = jnp.where(mask, pair_tile(krow - r + NA_WIN_R - 1), neg)
                else:
                    tile = neg
                ref[0, rr * GRID_W:(rr + 1) * GRID_W, ip * 2 * GRID_W:(ip + 1) * 2 * GRID_W] = tile


def _na_bias_tables(rpb):
    out = lambda n: pl.BlockSpec((1, 4 * GRID_W, n), lambda h: (h, 0, 0))
    shape = lambda n: jax.ShapeDtypeStruct((NA_HEADS, 4 * GRID_W, n), F32)
    return pl.pallas_call(
        _na_bias_kernel,
        grid=(NA_HEADS,),
        in_specs=[pl.BlockSpec(memory_space=pltpu.SMEM)],
        out_specs=[out(8 * GRID_W), out(12 * GRID_W), out(8 * GRID_W)],
        out_shape=[shape(8 * GRID_W), shape(12 * GRID_W), shape(8 * GRID_W)],
        compiler_params=_cparams(("arbitrary",)),
        name="na_bias",
    )(rpb.astype(F32).reshape(-1))


def _na_attention(q, k, v, cache_k, cache_v, layer_j, bias_tables):
    ba, bm, bc = bias_tables
    tok = pl.BlockSpec((TM, 128), lambda hp, b: (b, hp))
    cache = pl.BlockSpec((1, 1, PAST_LEN, 128), lambda hp, b: (b, layer_j, 0, hp))
    bias = lambda n: pl.BlockSpec((2, 4 * GRID_W, n), lambda hp, b: (hp, 0, 0))
    return pl.pallas_call(
        _na_attn_kernel,
        grid=(NA_HEADS // 2, DEC_BATCH),
        in_specs=[tok, tok, tok, cache, cache, bias(8 * GRID_W), bias(12 * GRID_W), bias(8 * GRID_W)],
        out_specs=tok,
        out_shape=jax.ShapeDtypeStruct((N_LAT_TOK, D_MODEL), BF16),
        compiler_params=_cparams(("arbitrary", "arbitrary")),
        name="na_attention",
    )(q, k, v, cache_k, cache_v, ba, bm, bc)


CONV_PAD = 16
CONV_ROWS = 128
CONV_LANES = 128


def _conv_tail_kernel(x_ref, u_ref, wdw_ref, bdw_ref, lng_ref, lnb_ref, w_ref, b_ref, mod_ref,
                      o_ref, pad_ref, a_ref):
    half = CONV_WIDTH // 2
    win = CONV_ROWS + 8
    zeros = jnp.zeros((CONV_PAD, D_MODEL), F32)

    def conv_sequence(base, seq_len):
        pad_ref[0:CONV_PAD, :] = zeros
        pad_ref[CONV_PAD + seq_len:2 * CONV_PAD + seq_len, :] = zeros
        pad_ref[CONV_PAD:CONV_PAD + seq_len, :] = u_ref[base:base + seq_len, :]

        def rows(i, carry):
            r0 = pl.multiple_of(i * CONV_ROWS, CONV_ROWS)
            strips = []
            for l0 in range(0, D_MODEL, CONV_LANES):
                lanes = slice(l0, l0 + CONV_LANES)
                acc = jnp.zeros((CONV_ROWS, CONV_LANES), F32) + bdw_ref[:, lanes]
                for b in range(8):
                    z = None
                    for a in range(-2, 2):
                        t = 8 * a + b + half
                        if not 0 <= t < CONV_WIDTH:
                            continue
                        term = wdw_ref[t:t + 1, lanes] * pad_ref[pl.ds(r0 + (CONV_PAD + 8 * a), win), lanes]
                        z = term if z is None else z + term
                    if b:
                        z = pltpu.roll(z, win - b, 0)
                    acc = acc + z[:CONV_ROWS]
                strips.append(acc)
            acc = jnp.concatenate(strips, axis=1)
            mu = jnp.mean(acc, axis=-1, keepdims=True)
            cen = acc - mu
            var = jnp.mean(cen * cen, axis=-1, keepdims=True)
            y = cen * lax.rsqrt(var + NORM_EPS) * lng_ref[...] + lnb_ref[...]
            a_ref[pl.ds(base + r0, CONV_ROWS), :] = (y * _sigmoid(y)).astype(BF16)
            return carry

        lax.fori_loop(0, seq_len // CONV_ROWS, rows, 0)

    @pl.when(pl.program_id(0) < N_CTX_GROUPS)
    def _():
        for s in range(TM // SEQ):
            conv_sequence(s * SEQ, SEQ)

    @pl.when(pl.program_id(0) >= N_CTX_GROUPS)
    def _():
        conv_sequence(0, DEC_SEQ)

    for rows in _row_slabs():
        r = jnp.dot(a_ref[rows, :], w_ref[...], preferred_element_type=F32) + b_ref[...]
        o_ref[rows, :] = x_ref[rows, :] + _mod_slice(mod_ref, 2) * r


def _conv_tail(x, u, w_dw, b_dw, ln_g, ln_b, w_pw2, b_pw2, mod):
    tok = pl.BlockSpec((TM, D_MODEL), lambda g: (g, 0))
    const = lambda g: (0, 0)
    vec = pl.BlockSpec((1, D_MODEL), const)
    return pl.pallas_call(
        _conv_tail_kernel,
        grid=(N_GROUPS,),
        in_specs=[tok, tok, pl.BlockSpec((CONV_WIDTH, D_MODEL), const), vec, vec, vec,
                  pl.BlockSpec((D_MODEL, D_MODEL), const), vec,
                  pl.BlockSpec((1, 1, 6 * D_MODEL), lambda g: (g, 0, 0))],
        out_specs=tok,
        out_shape=jax.ShapeDtypeStruct((N_TOK, D_MODEL), F32),
        scratch_shapes=[pltpu.VMEM((TM + 2 * CONV_PAD, D_MODEL), F32),
                        pltpu.VMEM((TM, D_MODEL), BF16)],
        compiler_params=_cparams(("arbitrary",)),
        name="conv_tail",
    )(x, u, w_dw, b_dw.reshape(1, -1), ln_g.reshape(1, -1), ln_b.reshape(1, -1), w_pw2,
      b_pw2.reshape(1, -1), mod)


def _log_sigmoid(z):
    return -(jnp.maximum(-z, 0.0) + jnp.log(1.0 + jnp.exp(-jnp.abs(z))))


def _gla_scan_kernel(*refs, seq_len, has_init, emit_state):
    it = iter(refs)
    q_ref, k_ref, v_ref, r_ref, w2_ref, bgk_ref = (next(it) for _ in range(6))
    s0_refs = (next(it), next(it)) if has_init else None
    o_ref = next(it)
    st_out = (next(it), next(it)) if emit_state else None
    g_ref, cum_ref, st_ref = (next(it) for _ in range(3))

    c = GLA_CHUNK
    n_chunks = seq_len // c
    rlow = r_ref[...].astype(BF16)
    row = lax.broadcasted_iota(jnp.int32, (c, 1), 0)
    sub8 = lax.broadcasted_iota(jnp.int32, (8, 1), 0)
    ri = lax.broadcasted_iota(jnp.int32, (c, c), 0)
    ci = lax.broadcasted_iota(jnp.int32, (c, c), 1)
    pair_xor = ri ^ ci

    for d in range(2):
        reverse = d == 1
        z = jnp.dot(rlow, w2_ref[d].astype(BF16), preferred_element_type=F32) + bgk_ref[d]
        g_ref[...] = _log_sigmoid(z) * (1.0 / GLA_GATE_NORM)
        if has_init:
            st_ref[...] = s0_refs[d][0, 0].T
        else:
            st_ref[...] = jnp.zeros_like(st_ref)

        order = range(n_chunks - 1, -1, -1) if reverse else range(n_chunks)
        for ch in order:
            base = ch * c
            rows = slice(base, base + c)
            q = q_ref[rows, :]
            k = k_ref[rows, :]
            v = v_ref[rows, :].astype(BF16)
            cum = g_ref[rows, :]
            sh = 1
            while sh < c:
                if reverse:
                    cum = cum + jnp.where(row < c - sh, pltpu.roll(cum, c - sh, 0), 0.0)
                else:
                    cum = cum + jnp.where(row >= sh, pltpu.roll(cum, sh, 0), 0.0)
                sh *= 2
            cum_ref[...] = cum

            att = None
            m = c
            while m >= 2:
                half = m // 2
                arow = half if reverse else half - 1
                if m >= 8:
                    pieces = [jnp.broadcast_to(cum_ref[b * m + arow:b * m + arow + 1, :], (m, GLA_DK))
                              for b in range(c // m)]
                else:
                    pieces = []
                    for t in range(c // 8):
                        tile = None
                        for b in range(8 // m - 1, -1, -1):
                            r = t * 8 + b * m + arow
                            cand = jnp.broadcast_to(cum_ref[r:r + 1, :], (8, GLA_DK))
                            tile = cand if tile is None else jnp.where(sub8 < (b + 1) * m, cand, tile)
                        pieces.append(tile)
                anchor = pieces[0] if len(pieces) == 1 else jnp.concatenate(pieces, axis=0)
                fac = jnp.exp(-jnp.abs(cum - anchor))
                p = _nt_dot((q * fac).astype(BF16), (k * fac).astype(BF16))
                att = p if att is None else jnp.where(pair_xor < m, p, att)
                m //= 2
            att = jnp.where(ri == ci, jnp.sum(q * k, axis=-1, keepdims=True), att)
            att = jnp.where((ri <= ci) if reverse else (ri >= ci), att, 0.0)

            last = 0 if reverse else c - 1
            total = cum_ref[last:last + 1, :]
            q_in = (q * jnp.exp(cum)).astype(BF16)
            k_out = (k * jnp.exp(total - cum)).astype(BF16)
            st = st_ref[...]
            o = (jnp.dot(att.astype(BF16), v, preferred_element_type=F32)
                 + _nt_dot(q_in, st.astype(BF16)))
            if reverse:
                o_ref[rows, :] += o
            else:
                o_ref[rows, :] = o
            kv = lax.dot_general(v, k_out, (((0,), (0,)), ((), ())), preferred_element_type=F32)
            st_ref[...] = st * jnp.exp(total) + kv

        if emit_state:
            st_out[d][0, 0] = st_ref[...].T


def _gla_scan(y, w2pad, b_gk, init_states, seq_len, blk0, n_seq, emit_state):
    has_init = init_states is not None
    qk = lambda off: pl.BlockSpec((seq_len, GLA_DK), lambda b, h: (b + blk0, h + off))
    state = pl.BlockSpec((1, 1, GLA_DK, GLA_DV), lambda b, h: (b, h, 0, 0))
    in_specs = [
        qk(0), qk(GLA_KEY_DIM // GLA_DK),
        pl.BlockSpec((seq_len, GLA_DV), lambda b, h: (b + blk0, h + 2 * GLA_KEY_DIM // GLA_DV)),
        pl.BlockSpec((seq_len, 128), lambda b, h: (b + blk0, (2 * GLA_KEY_DIM + 2 * GLA_VAL_DIM) // 128)),
        pl.BlockSpec((2, 128, GLA_DK), lambda b, h: (0, 0, h)),
        pl.BlockSpec((2, 1, GLA_DK), lambda b, h: (0, 0, h)),
    ]
    args = [y, y, y, y, w2pad, b_gk]
    if has_init:
        in_specs += [state, state]
        args += list(init_states)
    out_specs = [pl.BlockSpec((seq_len, GLA_DV), lambda b, h: (b, h))]
    out_shape = [jax.ShapeDtypeStruct((n_seq * seq_len, GLA_VAL_DIM), F32)]
    if emit_state:
        out_specs += [state, state]
        out_shape += [jax.ShapeDtypeStruct((n_seq, GLA_HEADS, GLA_DK, GLA_DV), F32)] * 2
    return pl.pallas_call(
        functools.partial(_gla_scan_kernel, seq_len=seq_len, has_init=has_init, emit_state=emit_state),
        grid=(n_seq, GLA_HEADS),
        in_specs=in_specs,
        out_specs=out_specs,
        out_shape=out_shape,
        scratch_shapes=[pltpu.VMEM((seq_len, GLA_DK), F32), pltpu.VMEM((GLA_CHUNK, GLA_DK), F32),
                        pltpu.VMEM((GLA_DV, GLA_DK), F32)],
        compiler_params=_cparams(("arbitrary", "arbitrary")),
        name="gla_scan",
    )(*args)


_GROUP_COND_ROW = np.array([0] * N_CTX_GROUPS + list(range(1, DEC_BATCH + 1)))


def kernel(x_prompt, x_sample, c, cache_attn_k, cache_attn_v, state_gla_fwd, state_gla_bwd, c_ctx,
           mod_w, mod_b, norm1_g, norm2_g,
           attn_w_qkv, attn_w_o, attn_q_norm, attn_k_norm, attn_rpb,
           conv_w_pw1, conv_b_pw1, conv_w_dw, conv_b_dw, conv_ln_g, conv_ln_b, conv_w_pw2, conv_b_pw2,
           gla_w_q, gla_w_k, gla_w_v, gla_w_g, gla_w_gk1, gla_w_gk2, gla_b_gk, gla_o_norm, gla_w_o,
           ffn_w_up, ffn_b_up, ffn_w_dw, ffn_b_dw, ffn_w_down, ffn_b_down):
    x_ctx0 = x_prompt.reshape(N_CTX_TOK, D_MODEL)
    x_lat0 = x_sample.reshape(N_LAT_TOK, D_MODEL)
    x = None
    cond =jnp.concatenate([c_ctx[None, :], c, jnp.zeros((N_COND - 1 - DEC_BATCH, D_MODEL), F32)], axis=0)
    mod_all = _modulation(cond, mod_w, mod_b)
    mod_all = mod_all[:, _GROUP_COND_ROW][:, :, None, :]

    cache_k = cache_attn_k.reshape(DEC_BATCH, -1, PAST_LEN, D_MODEL)
    cache_v = cache_attn_v.reshape(DEC_BATCH, -1, PAST_LEN, D_MODEL)
    ffn_w_up_bf16 = ffn_w_up.astype(BF16)
    ffn_w_down_bf16 = ffn_w_down.astype(BF16)
    new_k, new_v, new_sf, new_sb = [], [], [], []
    for i in range(DEPTH):
        kind, j = i % N_MIXERS, i // N_MIXERS
        mod = mod_all[i]
        n1 = norm1_g[i].reshape(1, D_MODEL)
        if kind == 0:
            w_qkv = attn_w_qkv[j].astype(BF16)
            gain = jnp.concatenate([jnp.tile(attn_q_norm[j], NA_HEADS) * (NA_HEAD_DIM ** -0.5),
                                    jnp.tile(attn_k_norm[j], NA_HEADS),
                                    jnp.ones((D_MODEL,), F32)]).reshape(1, 3 * D_MODEL)
            if x is None:
                x_ctx, x_lat, lat_block0 = x_ctx0, x_lat0, 0
            else:
                x_ctx, x_lat, lat_block0 = x, x, N_CTX_GROUPS
            n_lat_groups = N_GROUPS - N_CTX_GROUPS
            qp, kp, vp = _qkv_proj(x_ctx, n1, mod, w_qkv, gain, 0, 0, N_CTX_GROUPS, True)
            qs, ks, vs = _qkv_proj(x_lat, n1, mod, w_qkv, gain, lat_block0, N_CTX_GROUPS, n_lat_groups, False)
            o_ctx = _ctx_attention(qp, kp, vp)
            o_lat = _na_attention(qs, ks, vs, cache_k, cache_v, j, _na_bias_tables(attn_rpb[j]))
            x = _out_proj(x_ctx, x_lat, o_ctx, o_lat, attn_w_o[j].astype(BF16), mod)
            new_k.append(kp)
            new_v.append(vp)
        elif kind == 1:
            u = _glu_proj(x, n1, mod, conv_w_pw1[j].astype(BF16), conv_b_pw1[j])
            x = _conv_tail(x, u, conv_w_dw[j], conv_b_dw[j], conv_ln_g[j], conv_ln_b[j],
                           conv_w_pw2[j].astype(BF16), conv_b_pw2[j], mod)
        else:
            pad = GLA_PROJ_N - 2 * GLA_KEY_DIM - 2 * GLA_VAL_DIM - 2 * GLA_GATE_RANK
            w_cat = jnp.concatenate([gla_w_q[j], gla_w_k[j], gla_w_v[j], gla_w_g[j], gla_w_gk1[j, 0],
                                     gla_w_gk1[j, 1], jnp.zeros((D_MODEL, pad), F32)], axis=1).astype(BF16)
            colscale = jnp.concatenate([jnp.full((GLA_KEY_DIM,), GLA_DK ** -0.5, F32),
                                        jnp.ones((GLA_PROJ_N - GLA_KEY_DIM,), F32)]).reshape(1, GLA_PROJ_N)
            y = _scaled_proj(x, n1, mod, w_cat, colscale)
            w2pad = jnp.zeros((2, 128, GLA_KEY_DIM), F32)
            w2pad = w2pad.at[0, :GLA_GATE_RANK].set(gla_w_gk2[j, 0])
            w2pad = w2pad.at[1, GLA_GATE_RANK:2 * GLA_GATE_RANK].set(gla_w_gk2[j, 1])
            b_gk = gla_b_gk[j].reshape(2, 1, GLA_KEY_DIM)
            o_ctx, sf, sb = _gla_scan(y, w2pad, b_gk, None, SEQ, 0, BATCH, True)
            (o_lat,) = _gla_scan(y, w2pad, b_gk, (state_gla_fwd[:, j], state_gla_bwd[:, j]),
                                 DEC_SEQ, N_CTX_GROUPS, DEC_BATCH, False)
            x = _gla_out_proj(x, o_ctx, o_lat, y, gla_o_norm[j].reshape(1, GLA_DV),
                              gla_w_o[j].astype(BF16), mod)
            new_sf.append(sf)
            new_sb.append(sb)
        ffn_args = (norm2_g[i].reshape(1, D_MODEL), mod, i, ffn_w_up_bf16, ffn_b_up[i],
                    ffn_w_dw[i], ffn_b_dw[i], ffn_w_down_bf16, ffn_b_down[i])
        if i < DEPTH - 1:
            x = _ffn(x, *ffn_args)
        else:
            y_ctx = _ffn(x, *ffn_args, group0=0, n_groups=N_CTX_GROUPS)
            y_lat = _ffn(x, *ffn_args, group0=N_CTX_GROUPS, n_groups=N_GROUPS - N_CTX_GROUPS)

    y_prompt = y_ctx.reshape(BATCH, SEQ, D_MODEL)
    y_sample = y_lat.reshape(DEC_BATCH, DEC_SEQ, D_MODEL)
    def cache_layout(parts):
        stacked = jnp.stack(parts, axis=1).reshape(BATCH, len(parts), NA_HEADS, NA_HEAD_DIM, SEQ)
        return stacked.transpose(0, 1, 4, 2, 3)

    return (y_prompt, y_sample, cache_layout(new_k), cache_layout(new_v),
            jnp.stack(new_sf, axis=1), jnp.stack(new_sb, axis=1))
```

```python
import functools

import numpy as np
import jax
import jax.numpy as jnp
from jax import lax
from jax.experimental import pallas as pl
from jax.experimental.pallas import tpu as pltpu

F32 = jnp.float32
BF16 = jnp.bfloat16

D_MODEL = 1024
BATCH = 16
SEQ = 256
DEPTH = 4
DEC_BATCH = 4
DEC_SEQ = 1024
PAST_LEN = 512
GRID_W = 64
N_MIXERS = 3
NA_HEADS = 16
NA_HEAD_DIM = 64
NA_WIN_R = 8
NA_WIN_C = 16
CONV_WIDTH = 31
GLA_HEADS = 4
GLA_KEY_DIM = 512
GLA_VAL_DIM = 1024
GLA_DK = 128
GLA_DV = 256
GLA_GATE_RANK = 16
GLA_GATE_NORM = 16.0
D_FF = 2816
NORM_EPS = 1e-6
NEG_INF = -1e30

TM = 1024
N_CTX_TOK = BATCH * SEQ
N_LAT_TOK = DEC_BATCH * DEC_SEQ
N_TOK = N_CTX_TOK + N_LAT_TOK
N_CTX_GROUPS = N_CTX_TOK // TM
N_GROUPS = N_TOK // TM
N_COND = 8
TF = 256
PROJ_SLAB = 512
FFN_SLAB = SEQ
FFN_HALO = 8
FFN_STEP_SLABS = 4
GLA_CHUNK = 256
GLA_PROJ_N = 3328
VMEM_LIMIT = 60 * 1024 * 1024


def _cparams(sem, flags=None):
    return pltpu.CompilerParams(dimension_semantics=sem, vmem_limit_bytes=VMEM_LIMIT, flags=flags)


def _sigmoid(x):
    return 1.0 / (1.0 + jnp.exp(-x))


def _norm_mod(x, g, shift, scale):
    ms = jnp.mean(x * x, axis=-1, keepdims=True)
    y = x * lax.rsqrt(ms + NORM_EPS) * g
    return y * (1.0 + scale) + shift


def _mod_slice(mod_ref, idx):
    return mod_ref[0, :, idx * D_MODEL:(idx + 1) * D_MODEL]


def _mod_kernel(cond_ref, w_ref, b_ref, o_ref):
    c = cond_ref[...]
    s = (c * _sigmoid(c)).astype(BF16)
    o_ref[0] = jnp.dot(s, w_ref[0].astype(BF16), preferred_element_type=F32) + b_ref[0]


def _modulation(cond, mod_w, mod_b):
    tn = 1536
    n = 6 * D_MODEL
    return pl.pallas_call(
        _mod_kernel,
        grid=(DEPTH, n // tn),
        in_specs=[
            pl.BlockSpec((N_COND, D_MODEL), lambda l, j: (0, 0)),
            pl.BlockSpec((1, D_MODEL, tn), lambda l, j: (l, 0, j)),
            pl.BlockSpec((1, 1, tn), lambda l, j: (l, 0, j)),
        ],
        out_specs=pl.BlockSpec((1, N_COND, tn), lambda l, j: (l, 0, j)),
        out_shape=jax.ShapeDtypeStruct((DEPTH, N_COND, n), F32),
        compiler_params=_cparams(("arbitrary", "arbitrary")),
        name="modulation",
    )(cond, mod_w, mod_b.reshape(DEPTH, 1, n))


def _proj_slabs(x_ref, g_ref, mod_ref, h_ref, epilogue):
    def run(first):
        for s in range(TM // PROJ_SLAB):
            rows = slice(s * PROJ_SLAB, (s + 1) * PROJ_SLAB)
            if first:
                h = _norm_mod(x_ref[rows, :], g_ref[...], _mod_slice(mod_ref, 0), _mod_slice(mod_ref, 1))
                h_ref[rows, :] = h.astype(BF16)
            epilogue(rows, h_ref[rows, :])

    pl.when(pl.program_id(1) == 0)(lambda: run(True))
    pl.when(pl.program_id(1) != 0)(lambda: run(False))


def _head_rms(acc, gain, hsum_ref, hexp_ref):
    ms = jnp.dot((acc * acc).astype(BF16), hsum_ref[...], preferred_element_type=F32)
    inv = lax.rsqrt(ms + NORM_EPS)
    hi = inv.astype(BF16)
    lo = (inv - hi.astype(F32)).astype(BF16)
    inv_full = jnp.dot(jnp.concatenate([hi, lo], axis=1), hexp_ref[...], preferred_element_type=F32)
    return acc * inv_full * gain


def _qkv_kernel(x_ref, g_ref, mod_ref, w_ref, gain_ref, hsum_ref, hexp_ref,
                q_ref, k_ref, v_ref, h_ref, t_ref, *, kv_channel_major):
    j = pl.program_id(1)

    def column_step(out_ref, normed, first, channel_major):
        for s in range(TM // SEQ):
            rows = slice(s * SEQ, (s + 1) * SEQ)
            if first:
                h = _norm_mod(x_ref[rows, :], g_ref[...], _mod_slice(mod_ref, 0), _mod_slice(mod_ref, 1))
                h_ref[rows, :] = h.astype(BF16)
            acc = jnp.dot(h_ref[rows, :], w_ref[...], preferred_element_type=F32)
            if normed:
                acc = _head_rms(acc, gain_ref[...], hsum_ref, hexp_ref)
            if channel_major:
                t_ref[...] = acc
                out_ref[s] = t_ref[...].T
            else:
                out_ref[rows, :] = acc.astype(out_ref.dtype)

    pl.when(j == 0)(lambda: column_step(q_ref, True, True, False))
    pl.when(j == 1)(lambda: column_step(k_ref, True, False, kv_channel_major))
    pl.when(j == 2)(lambda: column_step(v_ref, False, False, kv_channel_major))


def _head_matrices():
    lane = np.arange(D_MODEL)
    hsum = np.zeros((D_MODEL, 128), np.float32)
    hsum[lane, lane // NA_HEAD_DIM] = 1.0 / NA_HEAD_DIM
    hexp = np.zeros((128, D_MODEL), np.float32)
    hexp[lane // NA_HEAD_DIM, lane] = 1.0
    return jnp.asarray(hsum, BF16), jnp.asarray(np.concatenate([hexp, hexp], 0), BF16)


def _qkv_proj(x, norm_g, mod, w_qkv, gain, x_group0, mod_group0, n_groups, kv_channel_major):
    hsum, hexp = _head_matrices()
    ntok = n_groups * TM
    row = lambda g, j: (g, 0)
    tok_spec = pl.BlockSpec((TM, D_MODEL), row)
    if kv_channel_major:
        kv_spec = pl.BlockSpec((TM // SEQ, D_MODEL, SEQ), lambda g, j: (g, 0, 0))
        kv_shape = jax.ShapeDtypeStruct((ntok // SEQ, D_MODEL, SEQ), F32)
    else:
        kv_spec = tok_spec
        kv_shape = jax.ShapeDtypeStruct((ntok, D_MODEL), BF16)
    return pl.pallas_call(
        functools.partial(_qkv_kernel, kv_channel_major=kv_channel_major),
        grid=(n_groups, 3),
        in_specs=[
            pl.BlockSpec((TM, D_MODEL), lambda g, j: (g + x_group0, 0)),
            pl.BlockSpec((1, D_MODEL), lambda g, j: (0, 0)),
            pl.BlockSpec((1, 1, 6 * D_MODEL), lambda g, j: (g + mod_group0, 0, 0)),
            pl.BlockSpec((D_MODEL, D_MODEL), lambda g, j: (0, j)),
            pl.BlockSpec((1, D_MODEL), lambda g, j: (0, j)),
            pl.BlockSpec((D_MODEL, 128), lambda g, j: (0, 0)),
            pl.BlockSpec((256, D_MODEL), lambda g, j: (0, 0)),
        ],
        out_specs=[tok_spec, kv_spec, kv_spec],
        out_shape=[jax.ShapeDtypeStruct((ntok, D_MODEL), BF16), kv_shape, kv_shape],
        scratch_shapes=[pltpu.VMEM((TM, D_MODEL), BF16), pltpu.VMEM((SEQ, D_MODEL), F32)],
        compiler_params=_cparams(("arbitrary", "arbitrary")),
        name="qkv_proj",
    )(x, norm_g, mod, w_qkv, gain, hsum, hexp)


def _glu_kernel(x_ref, g_ref, mod_ref, wa_ref, wg_ref, ba_ref, bg_ref, u_ref, h_ref):
    def epilogue(rows, h):
        a = jnp.dot(h, wa_ref[...], preferred_element_type=F32) + ba_ref[...]
        g = jnp.dot(h, wg_ref[...], preferred_element_type=F32) + bg_ref[...]
        u_ref[rows, :] = a * _sigmoid(g)

    _proj_slabs(x_ref, g_ref, mod_ref, h_ref, epilogue)


def _glu_proj(x, norm_g, mod, w_pw1, b_pw1):
    tn = 512
    nj = D_MODEL // tn
    b = b_pw1.reshape(1, 2 * D_MODEL)
    return pl.pallas_call(
        _glu_kernel,
        grid=(N_GROUPS, nj),
        in_specs=[
            pl.BlockSpec((TM, D_MODEL), lambda g, j: (g, 0)),
            pl.BlockSpec((1, D_MODEL), lambda g, j: (0, 0)),
            pl.BlockSpec((1, 1, 6 * D_MODEL), lambda g, j: (g, 0, 0)),
            pl.BlockSpec((D_MODEL, tn), lambda g, j: (0, j)),
            pl.BlockSpec((D_MODEL, tn), lambda g, j: (0, j + nj)),
            pl.BlockSpec((1, tn), lambda g, j: (0, j)),
            pl.BlockSpec((1, tn), lambda g, j: (0, j + nj)),
        ],
        out_specs=pl.BlockSpec((TM, tn), lambda g, j: (g, j)),
        out_shape=jax.ShapeDtypeStruct((N_TOK, D_MODEL), F32),
        scratch_shapes=[pltpu.VMEM((TM, D_MODEL), BF16)],
        compiler_params=_cparams(("arbitrary", "arbitrary")),
        name="glu_proj",
    )(x, norm_g, mod, w_pw1, w_pw1, b, b)


def _scaled_proj_kernel(x_ref, g_ref, mod_ref, w_ref, cs_ref, y_ref, h_ref):
    def epilogue(rows, h):
        y_ref[rows, :] = jnp.dot(h, w_ref[...], preferred_element_type=F32) * cs_ref[...]

    _proj_slabs(x_ref, g_ref, mod_ref, h_ref, epilogue)


def _scaled_proj(x, norm_g, mod, w, colscale):
    n = w.shape[1]
    tn = n // 2
    return pl.pallas_call(
        _scaled_proj_kernel,
        grid=(N_GROUPS, n // tn),
        in_specs=[
            pl.BlockSpec((TM, D_MODEL), lambda g, j: (g, 0)),
            pl.BlockSpec((1, D_MODEL), lambda g, j: (0, 0)),
            pl.BlockSpec((1, 1, 6 * D_MODEL), lambda g, j: (g, 0, 0)),
            pl.BlockSpec((D_MODEL, tn), lambda g, j: (0, j)),
            pl.BlockSpec((1, tn), lambda g, j: (0, j)),
        ],
        out_specs=pl.BlockSpec((TM, tn), lambda g, j: (g, j)),
        out_shape=jax.ShapeDtypeStruct((N_TOK, n), F32),
        scratch_shapes=[pltpu.VMEM((TM, D_MODEL), BF16)],
        compiler_params=_cparams(("arbitrary", "arbitrary")),
        name="gla_proj",
    )(x, norm_g, mod, w, colscale)


_CTX_PART = pl.BlockSpec((TM, D_MODEL), lambda g: (jnp.minimum(g, N_CTX_GROUPS - 1), 0))
_LAT_PART = pl.BlockSpec((TM, D_MODEL), lambda g: (jnp.maximum(g - N_CTX_GROUPS, 0), 0))


def _group_part(ctx_ref, lat_ref, rows):
    return jnp.where(pl.program_id(0) < N_CTX_GROUPS, ctx_ref[rows, :], lat_ref[rows, :])


def _row_slabs():
    return [slice(s * PROJ_SLAB, (s + 1) * PROJ_SLAB) for s in range(TM // PROJ_SLAB)]


def _oproj_kernel(x_ctx_ref, x_lat_ref, a_ctx_ref, a_lat_ref, w_ref, mod_ref, o_ref):
    for rows in _row_slabs():
        r = jnp.dot(_group_part(a_ctx_ref, a_lat_ref, rows), w_ref[...], preferred_element_type=F32)
        o_ref[rows, :] = _group_part(x_ctx_ref, x_lat_ref, rows) + _mod_slice(mod_ref, 2) * r


def _out_proj(x_ctx, x_lat, a_ctx, a_lat, w, mod):
    if x_lat is x_ctx:
        x_lat_spec = pl.BlockSpec((TM, D_MODEL), lambda g: (jnp.maximum(g, N_CTX_GROUPS), 0))
    else:
        x_lat_spec = _LAT_PART
    return pl.pallas_call(
        _oproj_kernel,
        grid=(N_GROUPS,),
        in_specs=[
            _CTX_PART, x_lat_spec, _CTX_PART, _LAT_PART,
            pl.BlockSpec((D_MODEL, D_MODEL), lambda g: (0, 0)),
            pl.BlockSpec((1, 1, 6 * D_MODEL), lambda g: (g, 0, 0)),
        ],
        out_specs=pl.BlockSpec((TM, D_MODEL), lambda g: (g, 0)),
        out_shape=jax.ShapeDtypeStruct((N_TOK, D_MODEL), F32),
        compiler_params=_cparams(("arbitrary",)),
        name="out_proj",
    )(x_ctx, x_lat, a_ctx, a_lat, w, mod)


def _gla_oproj_kernel(x_ref, o_ctx_ref, o_lat_ref, gz_ref, og_ref, w_ref, mod_ref, out_ref):
    og = og_ref[...]
    for rows in _row_slabs():
        o = _group_part(o_ctx_ref, o_lat_ref, rows)
        parts = []
        for h in range(GLA_HEADS):
            oh = o[:, h * GLA_DV:(h + 1) * GLA_DV]
            ms = jnp.mean(oh * oh, axis=-1, keepdims=True)
            parts.append(oh * lax.rsqrt(ms + NORM_EPS) * og)
        y = jnp.concatenate(parts, axis=1)
        gz = gz_ref[rows, :]
        a = (y * (gz * _sigmoid(gz))).astype(BF16)
        r = jnp.dot(a, w_ref[...], preferred_element_type=F32)
        out_ref[rows, :] = x_ref[rows, :] + _mod_slice(mod_ref, 2) * r


def _gla_out_proj(x, o_ctx, o_lat, y, o_norm, w, mod):
    return pl.pallas_call(
        _gla_oproj_kernel,
        grid=(N_GROUPS,),
        in_specs=[
            pl.BlockSpec((TM, D_MODEL), lambda g: (g, 0)),
            _CTX_PART, _LAT_PART,
            pl.BlockSpec((TM, GLA_VAL_DIM), lambda g: (g, 2)),
            pl.BlockSpec((1, GLA_DV), lambda g: (0, 0)),
            pl.BlockSpec((GLA_VAL_DIM, D_MODEL), lambda g: (0, 0)),
            pl.BlockSpec((1, 1, 6 * D_MODEL), lambda g: (g, 0, 0)),
        ],
        out_specs=pl.BlockSpec((TM, D_MODEL), lambda g: (g, 0)),
        out_shape=jax.ShapeDtypeStruct((N_TOK, D_MODEL), F32),
        compiler_params=_cparams(("arbitrary",)),
        name="gla_out_proj",
    )(x, o_ctx, o_lat, y, o_norm, w, mod)


def _ffn_kernel(x_ref, g_ref, mod_ref, wup_ref, bup_ref, wdw_ref, bdw_ref, wdn_ref, bdn_ref,
                out_ref, h_ref, acc_ref, ua0_ref, ug0_ref, ua1_ref, ug1_ref, act0_ref, act1_ref,
                *, group0):
    is_lat = pl.program_id(0) + group0 >= N_CTX_GROUPS
    up_bufs = ((ua0_ref, ug0_ref), (ua1_ref, ug1_ref))
    act_bufs = (act0_ref, act1_ref)
    n_slabs = TM // FFN_SLAB
    win = FFN_SLAB + 2 * FFN_HALO

    def chunk_cols(c):
        return pl.multiple_of(c * TF, TF), pl.multiple_of(D_FF + c * TF, TF)

    def step_rows(s0):
        return slice(s0 * FFN_SLAB, (s0 + FFN_STEP_SLABS) * FFN_SLAB)

    def up_step(c, slot, s0):
        h = h_ref[step_rows(s0), :]
        for buf, col in zip(up_bufs[slot], chunk_cols(c)):
            both = jnp.dot(h, wup_ref[:, pl.ds(col, TF)], preferred_element_type=F32)
            pad = jnp.broadcast_to(-bup_ref[:, pl.ds(col, TF)], (FFN_HALO, TF))
            for u in range(FFN_STEP_SLABS):
                s = s0 + u
                res = both[u * FFN_SLAB:(u + 1) * FFN_SLAB]
                base = s * win
                buf[base + FFN_HALO:base + FFN_HALO + FFN_SLAB, :] = res
                if s == 0:
                    buf[0:FFN_HALO, :] = pad
                else:
                    buf[base - FFN_HALO:base, :] = jnp.where(is_lat, res[:FFN_HALO], pad)
                if s == n_slabs - 1:
                    buf[base + win - FFN_HALO:base + win, :] = pad
                else:
                    buf[base + win:base + win + FFN_HALO, :] = jnp.where(
                        is_lat, res[FFN_SLAB - FFN_HALO:], pad)

    def conv_slab(buf, col, s):
        r0 = s * win + FFN_HALO
        prev = buf[r0 - 1:r0 - 1 + FFN_SLAB, :]
        mid = buf[r0:r0 + FFN_SLAB, :]
        nxt = buf[r0 + 1:r0 + 1 + FFN_SLAB, :]
        w = wdw_ref[:, pl.ds(col, TF)]
        bias = bdw_ref[:, pl.ds(col, TF)] + bup_ref[:, pl.ds(col, TF)] * (w[0:1] + w[1:2] + w[2:3])
        return w[0:1] * prev + w[1:2] * mid + w[2:3] * nxt + bias

    def gate_slab(c, slot, s):
        ca, cg = chunk_cols(c)
        a = conv_slab(up_bufs[slot][0], ca, s)
        g = conv_slab(up_bufs[slot][1], cg, s)
        act_bufs[slot][s * FFN_SLAB:(s + 1) * FFN_SLAB, :] = (g * _sigmoid(g) * a).astype(BF16)

    def gate_down_step(c, slot, s0):
        for u in range(FFN_STEP_SLABS):
            gate_slab(c, slot, s0 + u)
        rows = step_rows(s0)
        acc_ref[rows, :] += jnp.dot(act_bufs[slot][rows, :], wdn_ref[pl.ds(chunk_cols(c)[0], TF), :],
                                    preferred_element_type=F32)

    steps = range(0, n_slabs, FFN_STEP_SLABS)

    n_chunks = D_FF // TF
    acc_ref[...] = jnp.zeros_like(acc_ref)
    for s0 in steps:
        rows = step_rows(s0)
        h = _norm_mod(x_ref[rows, :], g_ref[...], _mod_slice(mod_ref, 3), _mod_slice(mod_ref, 4))
        h_ref[rows, :] = h.astype(BF16)
        up_step(0, 0, s0)

    def body(i, carry):
        for slot in range(2):
            c = 2 * i + slot
            for s0 in steps:
                up_step(c + 1, 1 - slot, s0)
                gate_down_step(c, slot, s0)
        return carry

    lax.fori_loop(0, (n_chunks - 1) // 2, body, 0)
    for s0 in steps:
        rows = step_rows(s0)
        gate_down_step(n_chunks - 1, 0, s0)
        out_ref[rows, :] = x_ref[rows, :] + _mod_slice(mod_ref, 5) * (acc_ref[rows, :] + bdn_ref[...])


def _ffn(x, norm_g, mod, layer, w_up, b_up, w_dw, b_dw, w_down, b_down, group0=0, n_groups=N_GROUPS):
    const = lambda g: (0, 0)
    layer_block = lambda g: (layer, 0, 0)
    resident = dict(pipeline_mode=pl.Buffered(1))
    return pl.pallas_call(
        functools.partial(_ffn_kernel, group0=group0),
        grid=(n_groups,),
        in_specs=[
            pl.BlockSpec((TM, D_MODEL), lambda g: (g + group0, 0)),
            pl.BlockSpec((1, D_MODEL), const),
            pl.BlockSpec((1, 1, 6 * D_MODEL), lambda g: (g + group0, 0, 0)),
            pl.BlockSpec((None, D_MODEL, 2 * D_FF), layer_block, **resident),
            pl.BlockSpec((1, 2 * D_FF), const),
            pl.BlockSpec((3, 2 * D_FF), const),
            pl.BlockSpec((1, 2 * D_FF), const),
            pl.BlockSpec((None, D_FF, D_MODEL), layer_block, **resident),
            pl.BlockSpec((1, D_MODEL), const),
        ],
        out_specs=pl.BlockSpec((TM, D_MODEL), lambda g: (g, 0)),
        out_shape=jax.ShapeDtypeStruct((n_groups * TM, D_MODEL), F32),
        scratch_shapes=[pltpu.VMEM((TM, D_MODEL), BF16), pltpu.VMEM((TM, D_MODEL), F32)]
        + [pltpu.VMEM((TM // FFN_SLAB * (FFN_SLAB + 2 * FFN_HALO), TF), F32)] * 4
        + [pltpu.VMEM((TM, TF), BF16)] * 2,
        compiler_params=_cparams(("arbitrary",)),
        name="conv_ffn",
    )(x, norm_g, mod, w_up, b_up.reshape(1, -1), w_dw, b_dw.reshape(1, -1), w_down, b_down.reshape(1, -1))


def _pair_queries(q):
    lane = lax.broadcasted_iota(jnp.int32, q.shape, 1)
    zero = jnp.zeros_like(q)
    return jnp.concatenate([jnp.where(lane < NA_HEAD_DIM, q, zero),
                            jnp.where(lane < NA_HEAD_DIM, zero, q)], axis=0)


def _pair_merge(o2):
    n = o2.shape[0] // 2
    lane = lax.broadcasted_iota(jnp.int32, (n, o2.shape[1]), 1)
    return jnp.where(lane < NA_HEAD_DIM, o2[:n], o2[n:])


def _nt_dot(a, b):
    return lax.dot_general(a, b, (((1,), (1,)), ((), ())), preferred_element_type=F32)


def _ctx_attn_kernel(q_ref, kt_ref, vt_ref, o_ref):
    for b in range(TM // SEQ):
        rows = slice(b * SEQ, (b + 1) * SEQ)
        q2 = _pair_queries(q_ref[rows, :])
        s = jnp.dot(q2, kt_ref[b].astype(BF16), preferred_element_type=F32)
        p = jnp.exp(s - jnp.max(s, axis=-1, keepdims=True))
        l = jnp.sum(p, axis=-1, keepdims=True)
        o2 = _nt_dot(p.astype(BF16), vt_ref[b].astype(BF16)) / l
        o_ref[rows, :] = _pair_merge(o2).astype(o_ref.dtype)


def _ctx_attention(q, kt, vt):
    spec = pl.BlockSpec((TM, 128), lambda g, hp: (g, hp))
    kv_spec = pl.BlockSpec((TM // SEQ, 128, SEQ), lambda g, hp: (g, hp, 0))
    return pl.pallas_call(
        _ctx_attn_kernel,
        grid=(N_CTX_GROUPS, NA_HEADS // 2),
        in_specs=[spec, kv_spec, kv_spec],
        out_specs=spec,
        out_shape=jax.ShapeDtypeStruct((N_CTX_TOK, D_MODEL), BF16),
        compiler_params=_cparams(("arbitrary", "arbitrary")),
        name="ctx_attention",
    )(q, kt, vt)


_NA_BLOCKS = ((0, 8, 0), (0, 12, 1), (4, 12, 1), (8, 8, 2))


def _na_attn_kernel(q_ref, k_ref, v_ref, ck_ref, cv_ref, ba_ref, bm_ref, bc_ref, o_ref):
    bias_refs = (ba_ref, bm_ref, bc_ref)
    ck = ck_ref[0, 0].astype(BF16)
    cv = cv_ref[0, 0].astype(BF16)
    for blk, (row0, nrows, bidx) in enumerate(_NA_BLOCKS):
        rows = slice(blk * 4 * GRID_W, (blk + 1) * 4 * GRID_W)
        keys = slice(row0 * GRID_W, (row0 + nrows) * GRID_W)
        q2 = _pair_queries(q_ref[rows, :])
        bias = bias_refs[bidx][...]
        s_loc = _nt_dot(q2, k_ref[keys, :]) + bias.reshape(2 * 4 * GRID_W, nrows * GRID_W)
        s_ctx = _nt_dot(q2, ck)
        m = jnp.maximum(jnp.max(s_loc, axis=-1, keepdims=True), jnp.max(s_ctx, axis=-1, keepdims=True))
        p_loc = jnp.exp(s_loc - m)
        p_ctx = jnp.exp(s_ctx - m)
        l = jnp.sum(p_loc, axis=-1, keepdims=True) + jnp.sum(p_ctx, axis=-1, keepdims=True)
        o2 = (jnp.dot(p_loc.astype(BF16), v_ref[keys, :], preferred_element_type=F32)
              + jnp.dot(p_ctx.astype(BF16), cv, preferred_element_type=F32)) / l
        o_ref[rows, :] = _pair_merge(o2).astype(o_ref.dtype)


N_RPB_R = 2 * NA_WIN_R - 1
N_RPB_C = 2 * NA_WIN_C - 1


def _na_bias_kernel(rpb_ref, ba_ref, bm_ref, bc_ref):
    base = pl.program_id(0) * (N_RPB_R * N_RPB_C)
    qc = lax.broadcasted_iota(jnp.int32, (GRID_W, 2 * GRID_W), 0)
    lane = lax.broadcasted_iota(jnp.int32, (GRID_W, 2 * GRID_W), 1)
    kc = lane & (GRID_W - 1)
    right = lane >= GRID_W
    dcol = kc - qc + (NA_WIN_C - 1)
    c_start = jnp.clip(qc - NA_WIN_C // 2, 0, GRID_W - NA_WIN_C)
    in_win = (kc >= c_start) & (kc < c_start + NA_WIN_C)
    neg = jnp.full((GRID_W, 2 * GRID_W), NEG_INF, F32)
    row_tiles, tiles = {}, {}

    def row_tile(dr):
        if dr not in row_tiles:
            acc = jnp.zeros((GRID_W, 2 * GRID_W), F32)
            if 0 <= dr < N_RPB_R:
                for j in range(N_RPB_C):
                    acc = jnp.where(dcol == j, rpb_ref[base + dr * N_RPB_C + j], acc)
            row_tiles[dr] = acc
        return row_tiles[dr]

    def pair_tile(dr):
        if dr not in tiles:
            tiles[dr] = jnp.where(right, row_tile(dr + 1), row_tile(dr))
        return tiles[dr]

    rows_total = DEC_SEQ // GRID_W
    for ref, blk in ((ba_ref, 0), (bm_ref, 1), (bc_ref, 3)):
        row0, nrows, _ = _NA_BLOCKS[blk]
        for rr in range(4):
            r = blk * 4 + rr
            r_start = min(max(r - NA_WIN_R // 2, 0), rows_total - NA_WIN_R)
            for ip in range(nrows // 2):
                krow = row0 + 2 * ip
                ok_l = r_start <= krow < r_start + NA_WIN_R
                ok_r = r_start <= krow + 1 < r_start + NA_WIN_R
                if ok_l or ok_r:
                    mask = in_win
                    if not ok_l:
                        mask = mask & right
                    if not ok_r:
                        mask = mask & jnp.logical_not(right)
                    tile = jnp.where(mask, pair_tile(krow - r + NA_WIN_R - 1), neg)
                else:
                    tile = neg
                ref[0, rr * GRID_W:(rr + 1) * GRID_W, ip * 2 * GRID_W:(ip + 1) * 2 * GRID_W] = tile


def _na_bias_tables(rpb):
    out = lambda n: pl.BlockSpec((1, 4 * GRID_W, n), lambda h: (h, 0, 0))
    shape = lambda n: jax.ShapeDtypeStruct((NA_HEADS, 4 * GRID_W, n), F32)
    return pl.pallas_call(
        _na_bias_kernel,
        grid=(NA_HEADS,),
        in_specs=[pl.BlockSpec(memory_space=pltpu.SMEM)],
        out_specs=[out(8 * GRID_W), out(12 * GRID_W), out(8 * GRID_W)],
        out_shape=[shape(8 * GRID_W), shape(12 * GRID_W), shape(8 * GRID_W)],
        compiler_params=_cparams(("arbitrary",)),
        name="na_bias",
    )(rpb.astype(F32).reshape(-1))


def _na_attention(q, k, v, cache_k, cache_v, layer_j, bias_tables):
    ba, bm, bc = bias_tables
    tok = pl.BlockSpec((TM, 128), lambda hp, b: (b, hp))
    cache = pl.BlockSpec((1, 1, PAST_LEN, 128), lambda hp, b: (b, layer_j, 0, hp))
    bias = lambda n: pl.BlockSpec((2, 4 * GRID_W, n), lambda hp, b: (hp, 0, 0))
    return pl.pallas_call(
        _na_attn_kernel,
        grid=(NA_HEADS // 2, DEC_BATCH),
        in_specs=[tok, tok, tok, cache, cache, bias(8 * GRID_W), bias(12 * GRID_W), bias(8 * GRID_W)],
        out_specs=tok,
        out_shape=jax.ShapeDtypeStruct((N_LAT_TOK, D_MODEL), BF16),
        compiler_params=_cparams(("arbitrary", "arbitrary")),
        name="na_attention",
    )(q, k, v, cache_k, cache_v, ba, bm, bc)


CONV_PAD = 16
CONV_ROWS = 128
CONV_LANES = 128


def _conv_tail_kernel(x_ref, u_ref, wdw_ref, bdw_ref, lng_ref, lnb_ref, w_ref, b_ref, mod_ref,
                      o_ref, pad_ref, a_ref):
    half = CONV_WIDTH // 2
    win = CONV_ROWS + 8
    zeros = jnp.zeros((CONV_PAD, D_MODEL), F32)

    def conv_sequence(base, seq_len):
        pad_ref[0:CONV_PAD, :] = zeros
        pad_ref[CONV_PAD + seq_len:2 * CONV_PAD + seq_len, :] = zeros
        pad_ref[CONV_PAD:CONV_PAD + seq_len, :] = u_ref[base:base + seq_len, :]

        def rows(i, carry):
            r0 = pl.multiple_of(i * CONV_ROWS, CONV_ROWS)
            strips = []
            for l0 in range(0, D_MODEL, CONV_LANES):
                lanes = slice(l0, l0 + CONV_LANES)
                acc = jnp.zeros((CONV_ROWS, CONV_LANES), F32) + bdw_ref[:, lanes]
                for b in range(8):
                    z = None
                    for a in range(-2, 2):
                        t = 8 * a + b + half
                        if not 0 <= t < CONV_WIDTH:
                            continue
                        term = wdw_ref[t:t + 1, lanes] * pad_ref[pl.ds(r0 + (CONV_PAD + 8 * a), win), lanes]
                        z = term if z is None else z + term
                    if b:
                        z = pltpu.roll(z, win - b, 0)
                    acc = acc + z[:CONV_ROWS]
                strips.append(acc)
            acc = jnp.concatenate(strips, axis=1)
            mu = jnp.mean(acc, axis=-1, keepdims=True)
            cen = acc - mu
            var = jnp.mean(cen * cen, axis=-1, keepdims=True)
            y = cen * lax.rsqrt(var + NORM_EPS) * lng_ref[...] + lnb_ref[...]
            a_ref[pl.ds(base + r0, CONV_ROWS), :] = (y * _sigmoid(y)).astype(BF16)
            return carry

        lax.fori_loop(0, seq_len // CONV_ROWS, rows, 0)

    @pl.when(pl.program_id(0) < N_CTX_GROUPS)
    def _():
        for s in range(TM // SEQ):
            conv_sequence(s * SEQ, SEQ)

    @pl.when(pl.program_id(0) >= N_CTX_GROUPS)
    def _():
        conv_sequence(0, DEC_SEQ)

    for rows in _row_slabs():
        r = jnp.dot(a_ref[rows, :], w_ref[...], preferred_element_type=F32) + b_ref[...]
        o_ref[rows, :] = x_ref[rows, :] + _mod_slice(mod_ref, 2) * r


def _conv_tail(x, u, w_dw, b_dw, ln_g, ln_b, w_pw2, b_pw2, mod):
    tok = pl.BlockSpec((TM, D_MODEL), lambda g: (g, 0))
    const = lambda g: (0, 0)
    vec = pl.BlockSpec((1, D_MODEL), const)
    return pl.pallas_call(
        _conv_tail_kernel,
        grid=(N_GROUPS,),
        in_specs=[tok, tok, pl.BlockSpec((CONV_WIDTH, D_MODEL), const), vec, vec, vec,
                  pl.BlockSpec((D_MODEL, D_MODEL), const), vec,
                  pl.BlockSpec((1, 1, 6 * D_MODEL), lambda g: (g, 0, 0))],
        out_specs=tok,
        out_shape=jax.ShapeDtypeStruct((N_TOK, D_MODEL), F32),
        scratch_shapes=[pltpu.VMEM((TM + 2 * CONV_PAD, D_MODEL), F32),
                        pltpu.VMEM((TM, D_MODEL), BF16)],
        compiler_params=_cparams(("arbitrary",)),
        name="conv_tail",
    )(x, u, w_dw, b_dw.reshape(1, -1), ln_g.reshape(1, -1), ln_b.reshape(1, -1), w_pw2,
      b_pw2.reshape(1, -1), mod)


def _log_sigmoid(z):
    return -(jnp.maximum(-z, 0.0) + jnp.log(1.0 + jnp.exp(-jnp.abs(z))))


def _gla_scan_kernel(*refs, seq_len, has_init, emit_state):
    it = iter(refs)
    q_ref, k_ref, v_ref, r_ref, w2_ref, bgk_ref = (next(it) for _ in range(6))
    s0_refs = (next(it), next(it)) if has_init else None
    o_ref = next(it)
    st_out = (next(it), next(it)) if emit_state else None
    g_ref, cum_ref, st_ref = (next(it) for _ in range(3))

    c = GLA_CHUNK
    n_chunks = seq_len // c
    rlow = r_ref[...].astype(BF16)
    row = lax.broadcasted_iota(jnp.int32, (c, 1), 0)
    sub8 = lax.broadcasted_iota(jnp.int32, (8, 1), 0)
    ri = lax.broadcasted_iota(jnp.int32, (c, c), 0)
    ci = lax.broadcasted_iota(jnp.int32, (c, c), 1)
    pair_xor = ri ^ ci

    for d in range(2):
        reverse = d == 1
        z = jnp.dot(rlow, w2_ref[d].astype(BF16), preferred_element_type=F32) + bgk_ref[d]
        g_ref[...] = _log_sigmoid(z) * (1.0 / GLA_GATE_NORM)
        if has_init:
            st_ref[...] = s0_refs[d][0, 0].T
        else:
            st_ref[...] = jnp.zeros_like(st_ref)

        order = range(n_chunks - 1, -1, -1) if reverse else range(n_chunks)
        for ch in order:
            base = ch * c
            rows = slice(base, base + c)
            q = q_ref[rows, :]
            k = k_ref[rows, :]
            v = v_ref[rows, :].astype(BF16)
            cum = g_ref[rows, :]
            sh = 1
            while sh < c:
                if reverse:
                    cum = cum + jnp.where(row < c - sh, pltpu.roll(cum, c - sh, 0), 0.0)
                else:
                    cum = cum + jnp.where(row >= sh, pltpu.roll(cum, sh, 0), 0.0)
                sh *= 2
            cum_ref[...] = cum

            att = None
            m = c
            while m >= 2:
                half = m // 2
                arow = half if reverse else half - 1
                if m >= 8:
                    pieces = [jnp.broadcast_to(cum_ref[b * m + arow:b * m + arow + 1, :], (m, GLA_DK))
                              for b in range(c // m)]
                else:
                    pieces = []
                    for t in range(c // 8):
                        tile = None
                        for b in range(8 // m - 1, -1, -1):
                            r = t * 8 + b * m + arow
                            cand = jnp.broadcast_to(cum_ref[r:r + 1, :], (8, GLA_DK))
                            tile = cand if tile is None else jnp.where(sub8 < (b + 1) * m, cand, tile)
                        pieces.append(tile)
                anchor = pieces[0] if len(pieces) == 1 else jnp.concatenate(pieces, axis=0)
                fac = jnp.exp(-jnp.abs(cum - anchor))
                p = _nt_dot((q * fac).astype(BF16), (k * fac).astype(BF16))
                att = p if att is None else jnp.where(pair_xor < m, p, att)
                m //= 2
            att = jnp.where(ri == ci, jnp.sum(q * k, axis=-1, keepdims=True), att)
            att = jnp.where((ri <= ci) if reverse else (ri >= ci), att, 0.0)

            last = 0 if reverse else c - 1
            total = cum_ref[last:last + 1, :]
            q_in = (q * jnp.exp(cum)).astype(BF16)
            k_out = (k * jnp.exp(total - cum)).astype(BF16)
            st = st_ref[...]
            o = (jnp.dot(att.astype(BF16), v, preferred_element_type=F32)
                 + _nt_dot(q_in, st.astype(BF16)))
            if reverse:
                o_ref[rows, :] += o
            else:
                o_ref[rows, :] = o
            kv = lax.dot_general(v, k_out, (((0,), (0,)), ((), ())), preferred_element_type=F32)
            st_ref[...] = st * jnp.exp(total) + kv

        if emit_state:
            st_out[d][0, 0] = st_ref[...].T


def _gla_scan(y, w2pad, b_gk, init_states, seq_len, blk0, n_seq, emit_state):
    has_init = init_states is not None
    qk = lambda off: pl.BlockSpec((seq_len, GLA_DK), lambda b, h: (b + blk0, h + off))
    state = pl.BlockSpec((1, 1, GLA_DK, GLA_DV), lambda b, h: (b, h, 0, 0))
    in_specs = [
        qk(0), qk(GLA_KEY_DIM // GLA_DK),
        pl.BlockSpec((seq_len, GLA_DV), lambda b, h: (b + blk0, h + 2 * GLA_KEY_DIM // GLA_DV)),
        pl.BlockSpec((seq_len, 128), lambda b, h: (b + blk0, (2 * GLA_KEY_DIM + 2 * GLA_VAL_DIM) // 128)),
        pl.BlockSpec((2, 128, GLA_DK), lambda b, h: (0, 0, h)),
        pl.BlockSpec((2, 1, GLA_DK), lambda b, h: (0, 0, h)),
    ]
    args = [y, y, y, y, w2pad, b_gk]
    if has_init:
        in_specs += [state, state]
        args += list(init_states)
    out_specs = [pl.BlockSpec((seq_len, GLA_DV), lambda b, h: (b, h))]
    out_shape = [jax.ShapeDtypeStruct((n_seq * seq_len, GLA_VAL_DIM), F32)]
    if emit_state:
        out_specs += [state, state]
        out_shape += [jax.ShapeDtypeStruct((n_seq, GLA_HEADS, GLA_DK, GLA_DV), F32)] * 2
    return pl.pallas_call(
        functools.partial(_gla_scan_kernel, seq_len=seq_len, has_init=has_init, emit_state=emit_state),
        grid=(n_seq, GLA_HEADS),
        in_specs=in_specs,
        out_specs=out_specs,
        out_shape=out_shape,
        scratch_shapes=[pltpu.VMEM((seq_len, GLA_DK), F32), pltpu.VMEM((GLA_CHUNK, GLA_DK), F32),
                        pltpu.VMEM((GLA_DV, GLA_DK), F32)],
        compiler_params=_cparams(("arbitrary", "arbitrary")),
        name="gla_scan",
    )(*args)


_GROUP_COND_ROW = np.array([0] * N_CTX_GROUPS + list(range(1, DEC_BATCH + 1)))


def kernel(x_prompt, x_sample, c, cache_attn_k, cache_attn_v, state_gla_fwd, state_gla_bwd, c_ctx,
           mod_w, mod_b, norm1_g, norm2_g,
           attn_w_qkv, attn_w_o, attn_q_norm, attn_k_norm, attn_rpb,
           conv_w_pw1, conv_b_pw1, conv_w_dw, conv_b_dw, conv_ln_g, conv_ln_b, conv_w_pw2, conv_b_pw2,
           gla_w_q, gla_w_k, gla_w_v, gla_w_g, gla_w_gk1, gla_w_gk2, gla_b_gk, gla_o_norm, gla_w_o,
           ffn_w_up, ffn_b_up, ffn_w_dw, ffn_b_dw, ffn_w_down, ffn_b_down):
    x_ctx0 = x_prompt.reshape(N_CTX_TOK, D_MODEL)
    x_lat0 = x_sample.reshape(N_LAT_TOK, D_MODEL)
    x = None
    cond =jnp.concatenate([c_ctx[None, :], c, jnp.zeros((N_COND - 1 - DEC_BATCH, D_MODEL), F32)], axis=0)
    mod_all = _modulation(cond, mod_w, mod_b)
    mod_all = mod_all[:, _GROUP_COND_ROW][:, :, None, :]

    cache_k = cache_attn_k.reshape(DEC_BATCH, -1, PAST_LEN, D_MODEL)
    cache_v = cache_attn_v.reshape(DEC_BATCH, -1, PAST_LEN, D_MODEL)
    ffn_w_up_bf16 = ffn_w_up.astype(BF16)
    ffn_w_down_bf16 = ffn_w_down.astype(BF16)
    new_k, new_v, new_sf, new_sb = [], [], [], []
    for i in range(DEPTH):
        kind, j = i % N_MIXERS, i // N_MIXERS
        mod = mod_all[i]
        n1 = norm1_g[i].reshape(1, D_MODEL)
        if kind == 0:
            w_qkv = attn_w_qkv[j].astype(BF16)
            gain = jnp.concatenate([jnp.tile(attn_q_norm[j], NA_HEADS) * (NA_HEAD_DIM ** -0.5),
                                    jnp.tile(attn_k_norm[j], NA_HEADS),
                                    jnp.ones((D_MODEL,), F32)]).reshape(1, 3 * D_MODEL)
            if x is None:
                x_ctx, x_lat, lat_block0 = x_ctx0, x_lat0, 0
            else:
                x_ctx, x_lat, lat_block0 = x, x, N_CTX_GROUPS
            n_lat_groups = N_GROUPS - N_CTX_GROUPS
            qp, kp, vp = _qkv_proj(x_ctx, n1, mod, w_qkv, gain, 0, 0, N_CTX_GROUPS, True)
            qs, ks, vs = _qkv_proj(x_lat, n1, mod, w_qkv, gain, lat_block0, N_CTX_GROUPS, n_lat_groups, False)
            o_ctx = _ctx_attention(qp, kp, vp)
            o_lat = _na_attention(qs, ks, vs, cache_k, cache_v, j, _na_bias_tables(attn_rpb[j]))
            x = _out_proj(x_ctx, x_lat, o_ctx, o_lat, attn_w_o[j].astype(BF16), mod)
            new_k.append(kp)
            new_v.append(vp)
        elif kind == 1:
            u = _glu_proj(x, n1, mod, conv_w_pw1[j].astype(BF16), conv_b_pw1[j])
            x = _conv_tail(x, u, conv_w_dw[j], conv_b_dw[j], conv_ln_g[j], conv_ln_b[j],
                           conv_w_pw2[j].astype(BF16), conv_b_pw2[j], mod)
        else:
            pad = GLA_PROJ_N - 2 * GLA_KEY_DIM - 2 * GLA_VAL_DIM - 2 * GLA_GATE_RANK
            w_cat = jnp.concatenate([gla_w_q[j], gla_w_k[j], gla_w_v[j], gla_w_g[j], gla_w_gk1[j, 0],
                                     gla_w_gk1[j, 1], jnp.zeros((D_MODEL, pad), F32)], axis=1).astype(BF16)
            colscale = jnp.concatenate([jnp.full((GLA_KEY_DIM,), GLA_DK ** -0.5, F32),
                                        jnp.ones((GLA_PROJ_N - GLA_KEY_DIM,), F32)]).reshape(1, GLA_PROJ_N)
            y = _scaled_proj(x, n1, mod, w_cat, colscale)
            w2pad = jnp.zeros((2, 128, GLA_KEY_DIM), F32)
            w2pad = w2pad.at[0, :GLA_GATE_RANK].set(gla_w_gk2[j, 0])
            w2pad = w2pad.at[1, GLA_GATE_RANK:2 * GLA_GATE_RANK].set(gla_w_gk2[j, 1])
            b_gk = gla_b_gk[j].reshape(2, 1, GLA_KEY_DIM)
            o_ctx, sf, sb = _gla_scan(y, w2pad, b_gk, None, SEQ, 0, BATCH, True)
            (o_lat,) = _gla_scan(y, w2pad, b_gk, (state_gla_fwd[:, j], state_gla_bwd[:, j]),
                                 DEC_SEQ, N_CTX_GROUPS, DEC_BATCH, False)
            x = _gla_out_proj(x, o_ctx, o_lat, y, gla_o_norm[j].reshape(1, GLA_DV),
                              gla_w_o[j].astype(BF16), mod)
            new_sf.append(sf)
            new_sb.append(sb)
        ffn_args = (norm2_g[i].reshape(1, D_MODEL), mod, i, ffn_w_up_bf16, ffn_b_up[i],
                    ffn_w_dw[i], ffn_b_dw[i], ffn_w_down_bf16, ffn_b_down[i])
        if i < DEPTH - 1:
            x = _ffn(x, *ffn_args)
        else:
            y_ctx = _ffn(x, *ffn_args, group0=0, n_groups=N_CTX_GROUPS)
            y_lat = _ffn(x, *ffn_args, group0=N_CTX_GROUPS, n_groups=N_GROUPS - N_CTX_GROUPS)

    y_prompt = y_ctx.reshape(BATCH, SEQ, D_MODEL)
    y_sample = y_lat.reshape(DEC_BATCH, DEC_SEQ, D_MODEL)
    def cache_layout(parts):
        stacked = jnp.stack(parts, axis=1).reshape(BATCH, len(parts), NA_HEADS, NA_HEAD_DIM, SEQ)
        return stacked.transpose(0, 1, 4, 2, 3)

    return (y_prompt, y_sample, cache_layout(new_k), cache_layout(new_v),
            jnp.stack(new_sf, axis=1), jnp.stack(new_sb, axis=1))
```

```python
import functools

import numpy as np
import jax
import jax.numpy as jnp
from jax import lax
from jax.experimental import pallas as pl
from jax.experimental.pallas import tpu as pltpu

F32 = jnp.float32
BF16 = jnp.bfloat16

D_MODEL = 1024
BATCH = 16
SEQ = 256
DEPTH = 4
DEC_BATCH = 4
DEC_SEQ = 1024
PAST_LEN = 512
GRID_W = 64
N_MIXERS = 3
NA_HEADS = 16
NA_HEAD_DIM = 64
NA_WIN_R = 8
NA_WIN_C = 16
CONV_WIDTH = 31
GLA_HEADS = 4
GLA_KEY_DIM = 512
GLA_VAL_DIM = 1024
GLA_DK = 128
GLA_DV = 256
GLA_GATE_RANK = 16
GLA_GATE_NORM = 16.0
D_FF = 2816
NORM_EPS = 1e-6
NEG_INF = -1e30

TM = 1024
N_CTX_TOK = BATCH * SEQ
N_LAT_TOK = DEC_BATCH * DEC_SEQ
N_TOK = N_CTX_TOK + N_LAT_TOK
N_CTX_GROUPS = N_CTX_TOK // TM
N_GROUPS = N_TOK // TM
N_COND = 8
TF = 256
PROJ_SLAB = 256
FFN_SLAB = SEQ
FFN_HALO = 8
FFN_STEP_SLABS = 4
GLA_CHUNK = 256
GLA_PROJ_N = 3328
VMEM_LIMIT = 60 * 1024 * 1024


def _cparams(sem, flags=None):
    return pltpu.CompilerParams(dimension_semantics=sem, vmem_limit_bytes=VMEM_LIMIT, flags=flags)


def _sigmoid(x):
    return 1.0 / (1.0 + jnp.exp(-x))


def _norm_mod(x, g, shift, scale):
    ms = jnp.mean(x * x, axis=-1, keepdims=True)
    y = x * lax.rsqrt(ms + NORM_EPS) * g
    return y * (1.0 + scale) + shift


def _mod_slice(mod_ref, idx):
    return mod_ref[0, :, idx * D_MODEL:(idx + 1) * D_MODEL]


def _mod_kernel(cond_ref, w_ref, b_ref, o_ref):
    c = cond_ref[...]
    s = (c * _sigmoid(c)).astype(BF16)
    o_ref[0] = jnp.dot(s, w_ref[0].astype(BF16), preferred_element_type=F32) + b_ref[0]


def _modulation(cond, mod_w, mod_b):
    tn = 1536
    n = 6 * D_MODEL
    return pl.pallas_call(
        _mod_kernel,
        grid=(DEPTH, n // tn),
        in_specs=[
            pl.BlockSpec((N_COND, D_MODEL), lambda l, j: (0, 0)),
            pl.BlockSpec((1, D_MODEL, tn), lambda l, j: (l, 0, j)),
            pl.BlockSpec((1, 1, tn), lambda l, j: (l, 0, j)),
        ],
        out_specs=pl.BlockSpec((1, N_COND, tn), lambda l, j: (l, 0, j)),
        out_shape=jax.ShapeDtypeStruct((DEPTH, N_COND, n), F32),
        compiler_params=_cparams(("arbitrary", "arbitrary")),
        name="modulation",
    )(cond, mod_w, mod_b.reshape(DEPTH, 1, n))


def _proj_slabs(x_ref, g_ref, mod_ref, h_ref, epilogue):
    def run(first):
        for s in range(TM // PROJ_SLAB):
            rows = slice(s * PROJ_SLAB, (s + 1) * PROJ_SLAB)
            if first:
                h = _norm_mod(x_ref[rows, :], g_ref[...], _mod_slice(mod_ref, 0), _mod_slice(mod_ref, 1))
                h_ref[rows, :] = h.astype(BF16)
            epilogue(rows, h_ref[rows, :])

    pl.when(pl.program_id(1) == 0)(lambda: run(True))
    pl.when(pl.program_id(1) != 0)(lambda: run(False))


def _head_rms(acc, gain, hsum_ref, hexp_ref):
    ms = jnp.dot((acc * acc).astype(BF16), hsum_ref[...], preferred_element_type=F32)
    inv = lax.rsqrt(ms + NORM_EPS)
    hi = inv.astype(BF16)
    lo = (inv - hi.astype(F32)).astype(BF16)
    inv_full = jnp.dot(jnp.concatenate([hi, lo], axis=1), hexp_ref[...], preferred_element_type=F32)
    return acc * inv_full * gain


def _qkv_kernel(x_ref, g_ref, mod_ref, w_ref, gain_ref, hsum_ref, hexp_ref,
                q_ref, k_ref, v_ref, h_ref, t_ref, *, kv_channel_major):
    j = pl.program_id(1)

    def column_step(out_ref, normed, first, channel_major):
        for s in range(TM // PROJ_SLAB):
            rows = slice(s * PROJ_SLAB, (s + 1) * PROJ_SLAB)
            if first:
                h = _norm_mod(x_ref[rows, :], g_ref[...], _mod_slice(mod_ref, 0), _mod_slice(mod_ref, 1))
                h_ref[rows, :] = h.astype(BF16)
            acc = jnp.dot(h_ref[rows, :], w_ref[...], preferred_element_type=F32)
            if normed:
                acc = _head_rms(acc, gain_ref[...], hsum_ref, hexp_ref)
            if channel_major:
                t_ref[...] = acc
                out_ref[s] = t_ref[...].T
            else:
                out_ref[rows, :] = acc.astype(out_ref.dtype)

    pl.when(j == 0)(lambda: column_step(q_ref, True, True, False))
    pl.when(j == 1)(lambda: column_step(k_ref, True, False, kv_channel_major))
    pl.when(j == 2)(lambda: column_step(v_ref, False, False, kv_channel_major))


def _head_matrices():
    lane = np.arange(D_MODEL)
    hsum = np.zeros((D_MODEL, 128), np.float32)
    hsum[lane, lane // NA_HEAD_DIM] = 1.0 / NA_HEAD_DIM
    hexp = np.zeros((128, D_MODEL), np.float32)
    hexp[lane // NA_HEAD_DIM, lane] = 1.0
    return jnp.asarray(hsum, BF16), jnp.asarray(np.concatenate([hexp, hexp], 0), BF16)


def _qkv_proj(x, norm_g, mod, w_qkv, gain, x_group0, mod_group0, n_groups, kv_channel_major):
    hsum, hexp = _head_matrices()
    ntok = n_groups * TM
    row = lambda g, j: (g, 0)
    tok_spec = pl.BlockSpec((TM, D_MODEL), row)
    if kv_channel_major:
        kv_spec = pl.BlockSpec((TM // SEQ, D_MODEL, SEQ), lambda g, j: (g, 0, 0))
        kv_shape = jax.ShapeDtypeStruct((ntok // SEQ, D_MODEL, SEQ), F32)
    else:
        kv_spec = tok_spec
        kv_shape = jax.ShapeDtypeStruct((ntok, D_MODEL), BF16)
    return pl.pallas_call(
        functools.partial(_qkv_kernel, kv_channel_major=kv_channel_major),
        grid=(n_groups, 3),
        in_specs=[
            pl.BlockSpec((TM, D_MODEL), lambda g, j: (g + x_group0, 0)),
            pl.BlockSpec((1, D_MODEL), lambda g, j: (0, 0)),
            pl.BlockSpec((1, 1, 6 * D_MODEL), lambda g, j: (g + mod_group0, 0, 0)),
            pl.BlockSpec((D_MODEL, D_MODEL), lambda g, j: (0, j)),
            pl.BlockSpec((1, D_MODEL), lambda g, j: (0, j)),
            pl.BlockSpec((D_MODEL, 128), lambda g, j: (0, 0)),
            pl.BlockSpec((256, D_MODEL), lambda g, j: (0, 0)),
        ],
        out_specs=[tok_spec, kv_spec, kv_spec],
        out_shape=[jax.ShapeDtypeStruct((ntok, D_MODEL), BF16), kv_shape, kv_shape],
        scratch_shapes=[pltpu.VMEM((TM, D_MODEL), BF16), pltpu.VMEM((PROJ_SLAB, D_MODEL), F32)],
        compiler_params=_cparams(("arbitrary", "arbitrary")),
        name="qkv_proj",
    )(x, norm_g, mod, w_qkv, gain, hsum, hexp)


def _glu_kernel(x_ref, g_ref, mod_ref, wa_ref, wg_ref, ba_ref, bg_ref, u_ref, h_ref):
    def epilogue(rows, h):
        a = jnp.dot(h, wa_ref[...], preferred_element_type=F32) + ba_ref[...]
        g = jnp.dot(h, wg_ref[...], preferred_element_type=F32) + bg_ref[...]
        u_ref[rows, :] = a * _sigmoid(g)

    _proj_slabs(x_ref, g_ref, mod_ref, h_ref, epilogue)


def _glu_proj(x, norm_g, mod, w_pw1, b_pw1):
    tn = 512
    nj = D_MODEL // tn
    b = b_pw1.reshape(1, 2 * D_MODEL)
    return pl.pallas_call(
        _glu_kernel,
        grid=(N_GROUPS, nj),
        in_specs=[
            pl.BlockSpec((TM, D_MODEL), lambda g, j: (g, 0)),
            pl.BlockSpec((1, D_MODEL), lambda g, j: (0, 0)),
            pl.BlockSpec((1, 1, 6 * D_MODEL), lambda g, j: (g, 0, 0)),
            pl.BlockSpec((D_MODEL, tn), lambda g, j: (0, j)),
            pl.BlockSpec((D_MODEL, tn), lambda g, j: (0, j + nj)),
            pl.BlockSpec((1, tn), lambda g, j: (0, j)),
            pl.BlockSpec((1, tn), lambda g, j: (0, j + nj)),
        ],
        out_specs=pl.BlockSpec((TM, tn), lambda g, j: (g, j)),
        out_shape=jax.ShapeDtypeStruct((N_TOK, D_MODEL), F32),
        scratch_shapes=[pltpu.VMEM((TM, D_MODEL), BF16)],
        compiler_params=_cparams(("arbitrary", "arbitrary")),
        name="glu_proj",
    )(x, norm_g, mod, w_pw1, w_pw1, b, b)


def _scaled_proj_kernel(x_ref, g_ref, mod_ref, w_ref, cs_ref, y_ref, h_ref):
    def epilogue(rows, h):
        y_ref[rows, :] = jnp.dot(h, w_ref[...], preferred_element_type=F32) * cs_ref[...]

    _proj_slabs(x_ref, g_ref, mod_ref, h_ref, epilogue)


def _scaled_proj(x, norm_g, mod, w, colscale):
    n = w.shape[1]
    tn = n // 2
    return pl.pallas_call(
        _scaled_proj_kernel,
        grid=(N_GROUPS, n // tn),
        in_specs=[
            pl.BlockSpec((TM, D_MODEL), lambda g, j: (g, 0)),
            pl.BlockSpec((1, D_MODEL), lambda g, j: (0, 0)),
            pl.BlockSpec((1, 1, 6 * D_MODEL), lambda g, j: (g, 0, 0)),
            pl.BlockSpec((D_MODEL, tn), lambda g, j: (0, j)),
            pl.BlockSpec((1, tn), lambda g, j: (0, j)),
        ],
        out_specs=pl.BlockSpec((TM, tn), lambda g, j: (g, j)),
        out_shape=jax.ShapeDtypeStruct((N_TOK, n), F32),
        scratch_shapes=[pltpu.VMEM((TM, D_MODEL), BF16)],
        compiler_params=_cparams(("arbitrary", "arbitrary")),
        name="gla_proj",
    )(x, norm_g, mod, w, colscale)


_CTX_PART = pl.BlockSpec((TM, D_MODEL), lambda g: (jnp.minimum(g, N_CTX_GROUPS - 1), 0))
_LAT_PART = pl.BlockSpec((TM, D_MODEL), lambda g: (jnp.maximum(g - N_CTX_GROUPS, 0), 0))


def _group_part(ctx_ref, lat_ref, rows):
    return jnp.where(pl.program_id(0) < N_CTX_GROUPS, ctx_ref[rows, :], lat_ref[rows, :])


def _row_slabs():
    return [slice(s * PROJ_SLAB, (s + 1) * PROJ_SLAB) for s in range(TM // PROJ_SLAB)]


def _oproj_kernel(x_ctx_ref, x_lat_ref, a_ctx_ref, a_lat_ref, w_ref, mod_ref, o_ref):
    for rows in _row_slabs():
        r = jnp.dot(_group_part(a_ctx_ref, a_lat_ref, rows), w_ref[...], preferred_element_type=F32)
        o_ref[rows, :] = _group_part(x_ctx_ref, x_lat_ref, rows) + _mod_slice(mod_ref, 2) * r


def _out_proj(x_ctx, x_lat, a_ctx, a_lat, w, mod):
    if x_lat is x_ctx:
        x_lat_spec = pl.BlockSpec((TM, D_MODEL), lambda g: (jnp.maximum(g, N_CTX_GROUPS), 0))
    else:
        x_lat_spec = _LAT_PART
    return pl.pallas_call(
        _oproj_kernel,
        grid=(N_GROUPS,),
        in_specs=[
            _CTX_PART, x_lat_spec, _CTX_PART, _LAT_PART,
            pl.BlockSpec((D_MODEL, D_MODEL), lambda g: (0, 0)),
            pl.BlockSpec((1, 1, 6 * D_MODEL), lambda g: (g, 0, 0)),
        ],
        out_specs=pl.BlockSpec((TM, D_MODEL), lambda g: (g, 0)),
        out_shape=jax.ShapeDtypeStruct((N_TOK, D_MODEL), F32),
        compiler_params=_cparams(("arbitrary",)),
        name="out_proj",
    )(x_ctx, x_lat, a_ctx, a_lat, w, mod)


def _gla_oproj_kernel(x_ref, o_ctx_ref, o_lat_ref, gz_ref, og_ref, w_ref, mod_ref, out_ref):
    og = og_ref[...]
    for rows in _row_slabs():
        o = _group_part(o_ctx_ref, o_lat_ref, rows)
        parts = []
        for h in range(GLA_HEADS):
            oh = o[:, h * GLA_DV:(h + 1) * GLA_DV]
            ms = jnp.mean(oh * oh, axis=-1, keepdims=True)
            parts.append(oh * lax.rsqrt(ms + NORM_EPS) * og)
        y = jnp.concatenate(parts, axis=1)
        gz = gz_ref[rows, :]
        a = (y * (gz * _sigmoid(gz))).astype(BF16)
        r = jnp.dot(a, w_ref[...], preferred_element_type=F32)
        out_ref[rows, :] = x_ref[rows, :] + _mod_slice(mod_ref, 2) * r


def _gla_out_proj(x, o_ctx, o_lat, y, o_norm, w, mod):
    return pl.pallas_call(
        _gla_oproj_kernel,
        grid=(N_GROUPS,),
        in_specs=[
            pl.BlockSpec((TM, D_MODEL), lambda g: (g, 0)),
            _CTX_PART, _LAT_PART,
            pl.BlockSpec((TM, GLA_VAL_DIM), lambda g: (g, 2)),
            pl.BlockSpec((1, GLA_DV), lambda g: (0, 0)),
            pl.BlockSpec((GLA_VAL_DIM, D_MODEL), lambda g: (0, 0)),
            pl.BlockSpec((1, 1, 6 * D_MODEL), lambda g: (g, 0, 0)),
        ],
        out_specs=pl.BlockSpec((TM, D_MODEL), lambda g: (g, 0)),
        out_shape=jax.ShapeDtypeStruct((N_TOK, D_MODEL), F32),
        compiler_params=_cparams(("arbitrary",)),
        name="gla_out_proj",
    )(x, o_ctx, o_lat, y, o_norm, w, mod)


def _ffn_kernel(x_ref, g_ref, mod_ref, wup_ref, bup_ref, wdw_ref, bdw_ref, wdn_ref, bdn_ref,
                out_ref, h_ref, acc_ref, ua0_ref, ug0_ref, ua1_ref, ug1_ref, act_ref,
                *, group0):
    is_lat = pl.program_id(0) + group0 >= N_CTX_GROUPS
    up_bufs = ((ua0_ref, ug0_ref), (ua1_ref, ug1_ref))
    n_slabs = TM // FFN_SLAB
    win = FFN_SLAB + 2 * FFN_HALO

    def chunk_cols(c):
        return pl.multiple_of(c * TF, TF), pl.multiple_of(D_FF + c * TF, TF)

    def step_rows(s0):
        return slice(s0 * FFN_SLAB, (s0 + FFN_STEP_SLABS) * FFN_SLAB)

    def up_step(c, slot, s0):
        h = h_ref[step_rows(s0), :]
        for buf, col in zip(up_bufs[slot], chunk_cols(c)):
            both = jnp.dot(h, wup_ref[:, pl.ds(col, TF)], preferred_element_type=F32)
            pad = jnp.broadcast_to(-bup_ref[:, pl.ds(col, TF)], (FFN_HALO, TF))
            for u in range(FFN_STEP_SLABS):
                s = s0 + u
                res = both[u * FFN_SLAB:(u + 1) * FFN_SLAB]
                base = s * win
                buf[base + FFN_HALO:base + FFN_HALO + FFN_SLAB, :] = res
                if s == 0:
                    buf[0:FFN_HALO, :] = pad
                else:
                    buf[base - FFN_HALO:base, :] = jnp.where(is_lat, res[:FFN_HALO], pad)
                if s == n_slabs - 1:
                    buf[base + win - FFN_HALO:base + win, :] = pad
                else:
                    buf[base + win:base + win + FFN_HALO, :] = jnp.where(
                        is_lat, res[FFN_SLAB - FFN_HALO:], pad)

    def conv_slab(buf, col, s):
        r0 = s * win + FFN_HALO
        prev = buf[r0 - 1:r0 - 1 + FFN_SLAB, :]
        mid = buf[r0:r0 + FFN_SLAB, :]
        nxt = buf[r0 + 1:r0 + 1 + FFN_SLAB, :]
        w = wdw_ref[:, pl.ds(col, TF)]
        bias = bdw_ref[:, pl.ds(col, TF)] + bup_ref[:, pl.ds(col, TF)] * (w[0:1] + w[1:2] + w[2:3])
        return w[0:1] * prev + w[1:2] * mid + w[2:3] * nxt + bias

    def gate_slab(c, slot, s):
        ca, cg = chunk_cols(c)
        a = conv_slab(up_bufs[slot][0], ca, s)
        g = conv_slab(up_bufs[slot][1], cg, s)
        act = (g * _sigmoid(g) * a).astype(BF16)
        act_ref[s * FFN_SLAB:(s + 1) * FFN_SLAB, slot * TF:(slot + 1) * TF] = act

    def gate_down_step(c, slot, s0, last=False):
        for u in range(FFN_STEP_SLABS):
            gate_slab(c, slot, s0 + u)
        rows = step_rows(s0)
        if last:
            acc_ref[rows, :] += jnp.dot(act_ref[rows, :TF], wdn_ref[pl.ds(chunk_cols(c)[0], TF), :],
                                        preferred_element_type=F32)
        elif slot == 1:
            pair0 = pl.multiple_of((c - 1) * TF, TF)
            acc_ref[rows, :] += jnp.dot(act_ref[rows, :], wdn_ref[pl.ds(pair0, 2 * TF), :],
                                        preferred_element_type=F32)

    steps = range(0, n_slabs, FFN_STEP_SLABS)

    n_chunks = D_FF // TF
    acc_ref[...] = jnp.zeros_like(acc_ref)
    for s0 in steps:
        rows = step_rows(s0)
        h = _norm_mod(x_ref[rows, :], g_ref[...], _mod_slice(mod_ref, 3), _mod_slice(mod_ref, 4))
        h_ref[rows, :] = h.astype(BF16)
        up_step(0, 0, s0)

    def body(i, carry):
        for slot in range(2):
            c = 2 * i + slot
            for s0 in steps:
                up_step(c + 1, 1 - slot, s0)
                gate_down_step(c, slot, s0)
        return carry

    lax.fori_loop(0, (n_chunks - 1) // 2, body, 0)
    for s0 in steps:
        rows = step_rows(s0)
        gate_down_step(n_chunks - 1, 0, s0, last=True)
        out_ref[rows, :] = x_ref[rows, :] + _mod_slice(mod_ref, 5) * (acc_ref[rows, :] + bdn_ref[...])


def _ffn(x, norm_g, mod, layer, w_up, b_up, w_dw, b_dw, w_down, b_down, group0=0, n_groups=N_GROUPS):
    const = lambda g: (0, 0)
    layer_block = lambda g: (layer, 0, 0)
    resident = dict(pipeline_mode=pl.Buffered(1))
    return pl.pallas_call(
        functools.partial(_ffn_kernel, group0=group0),
        grid=(n_groups,),
        in_specs=[
            pl.BlockSpec((TM, D_MODEL), lambda g: (g + group0, 0)),
            pl.BlockSpec((1, D_MODEL), const),
            pl.BlockSpec((1, 1, 6 * D_MODEL), lambda g: (g + group0, 0, 0)),
            pl.BlockSpec((None, D_MODEL, 2 * D_FF), layer_block, **resident),
            pl.BlockSpec((1, 2 * D_FF), const),
            pl.BlockSpec((3, 2 * D_FF), const),
            pl.BlockSpec((1, 2 * D_FF), const),
            pl.BlockSpec((None, D_FF, D_MODEL), layer_block, **resident),
            pl.BlockSpec((1, D_MODEL), const),
        ],
        out_specs=pl.BlockSpec((TM, D_MODEL), lambda g: (g, 0)),
        out_shape=jax.ShapeDtypeStruct((n_groups * TM, D_MODEL), F32),
        scratch_shapes=[pltpu.VMEM((TM, D_MODEL), BF16), pltpu.VMEM((TM, D_MODEL), F32)]
        + [pltpu.VMEM((TM // FFN_SLAB * (FFN_SLAB + 2 * FFN_HALO), TF), F32)] * 4
        + [pltpu.VMEM((TM, 2 * TF), BF16)],
        compiler_params=_cparams(("arbitrary",)),
        name="conv_ffn",
    )(x, norm_g, mod, w_up, b_up.reshape(1, -1), w_dw, b_dw.reshape(1, -1), w_down, b_down.reshape(1, -1))


def _pair_queries(q):
    lane = lax.broadcasted_iota(jnp.int32, q.shape, 1)
    zero = jnp.zeros_like(q)
    return jnp.concatenate([jnp.where(lane < NA_HEAD_DIM, q, zero),
                            jnp.where(lane < NA_HEAD_DIM, zero, q)], axis=0)


def _pair_merge(o2):
    n = o2.shape[0] // 2
    lane = lax.broadcasted_iota(jnp.int32, (n, o2.shape[1]), 1)
    return jnp.where(lane < NA_HEAD_DIM, o2[:n], o2[n:])


def _nt_dot(a, b):
    return lax.dot_general(a, b, (((1,), (1,)), ((), ())), preferred_element_type=F32)


def _ctx_attn_kernel(q_ref, kt_ref, vt_ref, o_ref):
    for b in range(TM // SEQ):
        rows = slice(b * SEQ, (b + 1) * SEQ)
        q2 = _pair_queries(q_ref[rows, :])
        s = jnp.dot(q2, kt_ref[b].astype(BF16), preferred_element_type=F32)
        p = jnp.exp(s - jnp.max(s, axis=-1, keepdims=True))
        l = jnp.sum(p, axis=-1, keepdims=True)
        o2 = _nt_dot(p.astype(BF16), vt_ref[b].astype(BF16)) / l
        o_ref[rows, :] = _pair_merge(o2).astype(o_ref.dtype)


def _ctx_attention(q, kt, vt):
    spec = pl.BlockSpec((TM, 128), lambda g, hp: (g, hp))
    kv_spec = pl.BlockSpec((TM // SEQ, 128, SEQ), lambda g, hp: (g, hp, 0))
    return pl.pallas_call(
        _ctx_attn_kernel,
        grid=(N_CTX_GROUPS, NA_HEADS // 2),
        in_specs=[spec, kv_spec, kv_spec],
        out_specs=spec,
        out_shape=jax.ShapeDtypeStruct((N_CTX_TOK, D_MODEL), BF16),
        compiler_params=_cparams(("arbitrary", "arbitrary")),
        name="ctx_attention",
    )(q, kt, vt)


_NA_BLOCKS = ((0, 8, 0), (0, 12, 1), (4, 12, 1), (8, 8, 2))


def _na_attn_kernel(q_ref, k_ref, v_ref, ck_ref, cv_ref, ba_ref, bm_ref, bc_ref, o_ref):
    bias_refs = (ba_ref, bm_ref, bc_ref)
    ck = ck_ref[0, 0].astype(BF16)
    cv = cv_ref[0, 0].astype(BF16)
    for blk, (row0, nrows, bidx) in enumerate(_NA_BLOCKS):
        rows = slice(blk * 4 * GRID_W, (blk + 1) * 4 * GRID_W)
        keys = slice(row0 * GRID_W, (row0 + nrows) * GRID_W)
        q2 = _pair_queries(q_ref[rows, :])
        bias = bias_refs[bidx][...]
        s_loc = _nt_dot(q2, k_ref[keys, :]) + bias.reshape(2 * 4 * GRID_W, nrows * GRID_W)
        s_ctx = _nt_dot(q2, ck)
        m = jnp.maximum(jnp.max(s_loc, axis=-1, keepdims=True), jnp.max(s_ctx, axis=-1, keepdims=True))
        p_loc = jnp.exp(s_loc - m)
        p_ctx = jnp.exp(s_ctx - m)
        l = jnp.sum(p_loc, axis=-1, keepdims=True) + jnp.sum(p_ctx, axis=-1, keepdims=True)
        o2 = (jnp.dot(p_loc.astype(BF16), v_ref[keys, :], preferred_element_type=F32)
              + jnp.dot(p_ctx.astype(BF16), cv, preferred_element_type=F32)) / l
        o_ref[rows, :] = _pair_merge(o2).astype(o_ref.dtype)


N_RPB_R = 2 * NA_WIN_R - 1
N_RPB_C = 2 * NA_WIN_C - 1


def _na_bias_kernel(rpb_ref, ba_ref, bm_ref, bc_ref):
    base = pl.program_id(0) * (N_RPB_R * N_RPB_C)
    qc = lax.broadcasted_iota(jnp.int32, (GRID_W, 2 * GRID_W), 0)
    lane = lax.broadcasted_iota(jnp.int32, (GRID_W, 2 * GRID_W), 1)
    kc = lane & (GRID_W - 1)
    right = lane >= GRID_W
    dcol = kc - qc + (NA_WIN_C - 1)
    c_start = jnp.clip(qc - NA_WIN_C // 2, 0, GRID_W - NA_WIN_C)
    in_win = (kc >= c_start) & (kc < c_start + NA_WIN_C)
    neg = jnp.full((GRID_W, 2 * GRID_W), NEG_INF, F32)
    row_tiles, tiles = {}, {}

    def row_tile(dr):
        if dr not in row_tiles:
            acc = jnp.zeros((GRID_W, 2 * GRID_W), F32)
            if 0 <= dr < N_RPB_R:
                for j in range(N_RPB_C):
                    acc = jnp.where(dcol == j, rpb_ref[base + dr * N_RPB_C + j], acc)
            row_tiles[dr] = acc
        return row_tiles[dr]

    def pair_tile(dr):
        if dr not in tiles:
            tiles[dr] = jnp.where(right, row_tile(dr + 1), row_tile(dr))
        return tiles[dr]

    rows_total = DEC_SEQ // GRID_W
    for ref, blk in ((ba_ref, 0), (bm_ref, 1), (bc_ref, 3)):
        row0, nrows, _ = _NA_BLOCKS[blk]
        for rr in range(4):
            r = blk * 4 + rr
            r_start = min(max(r - NA_WIN_R // 2, 0), rows_total - NA_WIN_R)
            for ip in range(nrows // 2):
                krow = row0 + 2 * ip
                ok_l = r_start <= krow < r_start + NA_WIN_R
                ok_r = r_start <= krow + 1 < r_start + NA_WIN_R
                if ok_l or ok_r:
                    mask = in_win
                    if not ok_l:
                        mask = mask & right
                    if not ok_r:
                        mask = mask & jnp.logical_not(right)
                    tile = jnp.where(mask, pair_tile(krow - r + NA_WIN_R - 1), neg)
                else:
                    tile = neg
                ref[0, rr * GRID_W:(rr + 1) * GRID_W, ip * 2 * GRID_W:(ip + 1) * 2 * GRID_W] = tile


def _na_bias_tables(rpb):
    out = lambda n: pl.BlockSpec((1, 4 * GRID_W, n), lambda h: (h, 0, 0))
    shape = lambda n: jax.ShapeDtypeStruct((NA_HEADS, 4 * GRID_W, n), F32)
    return pl.pallas_call(
        _na_bias_kernel,
        grid=(NA_HEADS,),
        in_specs=[pl.BlockSpec(memory_space=pltpu.SMEM)],
        out_specs=[out(8 * GRID_W), out(12 * GRID_W), out(8 * GRID_W)],
        out_shape=[shape(8 * GRID_W), shape(12 * GRID_W), shape(8 * GRID_W)],
        compiler_params=_cparams(("arbitrary",)),
        name="na_bias",
    )(rpb.astype(F32).reshape(-1))


def _na_attention(q, k, v, cache_k, cache_v, layer_j, bias_tables):
    ba, bm, bc = bias_tables
    tok = pl.BlockSpec((TM, 128), lambda hp, b: (b, hp))
    cache = pl.BlockSpec((1, 1, PAST_LEN, 128), lambda hp, b: (b, layer_j, 0, hp))
    bias = lambda n: pl.BlockSpec((2, 4 * GRID_W, n), lambda hp, b: (hp, 0, 0))
    return pl.pallas_call(
        _na_attn_kernel,
        grid=(NA_HEADS // 2, DEC_BATCH),
        in_specs=[tok, tok, tok, cache, cache, bias(8 * GRID_W), bias(12 * GRID_W), bias(8 * GRID_W)],
        out_specs=tok,
        out_shape=jax.ShapeDtypeStruct((N_LAT_TOK, D_MODEL), BF16),
        compiler_params=_cparams(("arbitrary", "arbitrary")),
        name="na_attention",
    )(q, k, v, cache_k, cache_v, ba, bm, bc)


CONV_PAD = 16
CONV_ROWS = 128
CONV_LANES = 128


def _conv_tail_kernel(x_ref, u_ref, wdw_ref, bdw_ref, lng_ref, lnb_ref, w_ref, b_ref, mod_ref,
                      o_ref, pad_ref, a_ref):
    half = CONV_WIDTH // 2
    win = CONV_ROWS + 8
    zeros = jnp.zeros((CONV_PAD, D_MODEL), F32)

    def conv_sequence(base, seq_len):
        pad_ref[0:CONV_PAD, :] = zeros
        pad_ref[CONV_PAD + seq_len:2 * CONV_PAD + seq_len, :] = zeros
        pad_ref[CONV_PAD:CONV_PAD + seq_len, :] = u_ref[base:base + seq_len, :]

        def rows(i, carry):
            r0 = pl.multiple_of(i * CONV_ROWS, CONV_ROWS)
            strips = []
            for l0 in range(0, D_MODEL, CONV_LANES):
                lanes = slice(l0, l0 + CONV_LANES)
                acc = jnp.zeros((CONV_ROWS, CONV_LANES), F32) + bdw_ref[:, lanes]
                for b in range(8):
                    z = None
                    for a in range(-2, 2):
                        t = 8 * a + b + half
                        if not 0 <= t < CONV_WIDTH:
                            continue
                        term = wdw_ref[t:t + 1, lanes] * pad_ref[pl.ds(r0 + (CONV_PAD + 8 * a), win), lanes]
                        z = term if z is None else z + term
                    if b:
                        z = pltpu.roll(z, win - b, 0)
                    acc = acc + z[:CONV_ROWS]
                strips.append(acc)
            acc = jnp.concatenate(strips, axis=1)
            mu = jnp.mean(acc, axis=-1, keepdims=True)
            cen = acc - mu
            var = jnp.mean(cen * cen, axis=-1, keepdims=True)
            y = cen * lax.rsqrt(var + NORM_EPS) * lng_ref[...] + lnb_ref[...]
            a_ref[pl.ds(base + r0, CONV_ROWS), :] = (y * _sigmoid(y)).astype(BF16)
            return carry

        lax.fori_loop(0, seq_len // CONV_ROWS, rows, 0)

    @pl.when(pl.program_id(0) < N_CTX_GROUPS)
    def _():
        for s in range(TM // SEQ):
            conv_sequence(s * SEQ, SEQ)

    @pl.when(pl.program_id(0) >= N_CTX_GROUPS)
    def _():
        conv_sequence(0, DEC_SEQ)

    for rows in _row_slabs():
        r = jnp.dot(a_ref[rows, :], w_ref[...], preferred_element_type=F32) + b_ref[...]
        o_ref[rows, :] = x_ref[rows, :] + _mod_slice(mod_ref, 2) * r


def _conv_tail(x, u, w_dw, b_dw, ln_g, ln_b, w_pw2, b_pw2, mod):
    tok = pl.BlockSpec((TM, D_MODEL), lambda g: (g, 0))
    const = lambda g: (0, 0)
    vec = pl.BlockSpec((1, D_MODEL), const)
    return pl.pallas_call(
        _conv_tail_kernel,
        grid=(N_GROUPS,),
        in_specs=[tok, tok, pl.BlockSpec((CONV_WIDTH, D_MODEL), const), vec, vec, vec,
                  pl.BlockSpec((D_MODEL, D_MODEL), const), vec,
                  pl.BlockSpec((1, 1, 6 * D_MODEL), lambda g: (g, 0, 0))],
        out_specs=tok,
        out_shape=jax.ShapeDtypeStruct((N_TOK, D_MODEL), F32),
        scratch_shapes=[pltpu.VMEM((TM + 2 * CONV_PAD, D_MODEL), F32),
                        pltpu.VMEM((TM, D_MODEL), BF16)],
        compiler_params=_cparams(("arbitrary",)),
        name="conv_tail",
    )(x, u, w_dw, b_dw.reshape(1, -1), ln_g.reshape(1, -1), ln_b.reshape(1, -1), w_pw2,
      b_pw2.reshape(1, -1), mod)


def _log_sigmoid(z):
    return -(jnp.maximum(-z, 0.0) + jnp.log(1.0 + jnp.exp(-jnp.abs(z))))


def _gla_scan_kernel(*refs, seq_len, has_init, emit_state):
    it = iter(refs)
    q_ref, k_ref, v_ref, r_ref, w2_ref, bgk_ref = (next(it) for _ in range(6))
    s0_refs = (next(it), next(it)) if has_init else None
    o_ref = next(it)
    st_out = (next(it), next(it)) if emit_state else None
    g_ref, cum_ref, st_ref = (next(it) for _ in range(3))

    c = GLA_CHUNK
    n_chunks = seq_len // c
    rlow = r_ref[...].astype(BF16)
    row = lax.broadcasted_iota(jnp.int32, (c, 1), 0)
    sub8 = lax.broadcasted_iota(jnp.int32, (8, 1), 0)
    ri = lax.broadcasted_iota(jnp.int32, (c, c), 0)
    ci = lax.broadcasted_iota(jnp.int32, (c, c), 1)
    pair_xor = ri ^ ci

    for d in range(2):
        reverse = d == 1
        z = jnp.dot(rlow, w2_ref[d].astype(BF16), preferred_element_type=F32) + bgk_ref[d]
        g_ref[...] = _log_sigmoid(z) * (1.0 / GLA_GATE_NORM)
        if has_init:
            st_ref[...] = s0_refs[d][0, 0].T
        else:
            st_ref[...] = jnp.zeros_like(st_ref)

        order = range(n_chunks - 1, -1, -1) if reverse else range(n_chunks)
        for ch in order:
            base = ch * c
            rows = slice(base, base + c)
            q = q_ref[rows, :]
            k = k_ref[rows, :]
            v = v_ref[rows, :].astype(BF16)
            cum = g_ref[rows, :]
            sh = 1
            while sh < c:
                if reverse:
                    cum = cum + jnp.where(row < c - sh, pltpu.roll(cum, c - sh, 0), 0.0)
                else:
                    cum = cum + jnp.where(row >= sh, pltpu.roll(cum, sh, 0), 0.0)
                sh *= 2
            cum_ref[...] = cum

            att = None
            m = c
            while m >= 2:
                half = m // 2
                arow = half if reverse else half - 1
                if m >= 8:
                    pieces = [jnp.broadcast_to(cum_ref[b * m + arow:b * m + arow + 1, :], (m, GLA_DK))
                              for b in range(c // m)]
                else:
                    pieces = []
                    for t in range(c // 8):
                        tile = None
                        for b in range(8 // m - 1, -1, -1):
                            r = t * 8 + b * m + arow
                            cand = jnp.broadcast_to(cum_ref[r:r + 1, :], (8, GLA_DK))
                            tile = cand if tile is None else jnp.where(sub8 < (b + 1) * m, cand, tile)
                        pieces.append(tile)
                anchor = pieces[0] if len(pieces) == 1 else jnp.concatenate(pieces, axis=0)
                fac = jnp.exp(-jnp.abs(cum - anchor))
                p = _nt_dot((q * fac).astype(BF16), (k * fac).astype(BF16))
                att = p if att is None else jnp.where(pair_xor < m, p, att)
                m //= 2
            att = jnp.where(ri == ci, jnp.sum(q * k, axis=-1, keepdims=True), att)
            att = jnp.where((ri <= ci) if reverse else (ri >= ci), att, 0.0)

            last = 0 if reverse else c - 1
            total = cum_ref[last:last + 1, :]
            q_in = (q * jnp.exp(cum)).astype(BF16)
            k_out = (k * jnp.exp(total - cum)).astype(BF16)
            st = st_ref[...]
            o = (jnp.dot(att.astype(BF16), v, preferred_element_type=F32)
                 + _nt_dot(q_in, st.astype(BF16)))
            if reverse:
                o_ref[rows, :] += o
            else:
                o_ref[rows, :] = o
            kv = lax.dot_general(v, k_out, (((0,), (0,)), ((), ())), preferred_element_type=F32)
            st_ref[...] = st * jnp.exp(total) + kv

        if emit_state:
            st_out[d][0, 0] = st_ref[...].T


def _gla_scan(y, w2pad, b_gk, init_states, seq_len, blk0, n_seq, emit_state):
    has_init = init_states is not None
    qk = lambda off: pl.BlockSpec((seq_len, GLA_DK), lambda b, h: (b + blk0, h + off))
    state = pl.BlockSpec((1, 1, GLA_DK, GLA_DV), lambda b, h: (b, h, 0, 0))
    in_specs = [
        qk(0), qk(GLA_KEY_DIM // GLA_DK),
        pl.BlockSpec((seq_len, GLA_DV), lambda b, h: (b + blk0, h + 2 * GLA_KEY_DIM // GLA_DV)),
        pl.BlockSpec((seq_len, 128), lambda b, h: (b + blk0, (2 * GLA_KEY_DIM + 2 * GLA_VAL_DIM) // 128)),
        pl.BlockSpec((2, 128, GLA_DK), lambda b, h: (0, 0, h)),
        pl.BlockSpec((2, 1, GLA_DK), lambda b, h: (0, 0, h)),
    ]
    args = [y, y, y, y, w2pad, b_gk]
    if has_init:
        in_specs += [state, state]
        args += list(init_states)
    out_specs = [pl.BlockSpec((seq_len, GLA_DV), lambda b, h: (b, h))]
    out_shape = [jax.ShapeDtypeStruct((n_seq * seq_len, GLA_VAL_DIM), F32)]
    if emit_state:
        out_specs += [state, state]
        out_shape += [jax.ShapeDtypeStruct((n_seq, GLA_HEADS, GLA_DK, GLA_DV), F32)] * 2
    return pl.pallas_call(
        functools.partial(_gla_scan_kernel, seq_len=seq_len, has_init=has_init, emit_state=emit_state),
        grid=(n_seq, GLA_HEADS),
        in_specs=in_specs,
        out_specs=out_specs,
        out_shape=out_shape,
        scratch_shapes=[pltpu.VMEM((seq_len, GLA_DK), F32), pltpu.VMEM((GLA_CHUNK, GLA_DK), F32),
                        pltpu.VMEM((GLA_DV, GLA_DK), F32)],
        compiler_params=_cparams(("arbitrary", "arbitrary")),
        name="gla_scan",
    )(*args)


_GROUP_COND_ROW = np.array([0] * N_CTX_GROUPS + list(range(1, DEC_BATCH + 1)))


def kernel(x_prompt, x_sample, c, cache_attn_k, cache_attn_v, state_gla_fwd, state_gla_bwd, c_ctx,
           mod_w, mod_b, norm1_g, norm2_g,
           attn_w_qkv, attn_w_o, attn_q_norm, attn_k_norm, attn_rpb,
           conv_w_pw1, conv_b_pw1, conv_w_dw, conv_b_dw, conv_ln_g, conv_ln_b, conv_w_pw2, conv_b_pw2,
           gla_w_q, gla_w_k, gla_w_v, gla_w_g, gla_w_gk1, gla_w_gk2, gla_b_gk, gla_o_norm, gla_w_o,
           ffn_w_up, ffn_b_up, ffn_w_dw, ffn_b_dw, ffn_w_down, ffn_b_down):
    x_ctx0 = x_prompt.reshape(N_CTX_TOK, D_MODEL)
    x_lat0 = x_sample.reshape(N_LAT_TOK, D_MODEL)
    x = None
    cond =jnp.concatenate([c_ctx[None, :], c, jnp.zeros((N_COND - 1 - DEC_BATCH, D_MODEL), F32)], axis=0)
    mod_all = _modulation(cond, mod_w, mod_b)
    mod_all = mod_all[:, _GROUP_COND_ROW][:, :, None, :]

    cache_k = cache_attn_k.reshape(DEC_BATCH, -1, PAST_LEN, D_MODEL)
    cache_v = cache_attn_v.reshape(DEC_BATCH, -1, PAST_LEN, D_MODEL)
    ffn_w_up_bf16 = ffn_w_up.astype(BF16)
    ffn_w_down_bf16 = ffn_w_down.astype(BF16)
    new_k, new_v, new_sf, new_sb = [], [], [], []
    for i in range(DEPTH):
        kind, j = i % N_MIXERS, i // N_MIXERS
        mod = mod_all[i]
        n1 = norm1_g[i].reshape(1, D_MODEL)
        if kind == 0:
            w_qkv = attn_w_qkv[j].astype(BF16)
            gain = jnp.concatenate([jnp.tile(attn_q_norm[j], NA_HEADS) * (NA_HEAD_DIM ** -0.5),
                                    jnp.tile(attn_k_norm[j], NA_HEADS),
                                    jnp.ones((D_MODEL,), F32)]).reshape(1, 3 * D_MODEL)
            if x is None:
                x_ctx, x_lat, lat_block0 = x_ctx0, x_lat0, 0
            else:
                x_ctx, x_lat, lat_block0 = x, x, N_CTX_GROUPS
            n_lat_groups = N_GROUPS - N_CTX_GROUPS
            qp, kp, vp = _qkv_proj(x_ctx, n1, mod, w_qkv, gain, 0, 0, N_CTX_GROUPS, True)
            qs, ks, vs = _qkv_proj(x_lat, n1, mod, w_qkv, gain, lat_block0, N_CTX_GROUPS, n_lat_groups, False)
            o_ctx = _ctx_attention(qp, kp, vp)
            o_lat = _na_attention(qs, ks, vs, cache_k, cache_v, j, _na_bias_tables(attn_rpb[j]))
            x = _out_proj(x_ctx, x_lat, o_ctx, o_lat, attn_w_o[j].astype(BF16), mod)
            new_k.append(kp)
            new_v.append(vp)
        elif kind == 1:
            u = _glu_proj(x, n1, mod, conv_w_pw1[j].astype(BF16), conv_b_pw1[j])
            x = _conv_tail(x, u, conv_w_dw[j], conv_b_dw[j], conv_ln_g[j], conv_ln_b[j],
                           conv_w_pw2[j].astype(BF16), conv_b_pw2[j], mod)
        else:
            pad = GLA_PROJ_N - 2 * GLA_KEY_DIM - 2 * GLA_VAL_DIM - 2 * GLA_GATE_RANK
            w_cat = jnp.concatenate([gla_w_q[j], gla_w_k[j], gla_w_v[j], gla_w_g[j], gla_w_gk1[j, 0],
                                     gla_w_gk1[j, 1], jnp.zeros((D_MODEL, pad), F32)], axis=1).astype(BF16)
            colscale = jnp.concatenate([jnp.full((GLA_KEY_DIM,), GLA_DK ** -0.5, F32),
                                        jnp.ones((GLA_PROJ_N - GLA_KEY_DIM,), F32)]).reshape(1, GLA_PROJ_N)
            y = _scaled_proj(x, n1, mod, w_cat, colscale)
            w2pad = jnp.zeros((2, 128, GLA_KEY_DIM), F32)
            w2pad = w2pad.at[0, :GLA_GATE_RANK].set(gla_w_gk2[j, 0])
            w2pad = w2pad.at[1, GLA_GATE_RANK:2 * GLA_GATE_RANK].set(gla_w_gk2[j, 1])
            b_gk = gla_b_gk[j].reshape(2, 1, GLA_KEY_DIM)
            o_ctx, sf, sb = _gla_scan(y, w2pad, b_gk, None, SEQ, 0, BATCH, True)
            (o_lat,) = _gla_scan(y, w2pad, b_gk, (state_gla_fwd[:, j], state_gla_bwd[:, j]),
                                 DEC_SEQ, N_CTX_GROUPS, DEC_BATCH, False)
            x = _gla_out_proj(x, o_ctx, o_lat, y, gla_o_norm[j].reshape(1, GLA_DV),
                              gla_w_o[j].astype(BF16), mod)
            new_sf.append(sf)
            new_sb.append(sb)
        ffn_args = (norm2_g[i].reshape(1, D_MODEL), mod, i, ffn_w_up_bf16, ffn_b_up[i],
                    ffn_w_dw[i], ffn_b_dw[i], ffn_w_down_bf16, ffn_b_down[i])
        if i < DEPTH - 1:
            x = _ffn(x, *ffn_args)
        else:
            y_ctx = _ffn(x, *ffn_args, group0=0, n_groups=N_CTX_GROUPS)
            y_lat = _ffn(x, *ffn_args, group0=N_CTX_GROUPS, n_groups=N_GROUPS - N_CTX_GROUPS)

    y_prompt = y_ctx.reshape(BATCH, SEQ, D_MODEL)
    y_sample = y_lat.reshape(DEC_BATCH, DEC_SEQ, D_MODEL)
    def cache_layout(parts):
        stacked = jnp.stack(parts, axis=1).reshape(BATCH, len(parts), NA_HEADS, NA_HEAD_DIM, SEQ)
        return stacked.transpose(0, 1, 4, 2, 3)

    return (y_prompt, y_sample, cache_layout(new_k), cache_layout(new_v),
            jnp.stack(new_sf, axis=1), jnp.stack(new_sb, axis=1))
```
